```python
import math
import jax, jax.numpy as jnp
from jax import lax
import numpy as np

D_MODEL = 2048
BATCH = 8
SEQ = 2048
DEPTH = 2
DEC_BATCH = 32
DEC_SEQ = 4
PAST_LEN = 8192
PAGE_SIZE = 128

D_HEAD = 128
ROPE_DIM = D_HEAD // 4
ROPE_THETA = 500000.0
DA_HEADS = 4
DA_KV = 2
NSA_HEADS = 8
NSA_KV = 2
CMP_STRIDE = 16
CMP_LEN = 2 * CMP_STRIDE
SEL_BLOCK = 64
SEL_TOPK = 16
WINDOW = 512
SB_HEADS = 8
SB_KV = 4
D_FF = ((8 * D_MODEL // 3 + 127) // 128) * 128
CONV_W = 3
PLE_DIM = 256
Q_BLOCK = 128
SEL_Q_BLOCK = 32
NEG_INF = -1e30
FORCED = 1e30
EPS = 1e-6

DA_QW = DA_HEADS * 2 * D_HEAD
DA_KW = DA_KV * 2 * D_HEAD
NSA_QW = NSA_HEADS * D_HEAD
NSA_KW = NSA_KV * D_HEAD
SB_QW = SB_HEADS * D_HEAD
SB_KW = SB_KV * D_HEAD
SPLITS = (DA_QW, DA_KW, DA_KW, NSA_QW, 6 * NSA_KW, 3 * NSA_HEADS, SB_QW, SB_KW, SB_KW, 3 * D_MODEL)
SPLIT_AT = tuple(sum(SPLITS[:i + 1]) for i in range(len(SPLITS) - 1))
IN_WIDTH = sum(SPLITS)

kernel_name = 'hybrid_diff_nsa_stickbreak_decoder_step'


def rmsnorm(x, g):
    xf = x.astype(jnp.float32)
    y = xf * lax.rsqrt(jnp.mean(xf * xf, axis=-1, keepdims=True) + EPS)
    return (y * g.astype(jnp.float32)).astype(x.dtype)


def masked_softmax(s, mask):
    return jax.nn.softmax(jnp.where(mask, s, NEG_INF), axis=-1)


def rope(x, pos):
    half = ROPE_DIM // 2
    inv = ROPE_THETA ** (-(jnp.arange(half, dtype=jnp.float32) * 2.0 / ROPE_DIM))
    ang = pos.astype(jnp.float32)[:, None] * inv[None, :]
    ang = ang.reshape((1, ang.shape[0]) + (1,) * (x.ndim - 3) + (half,))
    cos, sin = jnp.cos(ang), jnp.sin(ang)
    xf = x.astype(jnp.float32)
    x1, x2 = xf[..., :half], xf[..., half:ROPE_DIM]
    out = jnp.concatenate([x1 * cos - x2 * sin, x2 * cos + x1 * sin, xf[..., ROPE_DIM:]], axis=-1)
    return out.astype(x.dtype)


def over_query_blocks(fn, block, qpos, *qs):
    t = qpos.shape[0]
    if t <= block or t % block:
        return fn(qpos, *qs)
    n = t // block
    split = lambda a: jnp.moveaxis(a.reshape((a.shape[0], n, block) + a.shape[2:]), 1, 0)
    out = lax.map(lambda a: fn(*a), (qpos.reshape(n, block),) + tuple(split(q) for q in qs))
    out = jnp.moveaxis(out, 0, 1)
    return out.reshape((out.shape[0], t) + out.shape[3:])


def pad_rows(a, multiple):
    extra = (-a.shape[1]) % multiple
    if extra == 0:
        return a
    return jnp.pad(a, ((0, 0), (0, extra)) + ((0, 0),) * (a.ndim - 2))


def diff_attention(kpos, k, v, lam, sub_g, lam_init):
    scale = D_HEAD ** -0.5
    def fn(qpos, q):
        s = jnp.einsum('btgrcd,bsgcd->bgrcts', q, k, preferred_element_type=jnp.float32) * scale
        p = masked_softmax(s, kpos[None, :] <= qpos[:, None])
        a = p[:, :, :, 0] - lam * p[:, :, :, 1]
        o = jnp.einsum('bgrts,bsgd->btgrd', a.astype(v.dtype), v)
        return rmsnorm(o, sub_g) * (1.0 - lam_init)
    return fn


def stick_breaking(kpos, k, v):
    scale = D_HEAD ** -0.5
    def fn(qpos, q):
        z = jnp.einsum('btgrd,bsgd->bgrts', q, k, preferred_element_type=jnp.float32) * scale
        strict = kpos[None, :] < qpos[:, None]
        log_keep = jnp.where(strict, jax.nn.log_sigmoid(-z), 0.0)
        later = lax.cumsum(log_keep, axis=z.ndim - 1, reverse=True) - log_keep
        a = jnp.where(strict, jnp.exp(jax.nn.log_sigmoid(z) + later), 0.0)
        return jnp.einsum('bgrts,bsgd->btgrd', a.astype(v.dtype), v)
    return fn


def nsa_compress(k, w1, w2, pe):
    b, l, g, d = k.shape
    chunks = k.reshape(b, l // CMP_STRIDE, CMP_STRIDE, g, d).transpose(0, 1, 3, 2, 4)
    chunks = chunks.reshape(b, l // CMP_STRIDE, g, CMP_STRIDE * d)
    half = CMP_STRIDE * d
    lead = chunks @ w1[:half]
    trail = chunks @ w1[half:]
    h = jax.nn.gelu(lead[:, :-1] + trail[:, 1:] + pe.reshape(-1) @ w1, approximate=True)
    return h @ w2


def nsa_cmp_sel(kc, vc, ks, vs):
    b, n_keys, g, d = ks.shape
    n_cmp = kc.shape[1]
    n_sel = n_keys // SEL_BLOCK
    top = min(SEL_TOPK, n_sel)
    scale = D_HEAD ** -0.5
    cmp_start = jnp.arange(n_cmp) * CMP_STRIDE
    cmp_end = cmp_start + CMP_LEN - 1
    blk = jnp.arange(n_sel)
    cover = ((cmp_start[:, None] <= blk[None, :] * SEL_BLOCK + SEL_BLOCK - 1)
             & (cmp_end[:, None] >= blk[None, :] * SEL_BLOCK)).astype(jnp.float32)
    ksb = ks.reshape(b, n_sel, SEL_BLOCK, g, d).transpose(0, 3, 1, 2, 4)
    vsb = vs.reshape(b, n_sel, SEL_BLOCK, g, d).transpose(0, 3, 1, 2, 4)
    bi = jnp.arange(b)[:, None, None, None]
    gi = jnp.arange(g)[None, None, :, None]
    offs = jnp.arange(SEL_BLOCK)

    def fn(qpos, q, q_rot, g_cmp, g_sel):
        tq = qpos.shape[0]
        c_ok = (cmp_end[None, :] <= qpos[:, None])[None, :, None, None, :]
        s = jnp.einsum('btgrd,bngd->btgrn', q, kc, preferred_element_type=jnp.float32) * scale
        p = jnp.where(c_ok, masked_softmax(s, c_ok), 0.0)
        o_cmp = jnp.einsum('btgrn,bngd->btgrd', p.astype(vc.dtype), vc)
        imp = jnp.einsum('btgn,ns->btgs', p.sum(axis=3), cover)
        cur = qpos // SEL_BLOCK
        forced = (blk[None, :] == 0) | (blk[None, :] == cur[:, None]) | (blk[None, :] == cur[:, None] - 1)
        causal = blk[None, :] * SEL_BLOCK <= qpos[:, None]
        score = jnp.where(forced[None, :, None, :], FORCED,
                          jnp.where(causal[None, :, None, :], imp, NEG_INF))
        idx = lax.top_k(score, top)[1]
        gk = ksb[bi, gi, idx].reshape(b, tq, g, top * SEL_BLOCK, d)
        gv = vsb[bi, gi, idx].reshape(b, tq, g, top * SEL_BLOCK, d)
        kpos = (idx[..., None] * SEL_BLOCK + offs).reshape(b, tq, g, 1, top * SEL_BLOCK)
        ss = jnp.einsum('btgrd,btgmd->btgrm', q_rot, gk, preferred_element_type=jnp.float32) * scale
        ps = masked_softmax(ss, kpos <= qpos[None, :, None, None, None])
        o_sel = jnp.einsum('btgrm,btgmd->btgrd', ps.astype(gv.dtype), gv)
        return g_cmp[..., None] * o_cmp + g_sel[..., None] * o_sel
    return fn


def banded_attn(q, kv, qpos, kpos):
    s = jnp.einsum('bnqgrd,bnkgd->bngrqk', q, kv[..., 0, :], preferred_element_type=jnp.float32) * D_HEAD ** -0.5
    dist = qpos[:, :, None] - kpos[:, None, :]
    ok = (dist >= 0) & (dist < WINDOW) & (kpos[:, None, :] >= 0)
    p = masked_softmax(s, ok[None, :, None, None])
    return jnp.einsum('bngrqk,bnkgd->bnqgrd', p.astype(kv.dtype), kv[..., 1, :])


def window_prompt(q, kv):
    b, t = q.shape[:2]
    nb = t // Q_BLOCK
    kvp = jnp.pad(kv, ((0, 0), (WINDOW, 0), (0, 0), (0, 0), (0, 0)))
    idx = jnp.arange(nb)[:, None] * Q_BLOCK + jnp.arange(WINDOW + Q_BLOCK)[None, :]
    qpos = jnp.arange(nb)[:, None] * Q_BLOCK + jnp.arange(Q_BLOCK)[None, :]
    o = banded_attn(q.reshape((b, nb, Q_BLOCK) + q.shape[2:]), kvp[:, idx], qpos, idx - WINDOW)
    return o.reshape(q.shape)


def window_sample(q, pos, kv_all, start):
    kpos = start + jnp.arange(kv_all.shape[1])
    return banded_attn(q[:, None], kv_all[:, None], pos[None], kpos[None])[:, 0]


def layer(x, pe, past, li, ln1, w_in, diff_lambda, diff_subln, cmp_w1, cmp_w2, cmp_pe,
          w_br_a, w_br_b, w_br_c, w_out, ln2, w_ff_gate, w_ff_up, w_ff_down, ff_conv_w, ff_conv_b,
          ln3, w_ple, w_ple_gate):
    b, t, _ = x.shape
    p_len = 0 if past is None else past[0].shape[1]
    pos = p_len + jnp.arange(t, dtype=jnp.int32)

    def with_past(new, j):
        return new if past is None else jnp.concatenate([past[j], new], axis=1)

    h = rmsnorm(x, ln1)
    da_q, da_k, da_v, nq, nkv, ng, sq, sk, sv, bg = jnp.split(h @ w_in, SPLIT_AT, axis=-1)

    da_q = rope(da_q.reshape(b, t, DA_KV, DA_HEADS // DA_KV, 2, D_HEAD), pos)
    da_k = rope(da_k.reshape(b, t, DA_KV, 2, D_HEAD), pos).reshape(b, t, DA_KV, 2 * D_HEAD)
    da_new = jnp.stack([da_k, da_v.reshape(b, t, DA_KV, 2 * D_HEAD)], axis=3)
    da_all = with_past(da_new, 0)
    n_keys = da_all.shape[1]
    lam_init = 0.8 - 0.6 * math.exp(-0.3 * li)
    lp = diff_lambda.astype(jnp.float32)
    lam = jnp.exp(jnp.sum(lp[0] * lp[1])) - jnp.exp(jnp.sum(lp[2] * lp[3])) + lam_init
    da_fn = diff_attention(jnp.arange(n_keys), da_all[:, :, :, 0].reshape(b, n_keys, DA_KV, 2, D_HEAD),
                           da_all[:, :, :, 1], lam, diff_subln, lam_init)
    o_da = over_query_blocks(da_fn, Q_BLOCK, pos, da_q).reshape(b, t, DA_QW)

    rn = NSA_HEADS // NSA_KV
    nq = nq.reshape(b, t, NSA_KV, rn, D_HEAD)
    nq_rot = rope(nq, pos)
    kc, vc, ks, vs, kw, vw = [a.reshape(b, t, NSA_KV, D_HEAD) for a in jnp.split(nkv, 6, axis=-1)]
    cmp_new = jnp.stack([kc, vc], axis=3)
    sel_new = jnp.stack([rope(ks, pos), vs], axis=3)
    win_new = jnp.stack([rope(kw, pos), vw], axis=3)
    cmp_all = pad_rows(with_past(cmp_new, 1), SEL_BLOCK)
    sel_all = pad_rows(with_past(sel_new, 2), SEL_BLOCK)
    kc_blk = nsa_compress(cmp_all[:, :, :, 0], cmp_w1[0], cmp_w2[0], cmp_pe[0])
    vc_blk = nsa_compress(cmp_all[:, :, :, 1], cmp_w1[1], cmp_w2[1], cmp_pe[1])
    gates = jax.nn.sigmoid(ng.reshape(b, t, NSA_KV, rn, 3))
    cs_fn = nsa_cmp_sel(kc_blk, vc_blk, sel_all[:, :, :, 0], sel_all[:, :, :, 1])
    o_cs = over_query_blocks(cs_fn, SEL_Q_BLOCK, pos, nq, nq_rot, gates[..., 0], gates[..., 1])
    if past is None:
        win_all = win_new
        o_win = window_prompt(nq_rot, win_new)
    else:
        win_all = with_past(win_new, 4)
        o_win = window_sample(nq_rot, pos, win_all, p_len - past[4].shape[1])
    o_nsa = (o_cs + gates[..., 2:] * o_win).reshape(b, t, NSA_QW)
    win_state = win_all[:, win_all.shape[1] - min(WINDOW, win_all.shape[1]):]

    sq = sq.reshape(b, t, SB_KV, SB_HEADS // SB_KV, D_HEAD)
    sb_new = jnp.stack([sk.reshape(b, t, SB_KV, D_HEAD), sv.reshape(b, t, SB_KV, D_HEAD)], axis=3)
    sb_all = with_past(sb_new, 3)
    sb_fn = stick_breaking(jnp.arange(sb_all.shape[1]), sb_all[:, :, :, 0], sb_all[:, :, :, 1])
    o_sb = over_query_blocks(sb_fn, Q_BLOCK, pos, sq).reshape(b, t, SB_QW)

    g_a, g_b, g_c = jnp.split(jax.nn.sigmoid(bg), 3, axis=-1)
    merged = g_a * (o_da @ w_br_a) + g_b * (o_nsa @ w_br_b) + g_c * (o_sb @ w_br_c)
    x = x + merged @ w_out

    h = rmsnorm(x, ln2)
    gate_in = h @ w_ff_gate
    prev = jnp.zeros((b, CONV_W - 1, D_FF), gate_in.dtype) if past is None else past[5]
    gp = jnp.concatenate([prev, gate_in], axis=1)
    conv = ff_conv_b
    for i in range(CONV_W):
        conv = conv + ff_conv_w[i] * gp[:, i:i + t]
    x = x + (jax.nn.gelu(conv, approximate=True) * (h @ w_ff_up)) @ w_ff_down
    conv_state = gp[:, t:]

    x = x + jax.nn.sigmoid(rmsnorm(x, ln3) @ w_ple_gate) * (pe @ w_ple)
    return x, (da_new, cmp_new, sel_new, sb_new, win_state, conv_state)


def setup_inputs(seed: int = 0) -> dict:
    key = jax.random.key(seed)
    keys = iter(jax.random.split(key, 40))

    def nrm(shape, scale=1.0):
        return jax.random.normal(next(keys), shape, jnp.float32) * scale

    def gain(shape):
        return 1.0 + nrm(shape, 0.02)

    n_pages = PAST_LEN // PAGE_SIZE
    n_pool = (5 * DEC_BATCH * n_pages + 3) // 4
    win_buf = min(WINDOW, PAST_LEN)
    x_prompt = nrm((BATCH, SEQ, D_MODEL))
    x_sample = nrm((DEC_BATCH, DEC_SEQ, D_MODEL))
    cache_diff = nrm((DEPTH, n_pool, PAGE_SIZE, DA_KV, 2, 2 * D_HEAD))
    cache_cmp = nrm((DEPTH, n_pool, PAGE_SIZE, NSA_KV, 2, D_HEAD))
    cache_sel = nrm((DEPTH, n_pool, PAGE_SIZE, NSA_KV, 2, D_HEAD))
    cache_sb = nrm((DEPTH, n_pool, PAGE_SIZE, SB_KV, 2, D_HEAD))
    state_win = nrm((DEPTH, DEC_BATCH, win_buf, NSA_KV, 2, D_HEAD))
    state_conv = nrm((DEPTH, DEC_BATCH, CONV_W - 1, D_FF))
    page_table = jax.random.permutation(next(keys), n_pool)[:DEC_BATCH * n_pages]
    page_table = page_table.reshape(DEC_BATCH, n_pages).astype(jnp.int32)
    return {
        'x_prompt': x_prompt,
        'x_sample': x_sample,
        'cache_diff': cache_diff,
        'cache_cmp': cache_cmp,
        'cache_sel': cache_sel,
        'cache_sb': cache_sb,
        'state_win': state_win,
        'state_conv': state_conv,
        'page_table': page_table,
        'p_prompt': nrm((DEPTH, BATCH, SEQ, PLE_DIM)),
        'p_sample': nrm((DEPTH, DEC_BATCH, DEC_SEQ, PLE_DIM)),
        'ln1': gain((DEPTH, D_MODEL)),
        'w_in': nrm((DEPTH, D_MODEL, IN_WIDTH), D_MODEL ** -0.5),
        'diff_lambda': nrm((DEPTH, 4, D_HEAD), 0.1),
        'diff_subln': gain((DEPTH, 2 * D_HEAD)),
        'cmp_w1': nrm((DEPTH, 2, CMP_LEN * D_HEAD, D_HEAD), (CMP_LEN * D_HEAD) ** -0.5),
        'cmp_w2': nrm((DEPTH, 2, D_HEAD, D_HEAD), D_HEAD ** -0.5),
        'cmp_pe': nrm((DEPTH, 2, CMP_LEN, D_HEAD), 0.1),
        'w_br_a': nrm((DEPTH, DA_QW, D_MODEL), DA_QW ** -0.5),
        'w_br_b': nrm((DEPTH, NSA_QW, D_MODEL), NSA_QW ** -0.5),
        'w_br_c': nrm((DEPTH, SB_QW, D_MODEL), SB_QW ** -0.5),
        'w_out': nrm((DEPTH, D_MODEL, D_MODEL), D_MODEL ** -0.5),
        'ln2': gain((DEPTH, D_MODEL)),
        'w_ff_gate': nrm((DEPTH, D_MODEL, D_FF), D_MODEL ** -0.5),
        'w_ff_up': nrm((DEPTH, D_MODEL, D_FF), D_MODEL ** -0.5),
        'w_ff_down': nrm((DEPTH, D_FF, D_MODEL), D_FF ** -0.5),
        'ff_conv_w': nrm((DEPTH, CONV_W, D_FF), CONV_W ** -0.5),
        'ff_conv_b': nrm((DEPTH, D_FF), 0.01),
        'ln3': gain((DEPTH, D_MODEL)),
        'w_ple': nrm((DEPTH, PLE_DIM, D_MODEL), PLE_DIM ** -0.5),
        'w_ple_gate': nrm((DEPTH, D_MODEL, D_MODEL), D_MODEL ** -0.5),
        'ln_f': gain((D_MODEL,)),
    }


def reference(x_prompt, x_sample, cache_diff, cache_cmp, cache_sel, cache_sb, state_win, state_conv,
              page_table, p_prompt, p_sample, ln1, w_in, diff_lambda, diff_subln, cmp_w1, cmp_w2, cmp_pe,
              w_br_a, w_br_b, w_br_c, w_out, ln2, w_ff_gate, w_ff_up, w_ff_down, ff_conv_w, ff_conv_b,
              ln3, w_ple, w_ple_gate, ln_f):
    def paged(pool):
        rows = pool[page_table]
        return rows.reshape((rows.shape[0], rows.shape[1] * rows.shape[2]) + rows.shape[3:])

    xp, xs = x_prompt, x_sample
    st_p, st_s = [], []
    for i in range(DEPTH):
        w = (ln1[i], w_in[i], diff_lambda[i], diff_subln[i], cmp_w1[i], cmp_w2[i], cmp_pe[i],
             w_br_a[i], w_br_b[i], w_br_c[i], w_out[i], ln2[i], w_ff_gate[i], w_ff_up[i], w_ff_down[i],
             ff_conv_w[i], ff_conv_b[i], ln3[i], w_ple[i], w_ple_gate[i])
        xp, sp = layer(xp, p_prompt[i], None, i, *w)
        past = (paged(cache_diff[i]), paged(cache_cmp[i]), paged(cache_sel[i]), paged(cache_sb[i]),
                state_win[i], state_conv[i])
        xs, ss = layer(xs, p_sample[i], past, i, *w)
        st_p.append(sp)
        st_s.append(ss)
    diff_p, cmp_p, sel_p, sb_p, win_p, conv_p = [jnp.stack(a) for a in zip(*st_p)]
    diff_s, cmp_s, sel_s, sb_s, win_s, conv_s = [jnp.stack(a) for a in zip(*st_s)]
    y_prompt = rmsnorm(xp, ln_f)
    y_sample = rmsnorm(xs, ln_f)
    return (y_prompt, y_sample, diff_p, diff_s, cmp_p, cmp_s, sel_p, sel_s, sb_p, sb_s, win_p, win_s, conv_p, conv_s)
```

```python
import functools
import math

import jax
import jax.numpy as jnp
from jax import lax
from jax.experimental import pallas as pl
from jax.experimental.pallas import tpu as pltpu

F32 = jnp.float32
BF16 = jnp.bfloat16

D_MODEL = 2048
DEPTH = 2
PAGE_SIZE = 128
D_HEAD = 128
ROPE_DIM = D_HEAD // 4
ROPE_THETA = 500000.0
DA_HEADS = 4
DA_KV = 2
NSA_HEADS = 8
NSA_KV = 2
CMP_STRIDE = 16
CMP_LEN = 2 * CMP_STRIDE
SEL_BLOCK = 64
SEL_TOPK = 16
WINDOW = 512
SB_HEADS = 8
SB_KV = 4
D_FF = ((8 * D_MODEL // 3 + 127) // 128) * 128
D_FFP = ((D_FF + 511) // 512) * 512
CONV_W = 3
Q_BLOCK = 128
SEL_Q_BLOCK = 32
NEG_INF = -1e30
FORCED = 1e30
EPS = 1e-6

DA_QW = DA_HEADS * 2 * D_HEAD
DA_KW = DA_KV * 2 * D_HEAD
NSA_QW = NSA_HEADS * D_HEAD
NSA_KW = NSA_KV * D_HEAD
SB_QW = SB_HEADS * D_HEAD
SB_KW = SB_KV * D_HEAD
SPLITS = (DA_QW, DA_KW, DA_KW, NSA_QW, 6 * NSA_KW, 3 * NSA_HEADS, SB_QW, SB_KW, SB_KW, 3 * D_MODEL)
SPLIT_AT = tuple(sum(SPLITS[:i + 1]) for i in range(len(SPLITS) - 1))
NG_AT = SPLIT_AT[4]
NG_W = 3 * NSA_HEADS
LANE = 128
VMEM_LIMIT = 48 * 1024 * 1024


def _params(sem):
    return pltpu.CompilerParams(dimension_semantics=sem, vmem_limit_bytes=VMEM_LIMIT)


def _rms_rows(x, g):
    return x * lax.rsqrt(jnp.mean(x * x, axis=-1, keepdims=True) + EPS) * g


def _tile(n, pref):
    t = min(n, pref)
    assert n % t == 0, (n, pref)
    return t


def _norm_mm_kernel(x_ref, g_ref, w_ref, o_ref, h_ref):
    @pl.when(pl.program_id(1) == 0)
    def _():
        h_ref[...] = _rms_rows(x_ref[...], g_ref[...]).astype(BF16)
    o_ref[...] = jnp.dot(h_ref[...], w_ref[...], preferred_element_type=F32).astype(o_ref.dtype)


def norm_mm(x, g, w, out_dtype=F32, tm=512, tn=512):
    m, k = x.shape
    n = w.shape[1]
    tm, tn = _tile(m, tm), _tile(n, tn)
    return pl.pallas_call(
        _norm_mm_kernel,
        grid=(m // tm, n // tn),
        in_specs=[pl.BlockSpec((tm, k), lambda i, j: (i, 0)),
                  pl.BlockSpec((1, k), lambda i, j: (0, 0)),
                  pl.BlockSpec((k, tn), lambda i, j: (0, j))],
        out_specs=pl.BlockSpec((tm, tn), lambda i, j: (i, j)),
        out_shape=jax.ShapeDtypeStruct((m, n), out_dtype),
        scratch_shapes=[pltpu.VMEM((tm, k), BF16)],
        compiler_params=_params(("parallel", "arbitrary")),
        name="norm_mm",
    )(x, g.reshape(1, k), w)


def _resid_mm_kernel(x_ref, a_ref, w_ref, o_ref):
    o_ref[...] = x_ref[...] + jnp.dot(a_ref[...], w_ref[...], preferred_element_type=F32)


def resid_mm(x, a, w, tm=512, tn=512):
    m, k = a.shape
    n = w.shape[1]
    tm, tn = _tile(m, tm), _tile(n, tn)
    return pl.pallas_call(
        _resid_mm_kernel,
        grid=(m // tm, n // tn),
        in_specs=[pl.BlockSpec((tm, tn), lambda i, j: (i, j)),
                  pl.BlockSpec((tm, k), lambda i, j: (i, 0)),
                  pl.BlockSpec((k, tn), lambda i, j: (0, j))],
        out_specs=pl.BlockSpec((tm, tn), lambda i, j: (i, j)),
        out_shape=jax.ShapeDtypeStruct((m, n), F32),
        compiler_params=_params(("parallel", "arbitrary")),
        name="resid_mm",
    )(x, a, w)


def _merge_mm_kernel(oa_ref, ob_ref, oc_ref, ga_ref, gb_ref, gc_ref, wa_ref, wb_ref, wc_ref, o_ref):
    def br(o_r, g_r, w_r):
        y = jnp.dot(o_r[...], w_r[...], preferred_element_type=F32)
        return jax.nn.sigmoid(g_r[...]) * y
    o_ref[...] = (br(oa_ref, ga_ref, wa_ref) + br(ob_ref, gb_ref, wb_ref)
                  + br(oc_ref, gc_ref, wc_ref)).astype(o_ref.dtype)


def merge_mm(oa, ob, oc, bg, wa, wb, wc, tm=512, tn=512):
    m, k = oa.shape
    n = wa.shape[1]
    tm, tn = _tile(m, tm), _tile(n, tn)
    nj = n // tn
    o_spec = pl.BlockSpec((tm, k), lambda i, j: (i, 0))
    w_spec = pl.BlockSpec((k, tn), lambda i, j: (0, j))
    g_specs = [pl.BlockSpec((tm, tn), lambda i, j, s=s: (i, s * nj + j)) for s in range(3)]
    return pl.pallas_call(
        _merge_mm_kernel,
        grid=(m // tm, nj),
        in_specs=[o_spec, o_spec, o_spec] + g_specs + [w_spec, w_spec, w_spec],
        out_specs=pl.BlockSpec((tm, tn), lambda i, j: (i, j)),
        out_shape=jax.ShapeDtypeStruct((m, n), BF16),
        compiler_params=_params(("parallel", "arbitrary")),
        name="merge_mm",
    )(oa, ob, oc, bg, bg, bg, wa, wb, wc)


def _ffn_up_kernel(x_ref, g_ref, wg_ref, wu_ref, og_ref, ou_ref, h_ref):
    @pl.when(pl.program_id(1) == 0)
    def _():
        h_ref[...] = _rms_rows(x_ref[...], g_ref[...]).astype(BF16)
    h = h_ref[...]
    og_ref[...] = jnp.dot(h, wg_ref[...], preferred_element_type=F32)
    ou_ref[...] = jnp.dot(h, wu_ref[...], preferred_element_type=F32)


def ffn_up(x, g, wg, wu, tm=512, tn=512):
    m, k = x.shape
    n = wg.shape[1]
    tm, tn = _tile(m, tm), _tile(n, tn)
    w_spec = pl.BlockSpec((k, tn), lambda i, j: (0, j))
    o_spec = pl.BlockSpec((tm, tn), lambda i, j: (i, j))
    return pl.pallas_call(
        _ffn_up_kernel,
        grid=(m // tm, n // tn),
        in_specs=[pl.BlockSpec((tm, k), lambda i, j: (i, 0)),
                  pl.BlockSpec((1, k), lambda i, j: (0, 0)), w_spec, w_spec],
        out_specs=[o_spec, o_spec],
        out_shape=[jax.ShapeDtypeStruct((m, n), F32), jax.ShapeDtypeStruct((m, n), F32)],
        scratch_shapes=[pltpu.VMEM((tm, k), BF16)],
        compiler_params=_params(("parallel", "arbitrary")),
        name="ffn_up",
    )(x, g.reshape(1, k), wg, wu)


def _ple_kernel(x_ref, xt_ref, g_ref, wg_ref, pe_ref, wp_ref, o_ref, h_ref):
    @pl.when(pl.program_id(1) == 0)
    def _():
        h_ref[...] = _rms_rows(x_ref[...], g_ref[...]).astype(BF16)
    gate = jax.nn.sigmoid(jnp.dot(h_ref[...], wg_ref[...], preferred_element_type=F32))
    emb = jnp.dot(pe_ref[...], wp_ref[...], preferred_element_type=F32)
    o_ref[...] = xt_ref[...] + gate * emb


def ple_mm(x, g, wg, pe, wp, tm=512, tn=512):
    m, k = x.shape
    n = wg.shape[1]
    kp = pe.shape[1]
    tm, tn = _tile(m, tm), _tile(n, tn)
    return pl.pallas_call(
        _ple_kernel,
        grid=(m // tm, n // tn),
        in_specs=[pl.BlockSpec((tm, k), lambda i, j: (i, 0)),
                  pl.BlockSpec((tm, tn), lambda i, j: (i, j)),
                  pl.BlockSpec((1, k), lambda i, j: (0, 0)),
                  pl.BlockSpec((k, tn), lambda i, j: (0, j)),
                  pl.BlockSpec((tm, kp), lambda i, j: (i, 0)),
                  pl.BlockSpec((kp, tn), lambda i, j: (0, j))],
        out_specs=pl.BlockSpec((tm, tn), lambda i, j: (i, j)),
        out_shape=jax.ShapeDtypeStruct((m, n), F32),
        scratch_shapes=[pltpu.VMEM((tm, k), BF16)],
        compiler_params=_params(("parallel", "arbitrary")),
        name="ple_mm",
    )(x, x, g.reshape(1, k), wg, pe, wp)


def _rmsnorm_kernel(x_ref, g_ref, o_ref):
    o_ref[...] = _rms_rows(x_ref[...], g_ref[...])


def rmsnorm_rows(x, g, tm=512):
    m, k = x.shape
    tm = _tile(m, tm)
    return pl.pallas_call(
        _rmsnorm_kernel,
        grid=(m // tm,),
        in_specs=[pl.BlockSpec((tm, k), lambda i: (i, 0)), pl.BlockSpec((1, k), lambda i: (0, 0))],
        out_specs=pl.BlockSpec((tm, k), lambda i: (i, 0)),
        out_shape=jax.ShapeDtypeStruct((m, k), F32),
        compiler_params=_params(("parallel",)),
        name="final_rmsnorm",
    )(x, g.reshape(1, k))


def _rmsnorm(x, g):
    xf = x.astype(F32)
    y = xf * lax.rsqrt(jnp.mean(xf * xf, axis=-1, keepdims=True) + EPS)
    return (y * g.astype(F32)).astype(x.dtype)


def _masked_softmax(s, mask):
    return jax.nn.softmax(jnp.where(mask, s, NEG_INF), axis=-1)


def _rope(x, pos):
    half = ROPE_DIM // 2
    inv = ROPE_THETA ** (-(jnp.arange(half, dtype=F32) * 2.0 / ROPE_DIM))
    ang = pos.astype(F32)[:, None] * inv[None, :]
    ang = ang.reshape((1, ang.shape[0]) + (1,) * (x.ndim - 3) + (half,))
    cos, sin = jnp.cos(ang), jnp.sin(ang)
    x1, x2 = x[..., :half], x[..., half:ROPE_DIM]
    return jnp.concatenate([x1 * cos - x2 * sin, x2 * cos + x1 * sin, x[..., ROPE_DIM:]], axis=-1)


def _over_query_blocks(fn, block, qpos, *qs):
    t = qpos.shape[0]
    if t <= block or t % block:
        return fn(qpos, *qs)
    n = t // block
    split = lambda a: jnp.moveaxis(a.reshape((a.shape[0], n, block) + a.shape[2:]), 1, 0)
    out = lax.map(lambda a: fn(*a), (qpos.reshape(n, block),) + tuple(split(q) for q in qs))
    out = jnp.moveaxis(out, 0, 1)
    return out.reshape((out.shape[0], t) + out.shape[3:])


def _pad_rows(a, multiple):
    extra = (-a.shape[1]) % multiple
    if extra == 0:
        return a
    return jnp.pad(a, ((0, 0), (0, extra)) + ((0, 0),) * (a.ndim - 2))


def _diff_attention(kpos, k, v, lam, sub_g, lam_init):
    scale = D_HEAD ** -0.5
    def fn(qpos, q):
        s = jnp.einsum('btgrcd,bsgcd->bgrcts', q, k, preferred_element_type=F32) * scale
        p = _masked_softmax(s, kpos[None, :] <= qpos[:, None])
        a = p[:, :, :, 0] - lam * p[:, :, :, 1]
        o = jnp.einsum('bgrts,bsgd->btgrd', a.astype(v.dtype), v)
        return _rmsnorm(o, sub_g) * (1.0 - lam_init)
    return fn


def _stick_breaking(kpos, k, v):
    scale = D_HEAD ** -0.5
    def fn(qpos, q):
        z = jnp.einsum('btgrd,bsgd->bgrts', q, k, preferred_element_type=F32) * scale
        strict = kpos[None, :] < qpos[:, None]
        log_keep = jnp.where(strict, jax.nn.log_sigmoid(-z), 0.0)
        later = lax.cumsum(log_keep, axis=z.ndim - 1, reverse=True) - log_keep
        a = jnp.where(strict, jnp.exp(jax.nn.log_sigmoid(z) + later), 0.0)
        return jnp.einsum('bgrts,bsgd->btgrd', a.astype(v.dtype), v)
    return fn


def _nsa_compress(k, w1, w2, pe):
    b, l, g, d = k.shape
    chunks = k.reshape(b, l // CMP_STRIDE, CMP_STRIDE, g, d).transpose(0, 1, 3, 2, 4)
    chunks = chunks.reshape(b, l // CMP_STRIDE, g, CMP_STRIDE * d)
    half = CMP_STRIDE * d
    lead = chunks @ w1[:half]
    trail = chunks @ w1[half:]
    h = jax.nn.gelu(lead[:, :-1] + trail[:, 1:] + pe.reshape(-1) @ w1, approximate=True)
    return h @ w2


def _nsa_cmp_sel(kc, vc, ks, vs):
    b, n_keys, g, d = ks.shape
    n_cmp = kc.shape[1]
    n_sel = n_keys // SEL_BLOCK
    top = min(SEL_TOPK, n_sel)
    scale = D_HEAD ** -0.5
    cmp_start = jnp.arange(n_cmp) * CMP_STRIDE
    cmp_end = cmp_start + CMP_LEN - 1
    blk = jnp.arange(n_sel)
    cover = ((cmp_start[:, None] <= blk[None, :] * SEL_BLOCK + SEL_BLOCK - 1)
             & (cmp_end[:, None] >= blk[None, :] * SEL_BLOCK)).astype(F32)
    ksb = ks.reshape(b, n_sel, SEL_BLOCK, g, d).transpose(0, 3, 1, 2, 4)
    vsb = vs.reshape(b, n_sel, SEL_BLOCK, g, d).transpose(0, 3, 1, 2, 4)
    bi = jnp.arange(b)[:, None, None, None]
    gi = jnp.arange(g)[None, None, :, None]
    offs = jnp.arange(SEL_BLOCK)

    def fn(qpos, q, q_rot, g_cmp, g_sel):
        tq = qpos.shape[0]
        c_ok = (cmp_end[None, :] <= qpos[:, None])[None, :, None, None, :]
        s = jnp.einsum('btgrd,bngd->btgrn', q, kc, preferred_element_type=F32) * scale
        p = jnp.where(c_ok, _masked_softmax(s, c_ok), 0.0)
        o_cmp = jnp.einsum('btgrn,bngd->btgrd', p.astype(vc.dtype), vc)
        imp = jnp.einsum('btgn,ns->btgs', p.sum(axis=3), cover)
        cur = qpos // SEL_BLOCK
        forced = (blk[None, :] == 0) | (blk[None, :] == cur[:, None]) | (blk[None, :] == cur[:, None] - 1)
        causal = blk[None, :] * SEL_BLOCK <= qpos[:, None]
        score = jnp.where(forced[None, :, None, :], FORCED,
                          jnp.where(causal[None, :, None, :], imp, NEG_INF))
        idx = lax.top_k(score, top)[1]
        gk = ksb[bi, gi, idx].reshape(b, tq, g, top * SEL_BLOCK, d)
        gv = vsb[bi, gi, idx].reshape(b, tq, g, top * SEL_BLOCK, d)
        kpos = (idx[..., None] * SEL_BLOCK + offs).reshape(b, tq, g, 1, top * SEL_BLOCK)
        ss = jnp.einsum('btgrd,btgmd->btgrm', q_rot, gk, preferred_element_type=F32) * scale
        ps = _masked_softmax(ss, kpos <= qpos[None, :, None, None, None])
        o_sel = jnp.einsum('btgrm,btgmd->btgrd', ps.astype(gv.dtype), gv)
        return g_cmp[..., None] * o_cmp + g_sel[..., None] * o_sel
    return fn


def _banded_attn(q, kv, qpos, kpos):
    s = jnp.einsum('bnqgrd,bnkgd->bngrqk', q, kv[..., 0, :], preferred_element_type=F32) * D_HEAD ** -0.5
    dist = qpos[:, :, None] - kpos[:, None, :]
    ok = (dist >= 0) & (dist < WINDOW) & (kpos[:, None, :] >= 0)
    p = _masked_softmax(s, ok[None, :, None, None])
    return jnp.einsum('bngrqk,bnkgd->bnqgrd', p.astype(kv.dtype), kv[..., 1, :])


def _window_prompt(q, kv):
    b, t = q.shape[:2]
    nb = t // Q_BLOCK
    kvp = jnp.pad(kv, ((0, 0), (WINDOW, 0), (0, 0), (0, 0), (0, 0)))
    idx = jnp.arange(nb)[:, None] * Q_BLOCK + jnp.arange(WINDOW + Q_BLOCK)[None, :]
    qpos = jnp.arange(nb)[:, None] * Q_BLOCK + jnp.arange(Q_BLOCK)[None, :]
    o = _banded_attn(q.reshape((b, nb, Q_BLOCK) + q.shape[2:]), kvp[:, idx], qpos, idx - WINDOW)
    return o.reshape(q.shape)


def _window_sample(q, pos, kv_all, start):
    kpos = start + jnp.arange(kv_all.shape[1])
    return _banded_attn(q[:, None], kv_all[:, None], pos[None], kpos[None])[:, 0]


def _prep_weights(w_in, w_br_a, w_br_b, w_br_c, w_out, w_ff_gate, w_ff_up, w_ff_down, w_ple, w_ple_gate):
    w_main = jnp.concatenate([w_in[:, :NG_AT], w_in[:, NG_AT + NG_W:]], axis=1).astype(BF16)
    w_ng = jnp.pad(w_in[:, NG_AT:NG_AT + NG_W], ((0, 0), (0, LANE - NG_W))).astype(BF16)
    c = lambda a: a.astype(BF16)
    fpad = D_FFP - D_FF
    w_ff_gate = jnp.pad(w_ff_gate, ((0, 0), (0, fpad)))
    w_ff_up = jnp.pad(w_ff_up, ((0, 0), (0, fpad)))
    w_ff_down = jnp.pad(w_ff_down, ((0, fpad), (0, 0)))
    return (w_main, w_ng, c(w_br_a), c(w_br_b), c(w_br_c), c(w_out), c(w_ff_gate), c(w_ff_up),
            c(w_ff_down), c(w_ple), c(w_ple_gate))


def _layer(x, pe, past, li, ln1, wts, diff_lambda, diff_subln, cmp_w1, cmp_w2, cmp_pe,
           ln2, ff_conv_w, ff_conv_b, ln3):
    (w_main, w_ng, w_br_a, w_br_b, w_br_c, w_out, w_ff_gate, w_ff_up, w_ff_down, w_ple, w_ple_gate) = wts
    b, t, _ = x.shape
    m = b * t
    p_len = 0 if past is None else past[0].shape[1]
    pos = p_len + jnp.arange(t, dtype=jnp.int32)
    x2 = x.reshape(m, D_MODEL)

    def with_past(new, j):
        return new if past is None else jnp.concatenate([past[j], new], axis=1)

    proj = norm_mm(x2, ln1, w_main)
    ng = norm_mm(x2, ln1, w_ng)[:, :NG_W].reshape(b, t, NG_W)
    widths = (DA_QW, DA_KW, DA_KW, NSA_QW, 6 * NSA_KW, SB_QW, SB_KW, SB_KW, 3 * D_MODEL)
    at = tuple(sum(widths[:i + 1]) for i in range(len(widths) - 1))
    da_q, da_k, da_v, nq, nkv, sq, sk, sv, bg = jnp.split(proj.reshape(b, t, -1), at, axis=-1)

    da_q = _rope(da_q.reshape(b, t, DA_KV, DA_HEADS // DA_KV, 2, D_HEAD), pos)
    da_k = _rope(da_k.reshape(b, t, DA_KV, 2, D_HEAD), pos).reshape(b, t, DA_KV, 2 * D_HEAD)
    da_new = jnp.stack([da_k, da_v.reshape(b, t, DA_KV, 2 * D_HEAD)], axis=3)
    da_all = with_past(da_new, 0)
    n_keys = da_all.shape[1]
    lam_init = 0.8 - 0.6 * math.exp(-0.3 * li)
    lp = diff_lambda.astype(F32)
    lam = jnp.exp(jnp.sum(lp[0] * lp[1])) - jnp.exp(jnp.sum(lp[2] * lp[3])) + lam_init
    da_fn = _diff_attention(jnp.arange(n_keys), da_all[:, :, :, 0].reshape(b, n_keys, DA_KV, 2, D_HEAD),
                            da_all[:, :, :, 1], lam, diff_subln, lam_init)
    o_da = _over_query_blocks(da_fn, Q_BLOCK, pos, da_q).reshape(b, t, DA_QW)

    rn = NSA_HEADS // NSA_KV
    nq = nq.reshape(b, t, NSA_KV, rn, D_HEAD)
    nq_rot = _rope(nq, pos)
    kc, vc, ks, vs, kw, vw = [a.reshape(b, t, NSA_KV, D_HEAD) for a in jnp.split(nkv, 6, axis=-1)]
    cmp_new = jnp.stack([kc, vc], axis=3)
    sel_new = jnp.stack([_rope(ks, pos), vs], axis=3)
    win_new = jnp.stack([_rope(kw, pos), vw], axis=3)
    cmp_all = _pad_rows(with_past(cmp_new, 1), SEL_BLOCK)
    sel_all = _pad_rows(with_past(sel_new, 2), SEL_BLOCK)
    kc_blk = _nsa_compress(cmp_all[:, :, :, 0], cmp_w1[0], cmp_w2[0], cmp_pe[0])
    vc_blk = _nsa_compress(cmp_all[:, :, :, 1], cmp_w1[1], cmp_w2[1], cmp_pe[1])
    gates = jax.nn.sigmoid(ng.reshape(b, t, NSA_KV, rn, 3))
    cs_fn = _nsa_cmp_sel(kc_blk, vc_blk, sel_all[:, :, :, 0], sel_all[:, :, :, 1])
    o_cs = _over_query_blocks(cs_fn, SEL_Q_BLOCK, pos, nq, nq_rot, gates[..., 0], gates[..., 1])
    if past is None:
        win_all = win_new
        o_win = _window_prompt(nq_rot, win_new)
    else:
        win_all = with_past(win_new, 4)
        o_win = _window_sample(nq_rot, pos, win_all, p_len - past[4].shape[1])
    o_nsa = (o_cs + gates[..., 2:] * o_win).reshape(b, t, NSA_QW)
    win_state = win_all[:, win_all.shape[1] - min(WINDOW, win_all.shape[1]):]

    sq = sq.reshape(b, t, SB_KV, SB_HEADS // SB_KV, D_HEAD)
    sb_new = jnp.stack([sk.reshape(b, t, SB_KV, D_HEAD), sv.reshape(b, t, SB_KV, D_HEAD)], axis=3)
    sb_all = with_past(sb_new, 3)
    sb_fn = _stick_breaking(jnp.arange(sb_all.shape[1]), sb_all[:, :, :, 0], sb_all[:, :, :, 1])
    o_sb = _over_query_blocks(sb_fn, Q_BLOCK, pos, sq).reshape(b, t, SB_QW)

    merged = merge_mm(o_da.reshape(m, DA_QW).astype(BF16), o_nsa.reshape(m, NSA_QW).astype(BF16),
                      o_sb.reshape(m, SB_QW).astype(BF16), bg.reshape(m, 3 * D_MODEL),
                      w_br_a, w_br_b, w_br_c)
    x2 = resid_mm(x2, merged, w_out)

    gate_in, up = ffn_up(x2, ln2, w_ff_gate, w_ff_up)
    gate_in = gate_in.reshape(b, t, D_FFP)
    fpad = D_FFP - D_FF
    prev = jnp.zeros((b, CONV_W - 1, D_FFP), F32) if past is None else jnp.pad(past[5], ((0, 0), (0, 0), (0, fpad)))
    gp = jnp.concatenate([prev, gate_in], axis=1)
    conv = jnp.pad(ff_conv_b, (0, fpad))
    cw = jnp.pad(ff_conv_w, ((0, 0), (0, fpad)))
    for i in range(CONV_W):
        conv = conv + cw[i] * gp[:, i:i + t]
    act = (jax.nn.gelu(conv, approximate=True) * up.reshape(b, t, D_FFP)).astype(BF16)
    x2 = resid_mm(x2, act.reshape(m, D_FFP), w_ff_down)
    conv_state = gp[:, t:, :D_FF]

    x2 = ple_mm(x2, ln3, w_ple_gate, pe.reshape(m, -1).astype(BF16), w_ple)
    return x2.reshape(b, t, D_MODEL), (da_new, cmp_new, sel_new, sb_new, win_state, conv_state)


def kernel(x_prompt, x_sample, cache_diff, cache_cmp, cache_sel, cache_sb, state_win, state_conv, page_table, p_prompt, p_sample, ln1, w_in, diff_lambda, diff_subln, cmp_w1, cmp_w2, cmp_pe, w_br_a, w_br_b, w_br_c, w_out, ln2, w_ff_gate, w_ff_up, w_ff_down, ff_conv_w, ff_conv_b, ln3, w_ple, w_ple_gate, ln_f):
    def paged(pool):
        rows = pool[page_table]
        return rows.reshape((rows.shape[0], rows.shape[1] * rows.shape[2]) + rows.shape[3:])

    xp, xs = x_prompt, x_sample
    st_p, st_s = [], []
    for i in range(DEPTH):
        wts = _prep_weights(w_in[i], w_br_a[i], w_br_b[i], w_br_c[i], w_out[i], w_ff_gate[i], w_ff_up[i],
                            w_ff_down[i], w_ple[i], w_ple_gate[i])
        rest = (diff_lambda[i], diff_subln[i], cmp_w1[i], cmp_w2[i], cmp_pe[i], ln2[i], ff_conv_w[i],
                ff_conv_b[i], ln3[i])
        xp, sp = _layer(xp, p_prompt[i], None, i, ln1[i], wts, *rest)
        past = (paged(cache_diff[i]), paged(cache_cmp[i]), paged(cache_sel[i]), paged(cache_sb[i]),
                state_win[i], state_conv[i])
        xs, ss = _layer(xs, p_sample[i], past, i, ln1[i], wts, *rest)
        st_p.append(sp)
        st_s.append(ss)
    diff_p, cmp_p, sel_p, sb_p, win_p, conv_p = [jnp.stack(a) for a in zip(*st_p)]
    diff_s, cmp_s, sel_s, sb_s, win_s, conv_s = [jnp.stack(a) for a in zip(*st_s)]
    y_prompt = rmsnorm_rows(xp.reshape(-1, D_MODEL), ln_f).reshape(xp.shape)
    y_sample = rmsnorm_rows(xs.reshape(-1, D_MODEL), ln_f).reshape(xs.shape)
    return (y_prompt, y_sample, diff_p, diff_s, cmp_p, cmp_s, sel_p, sel_s, sb_p, sb_s, win_p, win_s, conv_p, conv_s)
```

```python
import functools
import math

import jax
import jax.numpy as jnp
from jax import lax
from jax.experimental import pallas as pl
from jax.experimental.pallas import tpu as pltpu

F32 = jnp.float32
BF16 = jnp.bfloat16

D_MODEL = 2048
DEPTH = 2
PAGE_SIZE = 128
D_HEAD = 128
ROPE_DIM = D_HEAD // 4
ROPE_THETA = 500000.0
DA_HEADS = 4
DA_KV = 2
NSA_HEADS = 8
NSA_KV = 2
CMP_STRIDE = 16
CMP_LEN = 2 * CMP_STRIDE
SEL_BLOCK = 64
SEL_TOPK = 16
WINDOW = 512
SB_HEADS = 8
SB_KV = 4
D_FF = ((8 * D_MODEL // 3 + 127) // 128) * 128
D_FFP = ((D_FF + 511) // 512) * 512
CONV_W = 3
Q_BLOCK = 128
SEL_Q_BLOCK = 32
NEG_INF = -1e30
FORCED = 1e30
EPS = 1e-6

DA_QW = DA_HEADS * 2 * D_HEAD
DA_KW = DA_KV * 2 * D_HEAD
NSA_QW = NSA_HEADS * D_HEAD
NSA_KW = NSA_KV * D_HEAD
SB_QW = SB_HEADS * D_HEAD
SB_KW = SB_KV * D_HEAD
SPLITS = (DA_QW, DA_KW, DA_KW, NSA_QW, 6 * NSA_KW, 3 * NSA_HEADS, SB_QW, SB_KW, SB_KW, 3 * D_MODEL)
SPLIT_AT = tuple(sum(SPLITS[:i + 1]) for i in range(len(SPLITS) - 1))
NG_AT = SPLIT_AT[4]
NG_W = 3 * NSA_HEADS
LANE = 128
VMEM_LIMIT = 48 * 1024 * 1024


def _params(sem):
    return pltpu.CompilerParams(dimension_semantics=sem, vmem_limit_bytes=VMEM_LIMIT)


def _rms_rows(x, g):
    return x * lax.rsqrt(jnp.mean(x * x, axis=-1, keepdims=True) + EPS) * g


def _tile(n, pref):
    t = min(n, pref)
    assert n % t == 0, (n, pref)
    return t


def _norm_mm_kernel(x_ref, g_ref, w_ref, o_ref, h_ref):
    @pl.when(pl.program_id(1) == 0)
    def _():
        h_ref[...] = _rms_rows(x_ref[...], g_ref[...]).astype(BF16)
    o_ref[...] = jnp.dot(h_ref[...], w_ref[...], preferred_element_type=F32).astype(o_ref.dtype)


def norm_mm(x, g, w, out_dtype=F32, tm=512, tn=512):
    m, k = x.shape
    n = w.shape[1]
    tm, tn = _tile(m, tm), _tile(n, tn)
    return pl.pallas_call(
        _norm_mm_kernel,
        grid=(m // tm, n // tn),
        in_specs=[pl.BlockSpec((tm, k), lambda i, j: (i, 0)),
                  pl.BlockSpec((1, k), lambda i, j: (0, 0)),
                  pl.BlockSpec((k, tn), lambda i, j: (0, j))],
        out_specs=pl.BlockSpec((tm, tn), lambda i, j: (i, j)),
        out_shape=jax.ShapeDtypeStruct((m, n), out_dtype),
        scratch_shapes=[pltpu.VMEM((tm, k), BF16)],
        compiler_params=_params(("parallel", "arbitrary")),
        name="norm_mm",
    )(x, g.reshape(1, k), w)


def _resid_mm_kernel(x_ref, a_ref, w_ref, o_ref):
    o_ref[...] = x_ref[...] + jnp.dot(a_ref[...], w_ref[...], preferred_element_type=F32)


def resid_mm(x, a, w, tm=512, tn=512):
    m, k = a.shape
    n = w.shape[1]
    tm, tn = _tile(m, tm), _tile(n, tn)
    return pl.pallas_call(
        _resid_mm_kernel,
        grid=(m // tm, n // tn),
        in_specs=[pl.BlockSpec((tm, tn), lambda i, j: (i, j)),
                  pl.BlockSpec((tm, k), lambda i, j: (i, 0)),
                  pl.BlockSpec((k, tn), lambda i, j: (0, j))],
        out_specs=pl.BlockSpec((tm, tn), lambda i, j: (i, j)),
        out_shape=jax.ShapeDtypeStruct((m, n), F32),
        compiler_params=_params(("parallel", "arbitrary")),
        name="resid_mm",
    )(x, a, w)


def _merge_mm_kernel(oa_ref, ob_ref, oc_ref, ga_ref, gb_ref, gc_ref, wa_ref, wb_ref, wc_ref, o_ref):
    def br(o_r, g_r, w_r):
        y = jnp.dot(o_r[...], w_r[...], preferred_element_type=F32)
        return jax.nn.sigmoid(g_r[...]) * y
    o_ref[...] = (br(oa_ref, ga_ref, wa_ref) + br(ob_ref, gb_ref, wb_ref)
                  + br(oc_ref, gc_ref, wc_ref)).astype(o_ref.dtype)


def merge_mm(oa, ob, oc, bg, wa, wb, wc, tm=512, tn=512):
    m, k = oa.shape
    n = wa.shape[1]
    tm, tn = _tile(m, tm), _tile(n, tn)
    nj = n // tn
    o_spec = pl.BlockSpec((tm, k), lambda i, j: (i, 0))
    w_spec = pl.BlockSpec((k, tn), lambda i, j: (0, j))
    g_specs = [pl.BlockSpec((tm, tn), lambda i, j, s=s: (i, s * nj + j)) for s in range(3)]
    return pl.pallas_call(
        _merge_mm_kernel,
        grid=(m // tm, nj),
        in_specs=[o_spec, o_spec, o_spec] + g_specs + [w_spec, w_spec, w_spec],
        out_specs=pl.BlockSpec((tm, tn), lambda i, j: (i, j)),
        out_shape=jax.ShapeDtypeStruct((m, n), BF16),
        compiler_params=_params(("parallel", "arbitrary")),
        name="merge_mm",
    )(oa, ob, oc, bg, bg, bg, wa, wb, wc)


def _ffn_up_kernel(x_ref, g_ref, wg_ref, wu_ref, og_ref, ou_ref, h_ref):
    @pl.when(pl.program_id(1) == 0)
    def _():
        h_ref[...] = _rms_rows(x_ref[...], g_ref[...]).astype(BF16)
    h = h_ref[...]
    og_ref[...] = jnp.dot(h, wg_ref[...], preferred_element_type=F32)
    ou_ref[...] = jnp.dot(h, wu_ref[...], preferred_element_type=F32)


def ffn_up(x, g, wg, wu, tm=512, tn=512):
    m, k = x.shape
    n = wg.shape[1]
    tm, tn = _tile(m, tm), _tile(n, tn)
    w_spec = pl.BlockSpec((k, tn), lambda i, j: (0, j))
    o_spec = pl.BlockSpec((tm, tn), lambda i, j: (i, j))
    return pl.pallas_call(
        _ffn_up_kernel,
        grid=(m // tm, n // tn),
        in_specs=[pl.BlockSpec((tm, k), lambda i, j: (i, 0)),
                  pl.BlockSpec((1, k), lambda i, j: (0, 0)), w_spec, w_spec],
        out_specs=[o_spec, o_spec],
        out_shape=[jax.ShapeDtypeStruct((m, n), F32), jax.ShapeDtypeStruct((m, n), F32)],
        scratch_shapes=[pltpu.VMEM((tm, k), BF16)],
        compiler_params=_params(("parallel", "arbitrary")),
        name="ffn_up",
    )(x, g.reshape(1, k), wg, wu)


def _ple_kernel(x_ref, xt_ref, g_ref, wg_ref, pe_ref, wp_ref, o_ref, h_ref):
    @pl.when(pl.program_id(1) == 0)
    def _():
        h_ref[...] = _rms_rows(x_ref[...], g_ref[...]).astype(BF16)
    gate = jax.nn.sigmoid(jnp.dot(h_ref[...], wg_ref[...], preferred_element_type=F32))
    emb = jnp.dot(pe_ref[...], wp_ref[...], preferred_element_type=F32)
    o_ref[...] = xt_ref[...] + gate * emb


def ple_mm(x, g, wg, pe, wp, tm=512, tn=512):
    m, k = x.shape
    n = wg.shape[1]
    kp = pe.shape[1]
    tm, tn = _tile(m, tm), _tile(n, tn)
    return pl.pallas_call(
        _ple_kernel,
        grid=(m // tm, n // tn),
        in_specs=[pl.BlockSpec((tm, k), lambda i, j: (i, 0)),
                  pl.BlockSpec((tm, tn), lambda i, j: (i, j)),
                  pl.BlockSpec((1, k), lambda i, j: (0, 0)),
                  pl.BlockSpec((k, tn), lambda i, j: (0, j)),
                  pl.BlockSpec((tm, kp), lambda i, j: (i, 0)),
                  pl.BlockSpec((kp, tn), lambda i, j: (0, j))],
        out_specs=pl.BlockSpec((tm, tn), lambda i, j: (i, j)),
        out_shape=jax.ShapeDtypeStruct((m, n), F32),
        scratch_shapes=[pltpu.VMEM((tm, k), BF16)],
        compiler_params=_params(("parallel", "arbitrary")),
        name="ple_mm",
    )(x, x, g.reshape(1, k), wg, pe, wp)


def _rmsnorm_kernel(x_ref, g_ref, o_ref):
    o_ref[...] = _rms_rows(x_ref[...], g_ref[...])


def rmsnorm_rows(x, g, tm=512):
    m, k = x.shape
    tm = _tile(m, tm)
    return pl.pallas_call(
        _rmsnorm_kernel,
        grid=(m // tm,),
        in_specs=[pl.BlockSpec((tm, k), lambda i: (i, 0)), pl.BlockSpec((1, k), lambda i: (0, 0))],
        out_specs=pl.BlockSpec((tm, k), lambda i: (i, 0)),
        out_shape=jax.ShapeDtypeStruct((m, k), F32),
        compiler_params=_params(("parallel",)),
        name="final_rmsnorm",
    )(x, g.reshape(1, k))


_NT = (((1,), (1,)), ((), ()))
ATT_TQ = 256
ATT_TK = 256
NSA_TQ = 128


def _iota(shape, dim):
    return lax.broadcasted_iota(jnp.int32, shape, dim)


def _split_bf16(x):
    hi = x.astype(BF16)
    lo = (x - hi.astype(F32)).astype(BF16)
    return hi, lo


def _diff_prompt_kernel(lam_ref, q_ref, k_ref, v_ref, sub_ref, o_ref, m_ref, l_ref, acc_ref, *, tq, tk, lam_init):
    i = pl.program_id(2)
    rows = 2 * tq
    qpos = i * tq + _iota((rows, 1), 0) % tq
    n_kv = (i * tq + tq + tk - 1) // tk
    for c in range(2):
        m_ref[c] = jnp.full((rows, 1), NEG_INF, F32)
        l_ref[c] = jnp.zeros((rows, 1), F32)
        acc_ref[c] = jnp.zeros((rows, 2 * D_HEAD), F32)

        def body(j, carry, c=c):
            k0 = pl.multiple_of(j * tk, tk)
            qc = jnp.concatenate([q_ref[0, :, (r * 2 + c) * D_HEAD:(r * 2 + c + 1) * D_HEAD] for r in range(2)], axis=0)
            ks = k_ref[0, pl.ds(k0, tk), c * D_HEAD:(c + 1) * D_HEAD]
            vs = v_ref[0, pl.ds(k0, tk), :]
            s = lax.dot_general(qc, ks, _NT, preferred_element_type=F32)
            ok = (k0 + _iota((1, tk), 1)) <= qpos
            s = jnp.where(ok, s, NEG_INF)
            m_old = m_ref[c]
            m_new = jnp.maximum(m_old, jnp.max(s, axis=-1, keepdims=True))
            alpha = jnp.exp(m_old - m_new)
            p = jnp.where(ok, jnp.exp(s - m_new), 0.0)
            l_ref[c] = alpha * l_ref[c] + jnp.sum(p, axis=-1, keepdims=True)
            acc_ref[c] = alpha * acc_ref[c] + jnp.dot(p.astype(BF16), vs, preferred_element_type=F32)
            m_ref[c] = m_new
            return carry

        lax.fori_loop(0, n_kv, body, 0)
    lp = lam_ref[...]
    lam = (jnp.exp(jnp.sum(lp[0:1] * lp[1:2], axis=-1, keepdims=True))
           - jnp.exp(jnp.sum(lp[2:3] * lp[3:4], axis=-1, keepdims=True)) + lam_init)
    o = acc_ref[0] / l_ref[0] - lam * (acc_ref[1] / l_ref[1])
    o = _rms_rows(o, sub_ref[...]) * (1.0 - lam_init)
    for r in range(2):
        o_ref[0, :, r * 2 * D_HEAD:(r + 1) * 2 * D_HEAD] = o[r * tq:(r + 1) * tq].astype(o_ref.dtype)


def diff_prompt(q, k, v, diff_lambda, diff_subln, lam_init, tq=ATT_TQ, tk=ATT_TK):
    b, t, _ = q.shape
    tq, tk = _tile(t, tq), _tile(t, tk)
    rows = 2 * tq
    return pl.pallas_call(
        functools.partial(_diff_prompt_kernel, tq=tq, tk=tk, lam_init=lam_init),
        grid=(b, DA_KV, t // tq),
        in_specs=[pl.BlockSpec((4, D_HEAD), lambda bi, g, i: (0, 0)),
                  pl.BlockSpec((1, tq, 4 * D_HEAD), lambda bi, g, i: (bi, i, g)),
                  pl.BlockSpec((1, t, 2 * D_HEAD), lambda bi, g, i: (bi, 0, g)),
                  pl.BlockSpec((1, t, 2 * D_HEAD), lambda bi, g, i: (bi, 0, g)),
                  pl.BlockSpec((1, 2 * D_HEAD), lambda bi, g, i: (0, 0))],
        out_specs=pl.BlockSpec((1, tq, 4 * D_HEAD), lambda bi, g, i: (bi, i, g)),
        out_shape=jax.ShapeDtypeStruct((b, t, DA_QW), BF16),
        scratch_shapes=[pltpu.VMEM((2, rows, 1), F32), pltpu.VMEM((2, rows, 1), F32),
                        pltpu.VMEM((2, rows, 2 * D_HEAD), F32)],
        compiler_params=_params(("parallel", "parallel", "arbitrary")),
        name="diff_prompt",
    )(diff_lambda.astype(F32), q, k, v, diff_subln.reshape(1, -1).astype(F32))


def _sb_prompt_kernel(q_ref, k_ref, v_ref, u_ref, o_ref, c_ref, acc_ref, *, tq, tk):
    i = pl.program_id(2)
    rows = 2 * tq
    qpos = i * tq + _iota((rows, 1), 0) % tq
    n_kv = (i * tq + tq + tk - 1) // tk
    c_ref[...] = jnp.zeros((rows, 1), F32)
    acc_ref[...] = jnp.zeros((rows, D_HEAD), F32)

    def body(jj, carry):
        j = n_kv - 1 - jj
        k0 = pl.multiple_of(j * tk, tk)
        qs = jnp.concatenate([q_ref[0, :, r * D_HEAD:(r + 1) * D_HEAD] for r in range(2)], axis=0)
        ks = k_ref[0, pl.ds(k0, tk), :]
        vs = v_ref[0, pl.ds(k0, tk), :]
        z = lax.dot_general(qs, ks, _NT, preferred_element_type=F32)
        strict = (k0 + _iota((1, tk), 1)) < qpos
        sp = jnp.maximum(z, 0.0) + jnp.log(1.0 + jnp.exp(-jnp.abs(z)))
        lk = jnp.where(strict, -sp, 0.0)
        hi, lo = _split_bf16(lk)
        u = u_ref[...]
        later = (jnp.dot(hi, u, preferred_element_type=F32) + jnp.dot(lo, u, preferred_element_type=F32)
                 + c_ref[...])
        a = jnp.where(strict, jnp.exp(z - sp + later), 0.0)
        acc_ref[...] += jnp.dot(a.astype(BF16), vs, preferred_element_type=F32)
        c_ref[...] += jnp.sum(lk, axis=-1, keepdims=True)
        return carry

    lax.fori_loop(0, n_kv, body, 0)
    for r in range(2):
        o_ref[0, :, r * D_HEAD:(r + 1) * D_HEAD] = acc_ref[r * tq:(r + 1) * tq, :].astype(o_ref.dtype)


def _suffix_matrix(tk):
    return (_iota((tk, tk), 0) > _iota((tk, tk), 1)).astype(BF16)


def sb_prompt(q, k, v, tq=ATT_TQ, tk=ATT_TK):
    b, t, _ = q.shape
    tq, tk = _tile(t, tq), _tile(t, tk)
    rows = 2 * tq
    return pl.pallas_call(
        functools.partial(_sb_prompt_kernel, tq=tq, tk=tk),
        grid=(b, SB_KV, t // tq),
        in_specs=[pl.BlockSpec((1, tq, 2 * D_HEAD), lambda bi, g, i: (bi, i, g)),
                  pl.BlockSpec((1, t, D_HEAD), lambda bi, g, i: (bi, 0, g)),
                  pl.BlockSpec((1, t, D_HEAD), lambda bi, g, i: (bi, 0, g)),
                  pl.BlockSpec((tk, tk), lambda bi, g, i: (0, 0))],
        out_specs=pl.BlockSpec((1, tq, 2 * D_HEAD), lambda bi, g, i: (bi, i, g)),
        out_shape=jax.ShapeDtypeStruct((b, t, SB_QW), BF16),
        scratch_shapes=[pltpu.VMEM((rows, 1), F32), pltpu.VMEM((rows, D_HEAD), F32)],
        compiler_params=_params(("parallel", "parallel", "arbitrary")),
        name="sb_prompt",
    )(q, k, v, _suffix_matrix(tk))


def _nsa_prompt_kernel(q_ref, qr_ref, kc_ref, vc_ref, ks_ref, vs_ref, kw_ref, vw_ref, ng_ref, cov_ref, exp_ref,
                       o_ref, sel_ref, m_ref, l_ref, acc_ref, *, tq, tk, t, n_cmp, n_sel):
    i = pl.program_id(2)
    rn = NSA_HEADS // NSA_KV
    rows = rn * tq
    q0 = i * tq
    qpos_t = q0 + _iota((tq, 1), 0)
    qpos = q0 + _iota((rows, 1), 0) % tq
    lane = _iota((1, LANE), 1)
    stack = lambda ref: jnp.concatenate([ref[0, :, r * D_HEAD:(r + 1) * D_HEAD] for r in range(rn)], axis=0)

    s = lax.dot_general(stack(q_ref), kc_ref[0, 0], _NT, preferred_element_type=F32)
    c_ok = ((lane * CMP_STRIDE + (CMP_LEN - 1)) <= qpos) & (lane < n_cmp)
    s = jnp.where(c_ok, s, NEG_INF)
    e = jnp.where(c_ok, jnp.exp(s - jnp.max(s, axis=-1, keepdims=True)), 0.0)
    den = jnp.sum(e, axis=-1, keepdims=True)
    p = e / jnp.where(den > 0.0, den, 1.0)
    o_cmp = jnp.dot(p.astype(BF16), vc_ref[0, 0], preferred_element_type=F32)
    psum = p[0:tq]
    for r in range(1, rn):
        psum = psum + p[r * tq:(r + 1) * tq]
    hi, lo = _split_bf16(psum)
    cov = cov_ref[...]
    imp = jnp.dot(hi, cov, preferred_element_type=F32) + jnp.dot(lo, cov, preferred_element_type=F32)

    cur = qpos_t // SEL_BLOCK
    forced = (lane == 0) | (lane == cur) | (lane == cur - 1)
    causal = lane * SEL_BLOCK <= qpos_t
    score = jnp.where(forced, FORCED, jnp.where(causal, imp, NEG_INF))
    score = jnp.where(lane < n_sel, score, -3e38)
    cnt = jnp.zeros((tq, LANE), F32)
    for sp in range(n_sel):
        col = score[:, sp:sp + 1]
        tie = jnp.where(lane > sp, 1.0, 0.0)
        cnt = cnt + jnp.where(col > score, 1.0, jnp.where(col == score, tie, 0.0))
    selm = jnp.where((cnt < float(min(SEL_TOPK, n_sel))) & (lane < n_sel), 1.0, 0.0).astype(BF16)
    for jj in range(t // tk):
        sel_ref[jj] = jnp.dot(selm, exp_ref[:, jj * tk:(jj + 1) * tk], preferred_element_type=F32)

    m_ref[...] = jnp.full((rows, 1), NEG_INF, F32)
    l_ref[...] = jnp.zeros((rows, 1), F32)
    acc_ref[...] = jnp.zeros((rows, D_HEAD), F32)
    n_kv = (q0 + tq + tk - 1) // tk

    def body(j, carry):
        k0 = pl.multiple_of(j * tk, tk)
        ks = ks_ref[0, pl.ds(k0, tk), :]
        vs = vs_ref[0, pl.ds(k0, tk), :]
        sj = lax.dot_general(stack(qr_ref), ks, _NT, preferred_element_type=F32)
        selx = sel_ref[j]
        ok = (jnp.concatenate([selx] * rn, axis=0) > 0.5) & ((k0 + _iota((1, tk), 1)) <= qpos)
        sj = jnp.where(ok, sj, NEG_INF)
        m_old = m_ref[...]
        m_new = jnp.maximum(m_old, jnp.max(sj, axis=-1, keepdims=True))
        alpha = jnp.exp(m_old - m_new)
        pj = jnp.where(ok, jnp.exp(sj - m_new), 0.0)
        l_ref[...] = alpha * l_ref[...] + jnp.sum(pj, axis=-1, keepdims=True)
        acc_ref[...] = alpha * acc_ref[...] + jnp.dot(pj.astype(BF16), vs, preferred_element_type=F32)
        m_ref[...] = m_new
        return carry

    lax.fori_loop(0, n_kv, body, 0)

    wlen = WINDOW + tq
    w0 = pl.multiple_of(jnp.maximum(q0 - WINDOW, 0), tq)
    kwin = kw_ref[0, pl.ds(w0, wlen), :]
    vwin = vw_ref[0, pl.ds(w0, wlen), :]
    dist = qpos_t - (w0 + _iota((1, wlen), 1))
    w_ok = (dist >= 0) & (dist < WINDOW)
    gates = jax.nn.sigmoid(ng_ref[0])
    for r in range(rn):
        sw = lax.dot_general(qr_ref[0, :, r * D_HEAD:(r + 1) * D_HEAD], kwin, _NT, preferred_element_type=F32)
        sw = jnp.where(w_ok, sw, NEG_INF)
        ew = jnp.where(w_ok, jnp.exp(sw - jnp.max(sw, axis=-1, keepdims=True)), 0.0)
        o_win = jnp.dot(ew.astype(BF16), vwin, preferred_element_type=F32) / jnp.sum(ew, axis=-1, keepdims=True)
        rs = slice(r * tq, (r + 1) * tq)
        o_sel = acc_ref[rs, :] / l_ref[rs, :]
        out = (gates[:, 3 * r:3 * r + 1] * o_cmp[rs] + gates[:, 3 * r + 1:3 * r + 2] * o_sel
               + gates[:, 3 * r + 2:3 * r + 3] * o_win)
        o_ref[0, :, r * D_HEAD:(r + 1) * D_HEAD] = out.astype(o_ref.dtype)


def nsa_prompt(q, qr, kc, vc, ks, vs, kw, vw, ng, tq=NSA_TQ, tk=ATT_TK):
    b, t, _ = q.shape
    tq, tk = _tile(t, tq), _tile(t, tk)
    n_cmp = t // CMP_STRIDE - 1
    n_sel = t // SEL_BLOCK
    assert t % SEL_BLOCK == 0 and n_cmp <= LANE and n_sel <= LANE and t >= WINDOW + tq and WINDOW % tq == 0
    rn = NSA_HEADS // NSA_KV
    rows = rn * tq
    cmp_i = _iota((LANE, LANE), 0)
    sel_i = _iota((LANE, LANE), 1)
    cover = ((cmp_i * CMP_STRIDE <= sel_i * SEL_BLOCK + SEL_BLOCK - 1)
             & (cmp_i * CMP_STRIDE + CMP_LEN - 1 >= sel_i * SEL_BLOCK)
             & (cmp_i < n_cmp) & (sel_i < n_sel)).astype(BF16)
    expand = (_iota((LANE, t), 0) == _iota((LANE, t), 1) // SEL_BLOCK).astype(BF16)
    qspec = pl.BlockSpec((1, tq, rn * D_HEAD), lambda bi, g, i: (bi, i, g))
    cspec = pl.BlockSpec((1, 1, LANE, D_HEAD), lambda bi, g, i: (bi, g, 0, 0))
    kspec = pl.BlockSpec((1, t, D_HEAD), lambda bi, g, i: (bi, 0, g))
    return pl.pallas_call(
        functools.partial(_nsa_prompt_kernel, tq=tq, tk=tk, t=t, n_cmp=n_cmp, n_sel=n_sel),
        grid=(b, NSA_KV, t // tq),
        in_specs=[qspec, qspec, cspec, cspec, kspec, kspec, kspec, kspec,
                  pl.BlockSpec((1, tq, LANE), lambda bi, g, i: (bi, i, g)),
                  pl.BlockSpec((LANE, LANE), lambda bi, g, i: (0, 0)),
                  pl.BlockSpec((LANE, t), lambda bi, g, i: (0, 0))],
        out_specs=qspec,
        out_shape=jax.ShapeDtypeStruct((b, t, NSA_QW), BF16),
        scratch_shapes=[pltpu.VMEM((t // tk, tq, tk), F32), pltpu.VMEM((rows, 1), F32), pltpu.VMEM((rows, 1), F32),
                        pltpu.VMEM((rows, D_HEAD), F32)],
        compiler_params=_params(("parallel", "parallel", "arbitrary")),
        name="nsa_prompt",
    )(q, qr, kc, vc, ks, vs, kw, vw, ng, cover, expand)


def _rmsnorm(x, g):
    xf = x.astype(F32)
    y = xf * lax.rsqrt(jnp.mean(xf * xf, axis=-1, keepdims=True) + EPS)
    return (y * g.astype(F32)).astype(x.dtype)


def _masked_softmax(s, mask):
    return jax.nn.softmax(jnp.where(mask, s, NEG_INF), axis=-1)


def _rope(x, pos):
    half = ROPE_DIM // 2
    inv = ROPE_THETA ** (-(jnp.arange(half, dtype=F32) * 2.0 / ROPE_DIM))
    ang = pos.astype(F32)[:, None] * inv[None, :]
    ang = ang.reshape((1, ang.shape[0]) + (1,) * (x.ndim - 3) + (half,))
    cos, sin = jnp.cos(ang), jnp.sin(ang)
    x1, x2 = x[..., :half], x[..., half:ROPE_DIM]
    return jnp.concatenate([x1 * cos - x2 * sin, x2 * cos + x1 * sin, x[..., ROPE_DIM:]], axis=-1)


def _over_query_blocks(fn, block, qpos, *qs):
    t = qpos.shape[0]
    if t <= block or t % block:
        return fn(qpos, *qs)
    n = t // block
    split = lambda a: jnp.moveaxis(a.reshape((a.shape[0], n, block) + a.shape[2:]), 1, 0)
    out = lax.map(lambda a: fn(*a), (qpos.reshape(n, block),) + tuple(split(q) for q in qs))
    out = jnp.moveaxis(out, 0, 1)
    return out.reshape((out.shape[0], t) + out.shape[3:])


def _pad_rows(a, multiple):
    extra = (-a.shape[1]) % multiple
    if extra == 0:
        return a
    return jnp.pad(a, ((0, 0), (0, extra)) + ((0, 0),) * (a.ndim - 2))


def _diff_attention(kpos, k, v, lam, sub_g, lam_init):
    scale = D_HEAD ** -0.5
    def fn(qpos, q):
        s = jnp.einsum('btgrcd,bsgcd->bgrcts', q, k, preferred_element_type=F32) * scale
        p = _masked_softmax(s, kpos[None, :] <= qpos[:, None])
        a = p[:, :, :, 0] - lam * p[:, :, :, 1]
        o = jnp.einsum('bgrts,bsgd->btgrd', a.astype(v.dtype), v)
        return _rmsnorm(o, sub_g) * (1.0 - lam_init)
    return fn


def _stick_breaking(kpos, k, v):
    scale = D_HEAD ** -0.5
    def fn(qpos, q):
        z = jnp.einsum('btgrd,bsgd->bgrts', q, k, preferred_element_type=F32) * scale
        strict = kpos[None, :] < qpos[:, None]
        log_keep = jnp.where(strict, jax.nn.log_sigmoid(-z), 0.0)
        later = lax.cumsum(log_keep, axis=z.ndim - 1, reverse=True) - log_keep
        a = jnp.where(strict, jnp.exp(jax.nn.log_sigmoid(z) + later), 0.0)
        return jnp.einsum('bgrts,bsgd->btgrd', a.astype(v.dtype), v)
    return fn


def _nsa_compress(k, w1, w2, pe):
    b, l, g, d = k.shape
    chunks = k.reshape(b, l // CMP_STRIDE, CMP_STRIDE, g, d).transpose(0, 1, 3, 2, 4)
    chunks = chunks.reshape(b, l // CMP_STRIDE, g, CMP_STRIDE * d)
    half = CMP_STRIDE * d
    lead = chunks @ w1[:half]
    trail = chunks @ w1[half:]
    h = jax.nn.gelu(lead[:, :-1] + trail[:, 1:] + pe.reshape(-1) @ w1, approximate=True)
    return h @ w2


def _nsa_cmp_sel(kc, vc, ks, vs):
    b, n_keys, g, d = ks.shape
    n_cmp = kc.shape[1]
    n_sel = n_keys // SEL_BLOCK
    top = min(SEL_TOPK, n_sel)
    scale = D_HEAD ** -0.5
    cmp_start = jnp.arange(n_cmp) * CMP_STRIDE
    cmp_end = cmp_start + CMP_LEN - 1
    blk = jnp.arange(n_sel)
    cover = ((cmp_start[:, None] <= blk[None, :] * SEL_BLOCK + SEL_BLOCK - 1)
             & (cmp_end[:, None] >= blk[None, :] * SEL_BLOCK)).astype(F32)
    ksb = ks.reshape(b, n_sel, SEL_BLOCK, g, d).transpose(0, 3, 1, 2, 4)
    vsb = vs.reshape(b, n_sel, SEL_BLOCK, g, d).transpose(0, 3, 1, 2, 4)
    bi = jnp.arange(b)[:, None, None, None]
    gi = jnp.arange(g)[None, None, :, None]
    offs = jnp.arange(SEL_BLOCK)

    def fn(qpos, q, q_rot, g_cmp, g_sel):
        tq = qpos.shape[0]
        c_ok = (cmp_end[None, :] <= qpos[:, None])[None, :, None, None, :]
        s = jnp.einsum('btgrd,bngd->btgrn', q, kc, preferred_element_type=F32) * scale
        p = jnp.where(c_ok, _masked_softmax(s, c_ok), 0.0)
        o_cmp = jnp.einsum('btgrn,bngd->btgrd', p.astype(vc.dtype), vc)
        imp = jnp.einsum('btgn,ns->btgs', p.sum(axis=3), cover)
        cur = qpos // SEL_BLOCK
        forced = (blk[None, :] == 0) | (blk[None, :] == cur[:, None]) | (blk[None, :] == cur[:, None] - 1)
        causal = blk[None, :] * SEL_BLOCK <= qpos[:, None]
        score = jnp.where(forced[None, :, None, :], FORCED,
                          jnp.where(causal[None, :, None, :], imp, NEG_INF))
        idx = lax.top_k(score, top)[1]
        gk = ksb[bi, gi, idx].reshape(b, tq, g, top * SEL_BLOCK, d)
        gv = vsb[bi, gi, idx].reshape(b, tq, g, top * SEL_BLOCK, d)
        kpos = (idx[..., None] * SEL_BLOCK + offs).reshape(b, tq, g, 1, top * SEL_BLOCK)
        ss = jnp.einsum('btgrd,btgmd->btgrm', q_rot, gk, preferred_element_type=F32) * scale
        ps = _masked_softmax(ss, kpos <= qpos[None, :, None, None, None])
        o_sel = jnp.einsum('btgrm,btgmd->btgrd', ps.astype(gv.dtype), gv)
        return g_cmp[..., None] * o_cmp + g_sel[..., None] * o_sel
    return fn


def _banded_attn(q, kv, qpos, kpos):
    s = jnp.einsum('bnqgrd,bnkgd->bngrqk', q, kv[..., 0, :], preferred_element_type=F32) * D_HEAD ** -0.5
    dist = qpos[:, :, None] - kpos[:, None, :]
    ok = (dist >= 0) & (dist < WINDOW) & (kpos[:, None, :] >= 0)
    p = _masked_softmax(s, ok[None, :, None, None])
    return jnp.einsum('bngrqk,bnkgd->bnqgrd', p.astype(kv.dtype), kv[..., 1, :])


def _window_prompt(q, kv):
    b, t = q.shape[:2]
    nb = t // Q_BLOCK
    kvp = jnp.pad(kv, ((0, 0), (WINDOW, 0), (0, 0), (0, 0), (0, 0)))
    idx = jnp.arange(nb)[:, None] * Q_BLOCK + jnp.arange(WINDOW + Q_BLOCK)[None, :]
    qpos = jnp.arange(nb)[:, None] * Q_BLOCK + jnp.arange(Q_BLOCK)[None, :]
    o = _banded_attn(q.reshape((b, nb, Q_BLOCK) + q.shape[2:]), kvp[:, idx], qpos, idx - WINDOW)
    return o.reshape(q.shape)


def _window_sample(q, pos, kv_all, start):
    kpos = start + jnp.arange(kv_all.shape[1])
    return _banded_attn(q[:, None], kv_all[:, None], pos[None], kpos[None])[:, 0]


def _prep_weights(w_in, w_br_a, w_br_b, w_br_c, w_out, w_ff_gate, w_ff_up, w_ff_down, w_ple, w_ple_gate):
    w_main = jnp.concatenate([w_in[:, :NG_AT], w_in[:, NG_AT + NG_W:]], axis=1).astype(BF16)
    w_ng = jnp.pad(w_in[:, NG_AT:NG_AT + NG_W], ((0, 0), (0, LANE - NG_W))).astype(BF16)
    c = lambda a: a.astype(BF16)
    fpad = D_FFP - D_FF
    w_ff_gate = jnp.pad(w_ff_gate, ((0, 0), (0, fpad)))
    w_ff_up = jnp.pad(w_ff_up, ((0, 0), (0, fpad)))
    w_ff_down = jnp.pad(w_ff_down, ((0, fpad), (0, 0)))
    return (w_main, w_ng, c(w_br_a), c(w_br_b), c(w_br_c), c(w_out), c(w_ff_gate), c(w_ff_up),
            c(w_ff_down), c(w_ple), c(w_ple_gate))


def _layer(x, pe, past, li, ln1, wts, diff_lambda, diff_subln, cmp_w1, cmp_w2, cmp_pe,
           ln2, ff_conv_w, ff_conv_b, ln3):
    (w_main, w_ng, w_br_a, w_br_b, w_br_c, w_out, w_ff_gate, w_ff_up, w_ff_down, w_ple, w_ple_gate) = wts
    b, t, _ = x.shape
    m = b * t
    p_len = 0 if past is None else past[0].shape[1]
    pos = p_len + jnp.arange(t, dtype=jnp.int32)
    x2 = x.reshape(m, D_MODEL)

    def with_past(new, j):
        return new if past is None else jnp.concatenate([past[j], new], axis=1)

    proj = norm_mm(x2, ln1, w_main)
    ng = norm_mm(x2, ln1, w_ng)[:, :NG_W].reshape(b, t, NG_W)
    widths = (DA_QW, DA_KW, DA_KW, NSA_QW, 6 * NSA_KW, SB_QW, SB_KW, SB_KW, 3 * D_MODEL)
    at = tuple(sum(widths[:i + 1]) for i in range(len(widths) - 1))
    da_q, da_k, da_v, nq, nkv, sq, sk, sv, bg = jnp.split(proj.reshape(b, t, -1), at, axis=-1)

    da_q = _rope(da_q.reshape(b, t, DA_KV, DA_HEADS // DA_KV, 2, D_HEAD), pos)
    da_k = _rope(da_k.reshape(b, t, DA_KV, 2, D_HEAD), pos).reshape(b, t, DA_KV, 2 * D_HEAD)
    da_new = jnp.stack([da_k, da_v.reshape(b, t, DA_KV, 2 * D_HEAD)], axis=3)
    da_all = with_past(da_new, 0)
    n_keys = da_all.shape[1]
    lam_init = 0.8 - 0.6 * math.exp(-0.3 * li)
    lp = diff_lambda.astype(F32)
    lam = jnp.exp(jnp.sum(lp[0] * lp[1])) - jnp.exp(jnp.sum(lp[2] * lp[3])) + lam_init
    scale = D_HEAD ** -0.5
    to_bf = lambda a: a.reshape(b, t, -1).astype(BF16)
    if past is None:
        o_da = diff_prompt(to_bf(da_q * scale), to_bf(da_k), to_bf(da_v), diff_lambda, diff_subln, lam_init)
    else:
        da_fn = _diff_attention(jnp.arange(n_keys), da_all[:, :, :, 0].reshape(b, n_keys, DA_KV, 2, D_HEAD),
                                da_all[:, :, :, 1], lam, diff_subln, lam_init)
        o_da = _over_query_blocks(da_fn, Q_BLOCK, pos, da_q).reshape(b, t, DA_QW)

    rn = NSA_HEADS // NSA_KV
    nq = nq.reshape(b, t, NSA_KV, rn, D_HEAD)
    nq_rot = _rope(nq, pos)
    kc, vc, ks, vs, kw, vw = [a.reshape(b, t, NSA_KV, D_HEAD) for a in jnp.split(nkv, 6, axis=-1)]
    cmp_new = jnp.stack([kc, vc], axis=3)
    sel_new = jnp.stack([_rope(ks, pos), vs], axis=3)
    win_new = jnp.stack([_rope(kw, pos), vw], axis=3)
    cmp_all = _pad_rows(with_past(cmp_new, 1), SEL_BLOCK)
    sel_all = _pad_rows(with_past(sel_new, 2), SEL_BLOCK)
    kc_blk = _nsa_compress(cmp_all[:, :, :, 0], cmp_w1[0], cmp_w2[0], cmp_pe[0])
    vc_blk = _nsa_compress(cmp_all[:, :, :, 1], cmp_w1[1], cmp_w2[1], cmp_pe[1])
    if past is None:
        win_all = win_new
        n_cmp = kc_blk.shape[1]
        blk = lambda a: jnp.pad(a.transpose(0, 2, 1, 3), ((0, 0), (0, 0), (0, LANE - n_cmp), (0, 0))).astype(BF16)
        ng_pad = jnp.pad(ng.reshape(b, t, NSA_KV, 3 * rn), ((0, 0), (0, 0), (0, 0), (0, LANE - 3 * rn)))
        o_nsa = nsa_prompt(to_bf(nq * scale), to_bf(nq_rot * scale), blk(kc_blk), blk(vc_blk),
                           to_bf(sel_new[:, :, :, 0]), to_bf(sel_new[:, :, :, 1]),
                           to_bf(win_new[:, :, :, 0]), to_bf(win_new[:, :, :, 1]), ng_pad.reshape(b, t, NSA_KV * LANE))
    else:
        gates = jax.nn.sigmoid(ng.reshape(b, t, NSA_KV, rn, 3))
        cs_fn = _nsa_cmp_sel(kc_blk, vc_blk, sel_all[:, :, :, 0], sel_all[:, :, :, 1])
        o_cs = _over_query_blocks(cs_fn, SEL_Q_BLOCK, pos, nq, nq_rot, gates[..., 0], gates[..., 1])
        win_all = with_past(win_new, 4)
        o_win = _window_sample(nq_rot, pos, win_all, p_len - past[4].shape[1])
        o_nsa = (o_cs + gates[..., 2:] * o_win).reshape(b, t, NSA_QW)
    win_state = win_all[:, win_all.shape[1] - min(WINDOW, win_all.shape[1]):]

    sq = sq.reshape(b, t, SB_KV, SB_HEADS // SB_KV, D_HEAD)
    sb_new = jnp.stack([sk.reshape(b, t, SB_KV, D_HEAD), sv.reshape(b, t, SB_KV, D_HEAD)], axis=3)
    if past is None:
        o_sb = sb_prompt(to_bf(sq * scale), to_bf(sk), to_bf(sv))
    else:
        sb_all = with_past(sb_new, 3)
        sb_fn = _stick_breaking(jnp.arange(sb_all.shape[1]), sb_all[:, :, :, 0], sb_all[:, :, :, 1])
        o_sb = _over_query_blocks(sb_fn, Q_BLOCK, pos, sq).reshape(b, t, SB_QW)

    merged = merge_mm(o_da.reshape(m, DA_QW).astype(BF16), o_nsa.reshape(m, NSA_QW).astype(BF16),
                      o_sb.reshape(m, SB_QW).astype(BF16), bg.reshape(m, 3 * D_MODEL),
                      w_br_a, w_br_b, w_br_c)
    x2 = resid_mm(x2, merged, w_out)

    gate_in, up = ffn_up(x2, ln2, w_ff_gate, w_ff_up)
    gate_in = gate_in.reshape(b, t, D_FFP)
    fpad = D_FFP - D_FF
    prev = jnp.zeros((b, CONV_W - 1, D_FFP), F32) if past is None else jnp.pad(past[5], ((0, 0), (0, 0), (0, fpad)))
    gp = jnp.concatenate([prev, gate_in], axis=1)
    conv = jnp.pad(ff_conv_b, (0, fpad))
    cw = jnp.pad(ff_conv_w, ((0, 0), (0, fpad)))
    for i in range(CONV_W):
        conv = conv + cw[i] * gp[:, i:i + t]
    act = (jax.nn.gelu(conv, approximate=True) * up.reshape(b, t, D_FFP)).astype(BF16)
    x2 = resid_mm(x2, act.reshape(m, D_FFP), w_ff_down)
    conv_state = gp[:, t:, :D_FF]

    x2 = ple_mm(x2, ln3, w_ple_gate, pe.reshape(m, -1).astype(BF16), w_ple)
    return x2.reshape(b, t, D_MODEL), (da_new, cmp_new, sel_new, sb_new, win_state, conv_state)


def kernel(x_prompt, x_sample, cache_diff, cache_cmp, cache_sel, cache_sb, state_win, state_conv, page_table, p_prompt, p_sample, ln1, w_in, diff_lambda, diff_subln, cmp_w1, cmp_w2, cmp_pe, w_br_a, w_br_b, w_br_c, w_out, ln2, w_ff_gate, w_ff_up, w_ff_down, ff_conv_w, ff_conv_b, ln3, w_ple, w_ple_gate, ln_f):
    def paged(pool):
        rows = pool[page_table]
        return rows.reshape((rows.shape[0], rows.shape[1] * rows.shape[2]) + rows.shape[3:])

    xp, xs = x_prompt, x_sample
    st_p, st_s = [], []
    for i in range(DEPTH):
        wts = _prep_weights(w_in[i], w_br_a[i], w_br_b[i], w_br_c[i], w_out[i], w_ff_gate[i], w_ff_up[i],
                            w_ff_down[i], w_ple[i], w_ple_gate[i])
        rest = (diff_lambda[i], diff_subln[i], cmp_w1[i], cmp_w2[i], cmp_pe[i], ln2[i], ff_conv_w[i],
                ff_conv_b[i], ln3[i])
        xp, sp = _layer(xp, p_prompt[i], None, i, ln1[i], wts, *rest)
        past = (paged(cache_diff[i]), paged(cache_cmp[i]), paged(cache_sel[i]), paged(cache_sb[i]),
                state_win[i], state_conv[i])
        xs, ss = _layer(xs, p_sample[i], past, i, ln1[i], wts, *rest)
        st_p.append(sp)
        st_s.append(ss)
    diff_p, cmp_p, sel_p, sb_p, win_p, conv_p = [jnp.stack(a) for a in zip(*st_p)]
    diff_s, cmp_s, sel_s, sb_s, win_s, conv_s = [jnp.stack(a) for a in zip(*st_s)]
    y_prompt = rmsnorm_rows(xp.reshape(-1, D_MODEL), ln_f).reshape(xp.shape)
    y_sample = rmsnorm_rows(xs.reshape(-1, D_MODEL), ln_f).reshape(xs.shape)
    return (y_prompt, y_sample, diff_p, diff_s, cmp_p, cmp_s, sel_p, sel_s, sb_p, sb_s, win_p, win_s, conv_p, conv_s)
```

```python
import functools
import math

import jax
import jax.numpy as jnp
from jax import lax
from jax.experimental import pallas as pl
from jax.experimental.pallas import tpu as pltpu

F32 = jnp.float32
BF16 = jnp.bfloat16

D_MODEL = 2048
DEPTH = 2
PAGE_SIZE = 128
D_HEAD = 128
ROPE_DIM = D_HEAD // 4
ROPE_THETA = 500000.0
DA_HEADS = 4
DA_KV = 2
NSA_HEADS = 8
NSA_KV = 2
CMP_STRIDE = 16
CMP_LEN = 2 * CMP_STRIDE
SEL_BLOCK = 64
SEL_TOPK = 16
WINDOW = 512
SB_HEADS = 8
SB_KV = 4
D_FF = ((8 * D_MODEL // 3 + 127) // 128) * 128
D_FFP = ((D_FF + 511) // 512) * 512
CONV_W = 3
Q_BLOCK = 128
SEL_Q_BLOCK = 32
NEG_INF = -1e30
FORCED = 1e30
EPS = 1e-6

DA_QW = DA_HEADS * 2 * D_HEAD
DA_KW = DA_KV * 2 * D_HEAD
NSA_QW = NSA_HEADS * D_HEAD
NSA_KW = NSA_KV * D_HEAD
SB_QW = SB_HEADS * D_HEAD
SB_KW = SB_KV * D_HEAD
SPLITS = (DA_QW, DA_KW, DA_KW, NSA_QW, 6 * NSA_KW, 3 * NSA_HEADS, SB_QW, SB_KW, SB_KW, 3 * D_MODEL)
SPLIT_AT = tuple(sum(SPLITS[:i + 1]) for i in range(len(SPLITS) - 1))
NG_AT = SPLIT_AT[4]
NG_W = 3 * NSA_HEADS
LANE = 128
VMEM_LIMIT = 48 * 1024 * 1024


def _params(sem):
    return pltpu.CompilerParams(dimension_semantics=sem, vmem_limit_bytes=VMEM_LIMIT)


def _rms_rows(x, g):
    return x * lax.rsqrt(jnp.mean(x * x, axis=-1, keepdims=True) + EPS) * g


def _tile(n, pref):
    t = min(n, pref)
    assert n % t == 0, (n, pref)
    return t


def _norm_mm_kernel(x_ref, g_ref, w_ref, o_ref, h_ref):
    @pl.when(pl.program_id(1) == 0)
    def _():
        h_ref[...] = _rms_rows(x_ref[...], g_ref[...]).astype(BF16)
    o_ref[...] = jnp.dot(h_ref[...], w_ref[...], preferred_element_type=F32).astype(o_ref.dtype)


def norm_mm(x, g, w, out_dtype=F32, tm=512, tn=512):
    m, k = x.shape
    n = w.shape[1]
    tm, tn = _tile(m, tm), _tile(n, tn)
    return pl.pallas_call(
        _norm_mm_kernel,
        grid=(m // tm, n // tn),
        in_specs=[pl.BlockSpec((tm, k), lambda i, j: (i, 0)),
                  pl.BlockSpec((1, k), lambda i, j: (0, 0)),
                  pl.BlockSpec((k, tn), lambda i, j: (0, j))],
        out_specs=pl.BlockSpec((tm, tn), lambda i, j: (i, j)),
        out_shape=jax.ShapeDtypeStruct((m, n), out_dtype),
        scratch_shapes=[pltpu.VMEM((tm, k), BF16)],
        compiler_params=_params(("parallel", "arbitrary")),
        name="norm_mm",
    )(x, g.reshape(1, k), w)


def _resid_mm_kernel(x_ref, a_ref, w_ref, o_ref):
    o_ref[...] = x_ref[...] + jnp.dot(a_ref[...], w_ref[...], preferred_element_type=F32)


def resid_mm(x, a, w, tm=512, tn=512):
    m, k = a.shape
    n = w.shape[1]
    tm, tn = _tile(m, tm), _tile(n, tn)
    return pl.pallas_call(
        _resid_mm_kernel,
        grid=(m // tm, n // tn),
        in_specs=[pl.BlockSpec((tm, tn), lambda i, j: (i, j)),
                  pl.BlockSpec((tm, k), lambda i, j: (i, 0)),
                  pl.BlockSpec((k, tn), lambda i, j: (0, j))],
        out_specs=pl.BlockSpec((tm, tn), lambda i, j: (i, j)),
        out_shape=jax.ShapeDtypeStruct((m, n), F32),
        compiler_params=_params(("parallel", "arbitrary")),
        name="resid_mm",
    )(x, a, w)


def _merge_mm_kernel(oa_ref, ob_ref, oc_ref, ga_ref, gb_ref, gc_ref, wa_ref, wb_ref, wc_ref, o_ref):
    def br(o_r, g_r, w_r):
        y = jnp.dot(o_r[...], w_r[...], preferred_element_type=F32)
        return jax.nn.sigmoid(g_r[...]) * y
    o_ref[...] = (br(oa_ref, ga_ref, wa_ref) + br(ob_ref, gb_ref, wb_ref)
                  + br(oc_ref, gc_ref, wc_ref)).astype(o_ref.dtype)


def merge_mm(oa, ob, oc, bg, wa, wb, wc, g_col=0, tm=512, tn=512):
    m, k = oa.shape
    n = wa.shape[1]
    tm, tn = _tile(m, tm), _tile(n, tn)
    nj = n // tn
    assert g_col % tn == 0
    g0 = g_col // tn
    o_spec = pl.BlockSpec((tm, k), lambda i, j: (i, 0))
    w_spec = pl.BlockSpec((k, tn), lambda i, j: (0, j))
    g_specs = [pl.BlockSpec((tm, tn), lambda i, j, s=s: (i, g0 + s * nj + j)) for s in range(3)]
    return pl.pallas_call(
        _merge_mm_kernel,
        grid=(m // tm, nj),
        in_specs=[o_spec, o_spec, o_spec] + g_specs + [w_spec, w_spec, w_spec],
        out_specs=pl.BlockSpec((tm, tn), lambda i, j: (i, j)),
        out_shape=jax.ShapeDtypeStruct((m, n), BF16),
        compiler_params=_params(("parallel", "arbitrary")),
        name="merge_mm",
    )(oa, ob, oc, bg, bg, bg, wa, wb, wc)


def _ffn_up_kernel(x_ref, g_ref, wg_ref, wu_ref, og_ref, ou_ref, h_ref):
    @pl.when(pl.program_id(1) == 0)
    def _():
        h_ref[...] = _rms_rows(x_ref[...], g_ref[...]).astype(BF16)
    h = h_ref[...]
    og_ref[...] = jnp.dot(h, wg_ref[...], preferred_element_type=F32)
    ou_ref[...] = jnp.dot(h, wu_ref[...], preferred_element_type=F32)


def ffn_up(x, g, wg, wu, tm=512, tn=512):
    m, k = x.shape
    n = wg.shape[1]
    tm, tn = _tile(m, tm), _tile(n, tn)
    w_spec = pl.BlockSpec((k, tn), lambda i, j: (0, j))
    o_spec = pl.BlockSpec((tm, tn), lambda i, j: (i, j))
    return pl.pallas_call(
        _ffn_up_kernel,
        grid=(m // tm, n // tn),
        in_specs=[pl.BlockSpec((tm, k), lambda i, j: (i, 0)),
                  pl.BlockSpec((1, k), lambda i, j: (0, 0)), w_spec, w_spec],
        out_specs=[o_spec, o_spec],
        out_shape=[jax.ShapeDtypeStruct((m, n), F32), jax.ShapeDtypeStruct((m, n), F32)],
        scratch_shapes=[pltpu.VMEM((tm, k), BF16)],
        compiler_params=_params(("parallel", "arbitrary")),
        name="ffn_up",
    )(x, g.reshape(1, k), wg, wu)


def _ffn_act_kernel(x_ref, g_ref, wg_ref, wu_ref, cw_ref, cb_ref, act_ref, tail_ref, h_ref, carry_ref, *, seq_tiles):
    i, j = pl.program_id(0), pl.program_id(1)

    @pl.when(j == 0)
    def _():
        h_ref[...] = _rms_rows(x_ref[...], g_ref[...]).astype(BF16)
    h = h_ref[...]
    gate = jnp.dot(h, wg_ref[...], preferred_element_type=F32)
    up = jnp.dot(h, wu_ref[...], preferred_element_type=F32)
    tm = gate.shape[0]

    @pl.when(i % seq_tiles == 0)
    def _():
        carry_ref[j] = jnp.zeros(carry_ref.shape[1:], F32)
    carry = carry_ref[j]
    row = _iota((tm, 1), 0)
    g1 = jnp.where(row == 0, carry[7:8], pltpu.roll(gate, 1, 0))
    g2 = jnp.where(row == 0, carry[6:7], jnp.where(row == 1, carry[7:8], pltpu.roll(gate, 2, 0)))
    cw = cw_ref[...]
    conv = cb_ref[...] + cw[0:1] * g2 + cw[1:2] * g1 + cw[2:3] * gate
    act_ref[...] = (jax.nn.gelu(conv, approximate=True) * up).astype(act_ref.dtype)
    carry_ref[j] = gate[tm - 8:tm]
    tail_ref[0] = gate[tm - 8:tm]


def ffn_act(x, g, wg, wu, cw, cb, seq_len, tm=512, tn=512):
    m, k = x.shape
    n = wg.shape[1]
    tm, tn = _tile(seq_len, tm), _tile(n, tn)
    seq_tiles = seq_len // tm
    w_spec = pl.BlockSpec((k, tn), lambda i, j: (0, j))
    act, tails = pl.pallas_call(
        functools.partial(_ffn_act_kernel, seq_tiles=seq_tiles),
        grid=(m // tm, n // tn),
        in_specs=[pl.BlockSpec((tm, k), lambda i, j: (i, 0)),
                  pl.BlockSpec((1, k), lambda i, j: (0, 0)), w_spec, w_spec,
                  pl.BlockSpec((8, tn), lambda i, j: (0, j)), pl.BlockSpec((1, tn), lambda i, j: (0, j))],
        out_specs=[pl.BlockSpec((tm, tn), lambda i, j: (i, j)),
                   pl.BlockSpec((1, 8, tn), lambda i, j: (i, 0, j))],
        out_shape=[jax.ShapeDtypeStruct((m, n), BF16), jax.ShapeDtypeStruct((m // tm, 8, n), F32)],
        scratch_shapes=[pltpu.VMEM((tm, k), BF16), pltpu.VMEM((n // tn, 8, tn), F32)],
        compiler_params=_params(("arbitrary", "arbitrary")),
        name="ffn_act",
    )(x, g.reshape(1, k), wg, wu, jnp.pad(cw, ((0, 8 - cw.shape[0]), (0, 0))), cb.reshape(1, n))
    return act, tails[seq_tiles - 1::seq_tiles]


def _ple_kernel(x_ref, xt_ref, g_ref, wg_ref, pe_ref, wp_ref, o_ref, h_ref):
    @pl.when(pl.program_id(1) == 0)
    def _():
        h_ref[...] = _rms_rows(x_ref[...], g_ref[...]).astype(BF16)
    gate = jax.nn.sigmoid(jnp.dot(h_ref[...], wg_ref[...], preferred_element_type=F32))
    emb = jnp.dot(pe_ref[...], wp_ref[...], preferred_element_type=F32)
    o_ref[...] = xt_ref[...] + gate * emb


def ple_mm(x, g, wg, pe, wp, tm=512, tn=512):
    m, k = x.shape
    n = wg.shape[1]
    kp = pe.shape[1]
    tm, tn = _tile(m, tm), _tile(n, tn)
    return pl.pallas_call(
        _ple_kernel,
        grid=(m // tm, n // tn),
        in_specs=[pl.BlockSpec((tm, k), lambda i, j: (i, 0)),
                  pl.BlockSpec((tm, tn), lambda i, j: (i, j)),
                  pl.BlockSpec((1, k), lambda i, j: (0, 0)),
                  pl.BlockSpec((k, tn), lambda i, j: (0, j)),
                  pl.BlockSpec((tm, kp), lambda i, j: (i, 0)),
                  pl.BlockSpec((kp, tn), lambda i, j: (0, j))],
        out_specs=pl.BlockSpec((tm, tn), lambda i, j: (i, j)),
        out_shape=jax.ShapeDtypeStruct((m, n), F32),
        scratch_shapes=[pltpu.VMEM((tm, k), BF16)],
        compiler_params=_params(("parallel", "arbitrary")),
        name="ple_mm",
    )(x, x, g.reshape(1, k), wg, pe, wp)


def _rmsnorm_kernel(x_ref, g_ref, o_ref):
    o_ref[...] = _rms_rows(x_ref[...], g_ref[...])


def rmsnorm_rows(x, g, tm=512):
    m, k = x.shape
    tm = _tile(m, tm)
    return pl.pallas_call(
        _rmsnorm_kernel,
        grid=(m // tm,),
        in_specs=[pl.BlockSpec((tm, k), lambda i: (i, 0)), pl.BlockSpec((1, k), lambda i: (0, 0))],
        out_specs=pl.BlockSpec((tm, k), lambda i: (i, 0)),
        out_shape=jax.ShapeDtypeStruct((m, k), F32),
        compiler_params=_params(("parallel",)),
        name="final_rmsnorm",
    )(x, g.reshape(1, k))


_NT = (((1,), (1,)), ((), ()))
ATT_TQ = 256
ATT_TK = 256
NSA_TQ = 128


def _iota(shape, dim):
    return lax.broadcasted_iota(jnp.int32, shape, dim)


def _split_bf16(x):
    hi = x.astype(BF16)
    lo = (x - hi.astype(F32)).astype(BF16)
    return hi, lo


def _diff_prompt_kernel(lam_ref, q_ref, k_ref, v_ref, sub_ref, o_ref, m_ref, l_ref, acc_ref, *, tq, tk, lam_init):
    i = pl.program_id(2)
    rows = 2 * tq
    qpos = i * tq + _iota((rows, 1), 0) % tq
    n_kv = (i * tq + tq + tk - 1) // tk
    for c in range(2):
        m_ref[c] = jnp.full((rows, 1), NEG_INF, F32)
        l_ref[c] = jnp.zeros((rows, 1), F32)
        acc_ref[c] = jnp.zeros((rows, 2 * D_HEAD), F32)

        def body(j, carry, c=c):
            k0 = pl.multiple_of(j * tk, tk)
            qc = jnp.concatenate([q_ref[0, :, (r * 2 + c) * D_HEAD:(r * 2 + c + 1) * D_HEAD] for r in range(2)], axis=0)
            ks = k_ref[0, pl.ds(k0, tk), c * D_HEAD:(c + 1) * D_HEAD]
            vs = v_ref[0, pl.ds(k0, tk), :]
            s = lax.dot_general(qc, ks, _NT, preferred_element_type=F32)
            ok = (k0 + _iota((1, tk), 1)) <= qpos
            s = jnp.where(ok, s, NEG_INF)
            m_old = m_ref[c]
            m_new = jnp.maximum(m_old, jnp.max(s, axis=-1, keepdims=True))
            alpha = jnp.exp(m_old - m_new)
            p = jnp.where(ok, jnp.exp(s - m_new), 0.0)
            l_ref[c] = alpha * l_ref[c] + jnp.sum(p, axis=-1, keepdims=True)
            acc_ref[c] = alpha * acc_ref[c] + jnp.dot(p.astype(BF16), vs, preferred_element_type=F32)
            m_ref[c] = m_new
            return carry

        lax.fori_loop(0, n_kv, body, 0)
    lp = lam_ref[...]
    lam = (jnp.exp(jnp.sum(lp[0:1] * lp[1:2], axis=-1, keepdims=True))
           - jnp.exp(jnp.sum(lp[2:3] * lp[3:4], axis=-1, keepdims=True)) + lam_init)
    o = acc_ref[0] / l_ref[0] - lam * (acc_ref[1] / l_ref[1])
    o = _rms_rows(o, sub_ref[...]) * (1.0 - lam_init)
    for r in range(2):
        o_ref[0, :, r * 2 * D_HEAD:(r + 1) * 2 * D_HEAD] = o[r * tq:(r + 1) * tq].astype(o_ref.dtype)


def diff_prompt(q, k, v, diff_lambda, diff_subln, lam_init, tq=ATT_TQ, tk=ATT_TK):
    b, t, _ = q.shape
    tq, tk = _tile(t, tq), _tile(t, tk)
    rows = 2 * tq
    return pl.pallas_call(
        functools.partial(_diff_prompt_kernel, tq=tq, tk=tk, lam_init=lam_init),
        grid=(b, DA_KV, t // tq),
        in_specs=[pl.BlockSpec((4, D_HEAD), lambda bi, g, i: (0, 0)),
                  pl.BlockSpec((1, tq, 4 * D_HEAD), lambda bi, g, i: (bi, i, g)),
                  pl.BlockSpec((1, t, 2 * D_HEAD), lambda bi, g, i: (bi, 0, g)),
                  pl.BlockSpec((1, t, 2 * D_HEAD), lambda bi, g, i: (bi, 0, g)),
                  pl.BlockSpec((1, 2 * D_HEAD), lambda bi, g, i: (0, 0))],
        out_specs=pl.BlockSpec((1, tq, 4 * D_HEAD), lambda bi, g, i: (bi, i, g)),
        out_shape=jax.ShapeDtypeStruct((b, t, DA_QW), BF16),
        scratch_shapes=[pltpu.VMEM((2, rows, 1), F32), pltpu.VMEM((2, rows, 1), F32),
                        pltpu.VMEM((2, rows, 2 * D_HEAD), F32)],
        compiler_params=_params(("parallel", "parallel", "arbitrary")),
        name="diff_prompt",
    )(diff_lambda.astype(F32), q, k, v, diff_subln.reshape(1, -1).astype(F32))


def _sb_prompt_kernel(q_ref, k_ref, v_ref, u_ref, o_ref, c_ref, acc_ref, *, tq, tk):
    i = pl.program_id(2)
    rows = 2 * tq
    qpos = i * tq + _iota((rows, 1), 0) % tq
    n_kv = (i * tq + tq + tk - 1) // tk
    c_ref[...] = jnp.zeros((rows, 1), F32)
    acc_ref[...] = jnp.zeros((rows, D_HEAD), F32)

    def body(jj, carry):
        j = n_kv - 1 - jj
        k0 = pl.multiple_of(j * tk, tk)
        qs = jnp.concatenate([q_ref[0, :, r * D_HEAD:(r + 1) * D_HEAD] for r in range(2)], axis=0)
        ks = k_ref[0, pl.ds(k0, tk), :]
        vs = v_ref[0, pl.ds(k0, tk), :]
        z = lax.dot_general(qs, ks, _NT, preferred_element_type=F32)
        strict = (k0 + _iota((1, tk), 1)) < qpos
        sp = jnp.maximum(z, 0.0) + jnp.log(1.0 + jnp.exp(-jnp.abs(z)))
        lk = jnp.where(strict, -sp, 0.0)
        hi, lo = _split_bf16(lk)
        u = u_ref[...]
        later = (jnp.dot(hi, u, preferred_element_type=F32) + jnp.dot(lo, u, preferred_element_type=F32)
                 + c_ref[...])
        a = jnp.where(strict, jnp.exp(z - sp + later), 0.0)
        acc_ref[...] += jnp.dot(a.astype(BF16), vs, preferred_element_type=F32)
        c_ref[...] += jnp.sum(lk, axis=-1, keepdims=True)
        return carry

    lax.fori_loop(0, n_kv, body, 0)
    for r in range(2):
        o_ref[0, :, r * D_HEAD:(r + 1) * D_HEAD] = acc_ref[r * tq:(r + 1) * tq, :].astype(o_ref.dtype)


def _suffix_matrix(tk):
    return (_iota((tk, tk), 0) > _iota((tk, tk), 1)).astype(BF16)


def sb_prompt(q, k, v, tq=ATT_TQ, tk=ATT_TK):
    b, t, _ = q.shape
    tq, tk = _tile(t, tq), _tile(t, tk)
    rows = 2 * tq
    return pl.pallas_call(
        functools.partial(_sb_prompt_kernel, tq=tq, tk=tk),
        grid=(b, SB_KV, t // tq),
        in_specs=[pl.BlockSpec((1, tq, 2 * D_HEAD), lambda bi, g, i: (bi, i, g)),
                  pl.BlockSpec((1, t, D_HEAD), lambda bi, g, i: (bi, 0, g)),
                  pl.BlockSpec((1, t, D_HEAD), lambda bi, g, i: (bi, 0, g)),
                  pl.BlockSpec((tk, tk), lambda bi, g, i: (0, 0))],
        out_specs=pl.BlockSpec((1, tq, 2 * D_HEAD), lambda bi, g, i: (bi, i, g)),
        out_shape=jax.ShapeDtypeStruct((b, t, SB_QW), BF16),
        scratch_shapes=[pltpu.VMEM((rows, 1), F32), pltpu.VMEM((rows, D_HEAD), F32)],
        compiler_params=_params(("parallel", "parallel", "arbitrary")),
        name="sb_prompt",
    )(q, k, v, _suffix_matrix(tk))


def _nsa_prompt_kernel(q_ref, qr_ref, kc_ref, vc_ref, ks_ref, vs_ref, kw_ref, vw_ref, ng_ref, cov_ref, exp_ref,
                       o_ref, sel_ref, m_ref, l_ref, acc_ref, *, tq, tk, t, n_cmp, n_sel):
    i = pl.program_id(2)
    rn = NSA_HEADS // NSA_KV
    rows = rn * tq
    q0 = i * tq
    qpos_t = q0 + _iota((tq, 1), 0)
    qpos = q0 + _iota((rows, 1), 0) % tq
    lane = _iota((1, LANE), 1)
    stack = lambda ref: jnp.concatenate([ref[0, :, r * D_HEAD:(r + 1) * D_HEAD] for r in range(rn)], axis=0)

    s = lax.dot_general(stack(q_ref), kc_ref[0, 0], _NT, preferred_element_type=F32)
    c_ok = ((lane * CMP_STRIDE + (CMP_LEN - 1)) <= qpos) & (lane < n_cmp)
    s = jnp.where(c_ok, s, NEG_INF)
    e = jnp.where(c_ok, jnp.exp(s - jnp.max(s, axis=-1, keepdims=True)), 0.0)
    den = jnp.sum(e, axis=-1, keepdims=True)
    p = e / jnp.where(den > 0.0, den, 1.0)
    o_cmp = jnp.dot(p.astype(BF16), vc_ref[0, 0], preferred_element_type=F32)
    psum = p[0:tq]
    for r in range(1, rn):
        psum = psum + p[r * tq:(r + 1) * tq]
    hi, lo = _split_bf16(psum)
    cov = cov_ref[...]
    imp = jnp.dot(hi, cov, preferred_element_type=F32) + jnp.dot(lo, cov, preferred_element_type=F32)

    cur = qpos_t // SEL_BLOCK
    forced = (lane == 0) | (lane == cur) | (lane == cur - 1)
    causal = lane * SEL_BLOCK <= qpos_t
    score = jnp.where(forced, FORCED, jnp.where(causal, imp, NEG_INF))
    score = jnp.where(lane < n_sel, score, -3e38)
    cnt = jnp.zeros((tq, LANE), F32)
    for sp in range(n_sel):
        col = score[:, sp:sp + 1]
        tie = jnp.where(lane > sp, 1.0, 0.0)
        cnt = cnt + jnp.where(col > score, 1.0, jnp.where(col == score, tie, 0.0))
    selm = jnp.where((cnt < float(min(SEL_TOPK, n_sel))) & (lane < n_sel), 1.0, 0.0).astype(BF16)
    for jj in range(t // tk):
        sel_ref[jj] = jnp.dot(selm, exp_ref[:, jj * tk:(jj + 1) * tk], preferred_element_type=F32)

    m_ref[...] = jnp.full((rows, 1), NEG_INF, F32)
    l_ref[...] = jnp.zeros((rows, 1), F32)
    acc_ref[...] = jnp.zeros((rows, D_HEAD), F32)
    n_kv = (q0 + tq + tk - 1) // tk

    def body(j, carry):
        k0 = pl.multiple_of(j * tk, tk)
        ks = ks_ref[0, pl.ds(k0, tk), :]
        vs = vs_ref[0, pl.ds(k0, tk), :]
        sj = lax.dot_general(stack(qr_ref), ks, _NT, preferred_element_type=F32)
        selx = sel_ref[j]
        ok = (jnp.concatenate([selx] * rn, axis=0) > 0.5) & ((k0 + _iota((1, tk), 1)) <= qpos)
        sj = jnp.where(ok, sj, NEG_INF)
        m_old = m_ref[...]
        m_new = jnp.maximum(m_old, jnp.max(sj, axis=-1, keepdims=True))
        alpha = jnp.exp(m_old - m_new)
        pj = jnp.where(ok, jnp.exp(sj - m_new), 0.0)
        l_ref[...] = alpha * l_ref[...] + jnp.sum(pj, axis=-1, keepdims=True)
        acc_ref[...] = alpha * acc_ref[...] + jnp.dot(pj.astype(BF16), vs, preferred_element_type=F32)
        m_ref[...] = m_new
        return carry

    lax.fori_loop(0, n_kv, body, 0)

    wlen = WINDOW + tq
    w0 = pl.multiple_of(jnp.maximum(q0 - WINDOW, 0), tq)
    kwin = kw_ref[0, pl.ds(w0, wlen), :]
    vwin = vw_ref[0, pl.ds(w0, wlen), :]
    dist = qpos_t - (w0 + _iota((1, wlen), 1))
    w_ok = (dist >= 0) & (dist < WINDOW)
    gates = jax.nn.sigmoid(ng_ref[0])
    for r in range(rn):
        sw = lax.dot_general(qr_ref[0, :, r * D_HEAD:(r + 1) * D_HEAD], kwin, _NT, preferred_element_type=F32)
        sw = jnp.where(w_ok, sw, NEG_INF)
        ew = jnp.where(w_ok, jnp.exp(sw - jnp.max(sw, axis=-1, keepdims=True)), 0.0)
        o_win = jnp.dot(ew.astype(BF16), vwin, preferred_element_type=F32) / jnp.sum(ew, axis=-1, keepdims=True)
        rs = slice(r * tq, (r + 1) * tq)
        o_sel = acc_ref[rs, :] / l_ref[rs, :]
        out = (gates[:, 3 * r:3 * r + 1] * o_cmp[rs] + gates[:, 3 * r + 1:3 * r + 2] * o_sel
               + gates[:, 3 * r + 2:3 * r + 3] * o_win)
        o_ref[0, :, r * D_HEAD:(r + 1) * D_HEAD] = out.astype(o_ref.dtype)


def nsa_prompt(q, qr, kc, vc, ks, vs, kw, vw, ng, tq=NSA_TQ, tk=ATT_TK):
    b, t, _ = q.shape
    tq, tk = _tile(t, tq), _tile(t, tk)
    n_cmp = t // CMP_STRIDE - 1
    n_sel = t // SEL_BLOCK
    assert t % SEL_BLOCK == 0 and n_cmp <= LANE and n_sel <= LANE and t >= WINDOW + tq and WINDOW % tq == 0
    rn = NSA_HEADS // NSA_KV
    rows = rn * tq
    cmp_i = _iota((LANE, LANE), 0)
    sel_i = _iota((LANE, LANE), 1)
    cover = ((cmp_i * CMP_STRIDE <= sel_i * SEL_BLOCK + SEL_BLOCK - 1)
             & (cmp_i * CMP_STRIDE + CMP_LEN - 1 >= sel_i * SEL_BLOCK)
             & (cmp_i < n_cmp) & (sel_i < n_sel)).astype(BF16)
    expand = (_iota((LANE, t), 0) == _iota((LANE, t), 1) // SEL_BLOCK).astype(BF16)
    qspec = pl.BlockSpec((1, tq, rn * D_HEAD), lambda bi, g, i: (bi, i, g))
    cspec = pl.BlockSpec((1, 1, LANE, D_HEAD), lambda bi, g, i: (bi, g, 0, 0))
    kspec = pl.BlockSpec((1, t, D_HEAD), lambda bi, g, i: (bi, 0, g))
    return pl.pallas_call(
        functools.partial(_nsa_prompt_kernel, tq=tq, tk=tk, t=t, n_cmp=n_cmp, n_sel=n_sel),
        grid=(b, NSA_KV, t // tq),
        in_specs=[qspec, qspec, cspec, cspec, kspec, kspec, kspec, kspec,
                  pl.BlockSpec((1, tq, LANE), lambda bi, g, i: (bi, i, g)),
                  pl.BlockSpec((LANE, LANE), lambda bi, g, i: (0, 0)),
                  pl.BlockSpec((LANE, t), lambda bi, g, i: (0, 0))],
        out_specs=qspec,
        out_shape=jax.ShapeDtypeStruct((b, t, NSA_QW), BF16),
        scratch_shapes=[pltpu.VMEM((t // tk, tq, tk), F32), pltpu.VMEM((rows, 1), F32), pltpu.VMEM((rows, 1), F32),
                        pltpu.VMEM((rows, D_HEAD), F32)],
        compiler_params=_params(("parallel", "parallel", "arbitrary")),
        name="nsa_prompt",
    )(q, qr, kc, vc, ks, vs, kw, vw, ng, cover, expand)


QKV_W = DA_QW + 2 * DA_KW + NSA_QW + 6 * NSA_KW + SB_QW + 2 * SB_KW
POST_BF16 = (DA_QW, DA_KW, DA_KW, NSA_QW, NSA_QW, NSA_KW, NSA_KW, NSA_KW, NSA_KW, SB_QW, SB_KW, SB_KW)
POST_F32 = (2 * DA_KW, 2 * NSA_KW, 2 * NSA_KW, 2 * NSA_KW, 2 * SB_KW)


def _qkv_post_kernel(p_ref, c_ref, s1_ref, s2_ref, daq_ref, dak_ref, dav_ref, nq_ref, nqr_ref, ks_ref, vs_ref, kw_ref,
                     vw_ref, sq_ref, sk_ref, sv_ref, dst_ref, cst_ref, sst_ref, wst_ref, bst_ref):
    scale = D_HEAD ** -0.5
    cos, s1, s2 = c_ref[...], s1_ref[...], s2_ref[...]
    head = lambda col: p_ref[:, col:col + D_HEAD]

    def rot(x):
        return x * cos + pltpu.roll(x, ROPE_DIM // 2, 1) * s1 + pltpu.roll(x, D_HEAD - ROPE_DIM // 2, 1) * s2

    def put(ref, j, x):
        ref[:, j * D_HEAD:(j + 1) * D_HEAD] = x.astype(ref.dtype)

    col = 0
    for j in range(DA_QW // D_HEAD):
        put(daq_ref, j, rot(head(col + j * D_HEAD)) * scale)
    col += DA_QW
    for g in range(DA_KV):
        for c in range(2):
            k = rot(head(col + (2 * g + c) * D_HEAD))
            put(dak_ref, 2 * g + c, k)
            put(dst_ref, 4 * g + c, k)
            v = head(col + DA_KW + (2 * g + c) * D_HEAD)
            put(dav_ref, 2 * g + c, v)
            put(dst_ref, 4 * g + 2 + c, v)
    col += 2 * DA_KW
    for j in range(NSA_QW // D_HEAD):
        x = head(col + j * D_HEAD)
        put(nq_ref, j, x * scale)
        put(nqr_ref, j, rot(x) * scale)
    col += NSA_QW
    for g in range(NSA_KV):
        kc, vc, ks, vs, kw, vw = [head(col + (NSA_KV * i + g) * D_HEAD) for i in range(6)]
        ks, kw = rot(ks), rot(kw)
        put(cst_ref, 2 * g, kc)
        put(cst_ref, 2 * g + 1, vc)
        put(ks_ref, g, ks)
        put(vs_ref, g, vs)
        put(sst_ref, 2 * g, ks)
        put(sst_ref, 2 * g + 1, vs)
        put(kw_ref, g, kw)
        put(vw_ref, g, vw)
        put(wst_ref, 2 * g, kw)
        put(wst_ref, 2 * g + 1, vw)
    col += 6 * NSA_KW
    for j in range(SB_QW // D_HEAD):
        put(sq_ref, j, head(col + j * D_HEAD) * scale)
    col += SB_QW
    for g in range(SB_KV):
        k = head(col + g * D_HEAD)
        v = head(col + SB_KW + g * D_HEAD)
        put(sk_ref, g, k)
        put(sv_ref, g, v)
        put(bst_ref, 2 * g, k)
        put(bst_ref, 2 * g + 1, v)


def qkv_post(proj, pos, tm=256):
    m = proj.shape[0]
    r = pos.shape[0]
    tm = min(tm, r) if r % min(tm, r) == 0 else r
    if m % tm:
        tm = m
    assert r % tm == 0 and m % tm == 0, (m, r, tm)
    half = ROPE_DIM // 2
    inv = ROPE_THETA ** (-(jnp.arange(half, dtype=F32) * 2.0 / ROPE_DIM))
    ang = pos.astype(F32)[:, None] * inv[None, :]
    cos, sin = jnp.cos(ang), jnp.sin(ang)
    rest = D_HEAD - ROPE_DIM
    c_tab = jnp.concatenate([cos, cos, jnp.ones((r, rest), F32)], axis=1)
    s1_tab = jnp.concatenate([jnp.zeros((r, half), F32), sin, jnp.zeros((r, rest), F32)], axis=1)
    s2_tab = jnp.concatenate([-sin, jnp.zeros((r, half + rest), F32)], axis=1)
    period = r // tm
    t_spec = pl.BlockSpec((tm, D_HEAD), lambda i: (i % period, 0))
    outs = [(w, BF16) for w in POST_BF16] + [(w, F32) for w in POST_F32]
    return pl.pallas_call(
        _qkv_post_kernel,
        grid=(m // tm,),
        in_specs=[pl.BlockSpec((tm, QKV_W), lambda i: (i, 0)), t_spec, t_spec, t_spec],
        out_specs=[pl.BlockSpec((tm, w), lambda i: (i, 0)) for w, _ in outs],
        out_shape=[jax.ShapeDtypeStruct((m, w), dt) for w, dt in outs],
        compiler_params=_params(("parallel",)),
        name="qkv_post",
    )(proj, c_tab, s1_tab, s2_tab)


def _cmp_prompt_kernel(x_ref, w1_ref, pe_ref, w2_ref, kc_ref, vc_ref):
    x = x_ref[0]
    n = x.shape[0]
    row_w = NSA_KV * 2 * D_HEAD
    for kv, out_ref in ((0, kc_ref), (1, vc_ref)):
        pw = jnp.dot(pe_ref[kv], w1_ref[kv], preferred_element_type=F32)
        pe_w1 = pw[0:1, :D_HEAD] + pw[1:2, D_HEAD:]
        for g in range(NSA_KV):
            c4 = 2 * g + kv
            xc = jnp.concatenate([x[:, r * row_w + c4 * D_HEAD:r * row_w + (c4 + 1) * D_HEAD]
                                  for r in range(CMP_STRIDE)], axis=1).astype(BF16)
            lt = jnp.dot(xc, w1_ref[kv], preferred_element_type=F32)
            h = jax.nn.gelu(lt[:, :D_HEAD] + pltpu.roll(lt[:, D_HEAD:], n - 1, 0) + pe_w1, approximate=True)
            out_ref[0, g] = jnp.dot(h.astype(BF16), w2_ref[kv], preferred_element_type=F32).astype(out_ref.dtype)


def cmp_prompt(chunks, w1lt, pe2, w2):
    b, n, w = chunks.shape
    assert n == LANE
    full = lambda shape: pl.BlockSpec(shape, lambda bi: (0,) * len(shape))
    o_spec = pl.BlockSpec((1, NSA_KV, n, D_HEAD), lambda bi: (bi, 0, 0, 0))
    return pl.pallas_call(
        _cmp_prompt_kernel,
        grid=(b,),
        in_specs=[pl.BlockSpec((1, n, w), lambda bi: (bi, 0, 0)), full(w1lt.shape), full(pe2.shape), full(w2.shape)],
        out_specs=[o_spec, o_spec],
        out_shape=[jax.ShapeDtypeStruct((b, NSA_KV, n, D_HEAD), BF16)] * 2,
        compiler_params=_params(("parallel",)),
        name="cmp_prompt",
    )(chunks, w1lt, pe2, w2)


DEC_ROWS = 16
MAX_PAGES_PER_STEP = 8


def _pages_per_step(n_pages):
    return max(d for d in range(1, MAX_PAGES_PER_STEP + 1) if n_pages % d == 0)


def _page_specs(n_per, width, reverse_from=None, rows=PAGE_SIZE):
    def spec(p):
        if reverse_from is None:
            return pl.BlockSpec((1, rows, width), lambda b, c, pt: (pt[b, c * n_per + p], 0, 0))
        return pl.BlockSpec((1, rows, width), lambda b, c, pt: (pt[b, reverse_from - (c * n_per + p)], 0, 0))
    return [spec(p) for p in range(n_per)]


def _softmax_step(m_ref, l_ref, acc_ref, g, s, v, ok):
    if ok is not None:
        s = jnp.where(ok, s, NEG_INF)
    m_old = m_ref[g]
    m_new = jnp.maximum(m_old, jnp.max(s, axis=-1, keepdims=True))
    alpha = jnp.exp(m_old - m_new)
    p = jnp.exp(s - m_new)
    if ok is not None:
        p = jnp.where(ok, p, 0.0)
    l_ref[g] = alpha * l_ref[g] + jnp.sum(p, axis=-1, keepdims=True)
    acc_ref[g] = alpha * acc_ref[g] + jnp.dot(p.astype(BF16), v, preferred_element_type=F32)
    m_ref[g] = m_new


def _new_causal(t_new):
    return _iota((DEC_ROWS, DEC_ROWS), 1) <= _iota((DEC_ROWS, DEC_ROWS), 0) % t_new


def _diff_dec_kernel(pt_ref, lam_ref, q_ref, kn_ref, vn_ref, sub_ref, *rest, n_per, lam_init, t_new):
    pages, o_ref, (m_ref, l_ref, acc_ref) = rest[:n_per], rest[n_per], rest[n_per + 1:]
    c = pl.program_id(1)
    dk = 2 * D_HEAD

    @pl.when(c == 0)
    def _():
        m_ref[...] = jnp.full(m_ref.shape, NEG_INF, F32)
        l_ref[...] = jnp.zeros(l_ref.shape, F32)
        acc_ref[...] = jnp.zeros(acc_ref.shape, F32)

    for p in range(n_per):
        for g in range(DA_KV):
            k = pages[p][0, :, g * 2 * dk:g * 2 * dk + dk].astype(BF16)
            v = pages[p][0, :, g * 2 * dk + dk:(g + 1) * 2 * dk].astype(BF16)
            s = lax.dot_general(q_ref[0, g], k, _NT, preferred_element_type=F32)
            _softmax_step(m_ref, l_ref, acc_ref, g, s, v, None)

    @pl.when(c == pl.num_programs(1) - 1)
    def _():
        lp = lam_ref[...]
        lam = (jnp.exp(jnp.sum(lp[0:1] * lp[1:2], axis=-1, keepdims=True))
               - jnp.exp(jnp.sum(lp[2:3] * lp[3:4], axis=-1, keepdims=True)) + lam_init)
        for g in range(DA_KV):
            s = lax.dot_general(q_ref[0, g], kn_ref[0, g], _NT, preferred_element_type=F32)
            _softmax_step(m_ref, l_ref, acc_ref, g, s, vn_ref[0, g], _new_causal(t_new))
            a = acc_ref[g] / l_ref[g]
            half = DEC_ROWS // 2
            o = a[0:half] - lam * a[half:DEC_ROWS]
            o_ref[0, g] = _rms_rows(o, sub_ref[...]) * (1.0 - lam_init)


def diff_dec(page_table, pool, q, kn, vn, diff_lambda, diff_subln, lam_init, t_new):
    b, n_pages = page_table.shape
    n_per = _pages_per_step(n_pages)
    blk = lambda w: pl.BlockSpec((1, DA_KV, DEC_ROWS, w), lambda bi, c, pt: (bi, 0, 0, 0))
    grid_spec = pltpu.PrefetchScalarGridSpec(
        num_scalar_prefetch=1, grid=(b, n_pages // n_per),
        in_specs=[pl.BlockSpec((4, D_HEAD), lambda bi, c, pt: (0, 0)), blk(2 * D_HEAD), blk(2 * D_HEAD), blk(2 * D_HEAD),
                  pl.BlockSpec((1, 2 * D_HEAD), lambda bi, c, pt: (0, 0))] + _page_specs(n_per, pool.shape[2]),
        out_specs=pl.BlockSpec((1, DA_KV, DEC_ROWS // 2, 2 * D_HEAD), lambda bi, c, pt: (bi, 0, 0, 0)),
        scratch_shapes=[pltpu.VMEM((DA_KV, DEC_ROWS, 1), F32), pltpu.VMEM((DA_KV, DEC_ROWS, 1), F32),
                        pltpu.VMEM((DA_KV, DEC_ROWS, 2 * D_HEAD), F32)])
    return pl.pallas_call(
        functools.partial(_diff_dec_kernel, n_per=n_per, lam_init=lam_init, t_new=t_new),
        grid_spec=grid_spec,
        out_shape=jax.ShapeDtypeStruct((b, DA_KV, DEC_ROWS // 2, 2 * D_HEAD), F32),
        compiler_params=_params(("parallel", "arbitrary")),
        name="diff_dec",
    )(page_table, diff_lambda.astype(F32), q, kn, vn, diff_subln.reshape(1, -1).astype(F32), *([pool] * n_per))


def _sb_dec_kernel(pt_ref, q_ref, kn_ref, vn_ref, *rest, n_per, t_new):
    pages, o_ref, (c_ref, acc_ref) = rest[:n_per], rest[n_per], rest[n_per + 1:]
    c = pl.program_id(1)
    lane = _iota((1, PAGE_SIZE), 1)

    def tile(g, k, v, strict):
        z = lax.dot_general(q_ref[0, g], k, _NT, preferred_element_type=F32)
        sp = jnp.maximum(z, 0.0) + jnp.log(1.0 + jnp.exp(-jnp.abs(z)))
        lk = -sp if strict is None else jnp.where(strict, -sp, 0.0)
        y = lk
        d = 1
        while d < PAGE_SIZE:
            y = y + jnp.where(lane < PAGE_SIZE - d, pltpu.roll(y, PAGE_SIZE - d, 1), 0.0)
            d *= 2
        a = jnp.exp(z - sp + (y - lk) + c_ref[g])
        if strict is not None:
            a = jnp.where(strict, a, 0.0)
        acc_ref[g] += jnp.dot(a.astype(BF16), v, preferred_element_type=F32)
        c_ref[g] += y[:, 0:1]

    @pl.when(c == 0)
    def _():
        c_ref[...] = jnp.zeros(c_ref.shape, F32)
        acc_ref[...] = jnp.zeros(acc_ref.shape, F32)
        strict = lane < _iota((DEC_ROWS, 1), 0) % t_new
        for g in range(SB_KV):
            tile(g, kn_ref[0, g], vn_ref[0, g], strict)

    for p in range(n_per):
        for g in range(SB_KV):
            k = pages[p][0, :, g * 2 * D_HEAD:g * 2 * D_HEAD + D_HEAD].astype(BF16)
            v = pages[p][0, :, g * 2 * D_HEAD + D_HEAD:(g + 1) * 2 * D_HEAD].astype(BF16)
            tile(g, k, v, None)

    @pl.when(c == pl.num_programs(1) - 1)
    def _():
        o_ref[0] = acc_ref[...]


def sb_dec(page_table, pool, q, kn, vn, t_new):
    b, n_pages = page_table.shape
    n_per = _pages_per_step(n_pages)
    blk = lambda r: pl.BlockSpec((1, SB_KV, r, D_HEAD), lambda bi, c, pt: (bi, 0, 0, 0))
    grid_spec = pltpu.PrefetchScalarGridSpec(
        num_scalar_prefetch=1, grid=(b, n_pages // n_per),
        in_specs=[blk(DEC_ROWS), blk(PAGE_SIZE), blk(PAGE_SIZE)]
        + _page_specs(n_per, pool.shape[2], reverse_from=n_pages - 1),
        out_specs=blk(DEC_ROWS),
        scratch_shapes=[pltpu.VMEM((SB_KV, DEC_ROWS, 1), F32), pltpu.VMEM((SB_KV, DEC_ROWS, D_HEAD), F32)])
    return pl.pallas_call(
        functools.partial(_sb_dec_kernel, n_per=n_per, t_new=t_new),
        grid_spec=grid_spec,
        out_shape=jax.ShapeDtypeStruct((b, SB_KV, DEC_ROWS, D_HEAD), F32),
        compiler_params=_params(("parallel", "arbitrary")),
        name="sb_dec",
    )(page_table, q, kn, vn, *([pool] * n_per))


def _nsa_cmp_dec_kernel(pt_ref, q_ref, xn_ref, w1_ref, pe_ref, w2_ref, cov_ref, exp_ref, *rest,
                        n_per, t_new, p_len, n_cmp, n_sel):
    pages = rest[:n_per]
    ocmp_ref, sel_ref = rest[n_per], rest[n_per + 1]
    lead_ref, trail_ref = rest[n_per + 2:]
    c = pl.program_id(1)
    row_w = NSA_KV * 2 * D_HEAD
    n_ch = 8 * n_per
    cb = cov_ref.shape[0]
    sl = cov_ref.shape[1]

    @pl.when(c == 0)
    def _():
        lead_ref[...] = jnp.zeros(lead_ref.shape, F32)
        trail_ref[...] = jnp.zeros(trail_ref.shape, F32)

    def project(x, base):
        for c4 in range(2 * NSA_KV):
            xc = jnp.concatenate([x[:, r * row_w + c4 * D_HEAD:r * row_w + (c4 + 1) * D_HEAD]
                                  for r in range(CMP_STRIDE)], axis=1).astype(BF16)
            lt = jnp.dot(xc, w1_ref[c4 % 2], preferred_element_type=F32)
            lead_ref[c4, pl.ds(base, x.shape[0]), :] = lt[:, :D_HEAD]
            trail_ref[c4, pl.ds(base, x.shape[0]), :] = lt[:, D_HEAD:]

    project(jnp.concatenate([pg[0] for pg in pages], axis=0), pl.multiple_of(c * n_ch, n_ch))

    @pl.when(c == pl.num_programs(1) - 1)
    def _():
        project(xn_ref[0], p_len // CMP_STRIDE)
        qpos = p_len + _iota((DEC_ROWS, 1), 0) % t_new
        n_idx = _iota((1, cb), 1)
        c_ok = ((n_idx * CMP_STRIDE + (CMP_LEN - 1)) <= qpos) & (n_idx < n_cmp)
        rr = (_iota((DEC_ROWS, DEC_ROWS), 0) % t_new == _iota((DEC_ROWS, DEC_ROWS), 1) % t_new)
        rr = jnp.where(rr, 1.0, 0.0).astype(BF16)
        lane = _iota((1, sl), 1)
        cur = qpos // SEL_BLOCK
        forced = (lane == 0) | (lane == cur) | (lane == cur - 1)
        causal = lane * SEL_BLOCK <= qpos
        for g in range(NSA_KV):
            blocks = []
            for kv in range(2):
                c4 = g * 2 + kv
                pw = jnp.dot(pe_ref[kv], w1_ref[kv], preferred_element_type=F32)
                pe_w1 = pw[0:1, :D_HEAD] + pw[1:2, D_HEAD:]
                h = jax.nn.gelu(lead_ref[c4, 0:cb, :] + trail_ref[c4, 1:cb + 1, :] + pe_w1, approximate=True)
                blocks.append(jnp.dot(h.astype(BF16), w2_ref[kv], preferred_element_type=F32).astype(BF16))
            s = lax.dot_general(q_ref[0, g], blocks[0], _NT, preferred_element_type=F32)
            s = jnp.where(c_ok, s, NEG_INF)
            e = jnp.where(c_ok, jnp.exp(s - jnp.max(s, axis=-1, keepdims=True)), 0.0)
            den = jnp.sum(e, axis=-1, keepdims=True)
            p = e / jnp.where(den > 0.0, den, 1.0)
            ocmp_ref[0, g] = jnp.dot(p.astype(BF16), blocks[1], preferred_element_type=F32)
            hi, lo = _split_bf16(p)
            psum = jnp.dot(rr, hi, preferred_element_type=F32) + jnp.dot(rr, lo, preferred_element_type=F32)
            hi, lo = _split_bf16(psum)
            cov = cov_ref[...]
            imp = jnp.dot(hi, cov, preferred_element_type=F32) + jnp.dot(lo, cov, preferred_element_type=F32)
            score = jnp.where(forced, FORCED, jnp.where(causal, imp, NEG_INF))
            score = jnp.where(lane < n_sel, score, -3e38)
            cnt = jnp.zeros((DEC_ROWS, sl), F32)
            for sp in range(n_sel):
                col = score[:, sp:sp + 1]
                tie = jnp.where(lane > sp, 1.0, 0.0)
                cnt = cnt + jnp.where(col > score, 1.0, jnp.where(col == score, tie, 0.0))
            selm = jnp.where((cnt < float(min(SEL_TOPK, n_sel))) & (lane < n_sel), 1.0, 0.0).astype(BF16)
            sel_ref[0, g] = jnp.dot(selm, exp_ref[...], preferred_element_type=F32)


def nsa_cmp_dec(page_table, pool, q, xnew, w1lt, pe2, w2, t_new):
    b, n_pages = page_table.shape
    n_per = _pages_per_step(n_pages)
    p_len = n_pages * PAGE_SIZE
    total = p_len + SEL_BLOCK
    n_cmp = total // CMP_STRIDE - 1
    n_sel = total // SEL_BLOCK
    cb = ((p_len // CMP_STRIDE + 8 + LANE - 1) // LANE) * LANE
    sl = ((n_sel + LANE - 1) // LANE) * LANE
    assert t_new <= CMP_STRIDE and DEC_ROWS % t_new == 0
    cmp_i = _iota((cb, sl), 0)
    sel_i = _iota((cb, sl), 1)
    cover = ((cmp_i * CMP_STRIDE <= sel_i * SEL_BLOCK + SEL_BLOCK - 1)
             & (cmp_i * CMP_STRIDE + CMP_LEN - 1 >= sel_i * SEL_BLOCK)
             & (cmp_i < n_cmp) & (sel_i < n_sel)).astype(BF16)
    klen = p_len + LANE
    expand = (_iota((sl, klen), 0) == _iota((sl, klen), 1) // SEL_BLOCK).astype(BF16)
    full = lambda shape: pl.BlockSpec(shape, lambda bi, c, pt: (0,) * len(shape))
    qspec = pl.BlockSpec((1, NSA_KV, DEC_ROWS, D_HEAD), lambda bi, c, pt: (bi, 0, 0, 0))
    grid_spec = pltpu.PrefetchScalarGridSpec(
        num_scalar_prefetch=1, grid=(b, n_pages // n_per),
        in_specs=[qspec, pl.BlockSpec((1, 8, pool.shape[2]), lambda bi, c, pt: (bi, 0, 0)),
                  full(w1lt.shape), full(pe2.shape), full(w2.shape), full(cover.shape), full(expand.shape)]
        + _page_specs(n_per, pool.shape[2], rows=8),
        out_specs=[qspec, pl.BlockSpec((1, NSA_KV, DEC_ROWS, klen), lambda bi, c, pt: (bi, 0, 0, 0))],
        scratch_shapes=[pltpu.VMEM((2 * NSA_KV, cb + 8, D_HEAD), F32), pltpu.VMEM((2 * NSA_KV, cb + 8, D_HEAD), F32)])
    return pl.pallas_call(
        functools.partial(_nsa_cmp_dec_kernel, n_per=n_per, t_new=t_new, p_len=p_len, n_cmp=n_cmp, n_sel=n_sel),
        grid_spec=grid_spec,
        out_shape=[jax.ShapeDtypeStruct((b, NSA_KV, DEC_ROWS, D_HEAD), F32),
                   jax.ShapeDtypeStruct((b, NSA_KV, DEC_ROWS, klen), F32)],
        compiler_params=_params(("parallel", "arbitrary")),
        name="nsa_cmp_dec",
    )(page_table, q, xnew, w1lt, pe2, w2, cover, expand, *([pool] * n_per))


def _nsa_sel_dec_kernel(pt_ref, q_ref, selp_ref, seln_ref, ksn_ref, vsn_ref, ocmp_ref, ng_ref, win_ref, kwn_ref,
                        vwn_ref, *rest, n_per, t_new):
    pages, o_ref, (m_ref, l_ref, acc_ref) = rest[:n_per], rest[n_per], rest[n_per + 1:]
    c = pl.program_id(1)

    @pl.when(c == 0)
    def _():
        m_ref[...] = jnp.full(m_ref.shape, NEG_INF, F32)
        l_ref[...] = jnp.zeros(l_ref.shape, F32)
        acc_ref[...] = jnp.zeros(acc_ref.shape, F32)

    for p in range(n_per):
        for g in range(NSA_KV):
            k = pages[p][0, :, g * 2 * D_HEAD:g * 2 * D_HEAD + D_HEAD].astype(BF16)
            v = pages[p][0, :, g * 2 * D_HEAD + D_HEAD:(g + 1) * 2 * D_HEAD].astype(BF16)
            s = lax.dot_general(q_ref[0, g], k, _NT, preferred_element_type=F32)
            ok = selp_ref[0, g, :, p * PAGE_SIZE:(p + 1) * PAGE_SIZE] > 0.5
            _softmax_step(m_ref, l_ref, acc_ref, g, s, v, ok)

    @pl.when(c == pl.num_programs(1) - 1)
    def _():
        causal = _new_causal(t_new)
        wb = win_ref.shape[1]
        t_row = _iota((DEC_ROWS, 1), 0) % t_new
        w_ok = _iota((1, wb), 1) > t_row
        for g in range(NSA_KV):
            q = q_ref[0, g]
            s = lax.dot_general(q, ksn_ref[0, g], _NT, preferred_element_type=F32)
            _softmax_step(m_ref, l_ref, acc_ref, g, s, vsn_ref[0, g], (seln_ref[0, g, :, 0:DEC_ROWS] > 0.5) & causal)
            o_sel = acc_ref[g] / l_ref[g]
            kw = win_ref[0, :, g * 2 * D_HEAD:g * 2 * D_HEAD + D_HEAD].astype(BF16)
            vw = win_ref[0, :, g * 2 * D_HEAD + D_HEAD:(g + 1) * 2 * D_HEAD].astype(BF16)
            sw = jnp.where(w_ok, lax.dot_general(q, kw, _NT, preferred_element_type=F32), NEG_INF)
            sn = jnp.where(causal, lax.dot_general(q, kwn_ref[0, g], _NT, preferred_element_type=F32), NEG_INF)
            mx = jnp.maximum(jnp.max(sw, axis=-1, keepdims=True), jnp.max(sn, axis=-1, keepdims=True))
            ew = jnp.where(w_ok, jnp.exp(sw - mx), 0.0)
            en = jnp.where(causal, jnp.exp(sn - mx), 0.0)
            den = jnp.sum(ew, axis=-1, keepdims=True) + jnp.sum(en, axis=-1, keepdims=True)
            o_win = (jnp.dot(ew.astype(BF16), vw, preferred_element_type=F32)
                     + jnp.dot(en.astype(BF16), vwn_ref[0, g], preferred_element_type=F32)) / den
            gates = jax.nn.sigmoid(ng_ref[0, g])
            o_ref[0, g] = gates[:, 0:1] * ocmp_ref[0, g] + gates[:, 1:2] * o_sel + gates[:, 2:3] * o_win


def nsa_sel_dec(page_table, pool, q, selexp, ksn, vsn, o_cmp, ng, win, kwn, vwn, t_new):
    b, n_pages = page_table.shape
    n_per = _pages_per_step(n_pages)
    assert win.shape[1] == WINDOW
    blk = pl.BlockSpec((1, NSA_KV, DEC_ROWS, D_HEAD), lambda bi, c, pt: (bi, 0, 0, 0))
    grid_spec = pltpu.PrefetchScalarGridSpec(
        num_scalar_prefetch=1, grid=(b, n_pages // n_per),
        in_specs=[blk, pl.BlockSpec((1, NSA_KV, DEC_ROWS, n_per * PAGE_SIZE), lambda bi, c, pt: (bi, 0, 0, c)),
                  pl.BlockSpec((1, NSA_KV, DEC_ROWS, LANE), lambda bi, c, pt: (bi, 0, 0, n_pages)),
                  blk, blk, blk, blk,
                  pl.BlockSpec((1, win.shape[1], win.shape[2]), lambda bi, c, pt: (bi, 0, 0)), blk, blk]
        + _page_specs(n_per, pool.shape[2]),
        out_specs=blk,
        scratch_shapes=[pltpu.VMEM((NSA_KV, DEC_ROWS, 1), F32), pltpu.VMEM((NSA_KV, DEC_ROWS, 1), F32),
                        pltpu.VMEM((NSA_KV, DEC_ROWS, D_HEAD), F32)])
    return pl.pallas_call(
        functools.partial(_nsa_sel_dec_kernel, n_per=n_per, t_new=t_new),
        grid_spec=grid_spec,
        out_shape=jax.ShapeDtypeStruct((b, NSA_KV, DEC_ROWS, D_HEAD), F32),
        compiler_params=_params(("parallel", "arbitrary")),
        name="nsa_sel_dec",
    )(page_table, q, selexp, selexp, ksn, vsn, o_cmp, ng, win, kwn, vwn, *([pool] * n_per))


def _rmsnorm(x, g):
    xf = x.astype(F32)
    y = xf * lax.rsqrt(jnp.mean(xf * xf, axis=-1, keepdims=True) + EPS)
    return (y * g.astype(F32)).astype(x.dtype)


def _masked_softmax(s, mask):
    return jax.nn.softmax(jnp.where(mask, s, NEG_INF), axis=-1)


def _rope(x, pos):
    half = ROPE_DIM // 2
    inv = ROPE_THETA ** (-(jnp.arange(half, dtype=F32) * 2.0 / ROPE_DIM))
    ang = pos.astype(F32)[:, None] * inv[None, :]
    ang = ang.reshape((1, ang.shape[0]) + (1,) * (x.ndim - 3) + (half,))
    cos, sin = jnp.cos(ang), jnp.sin(ang)
    x1, x2 = x[..., :half], x[..., half:ROPE_DIM]
    return jnp.concatenate([x1 * cos - x2 * sin, x2 * cos + x1 * sin, x[..., ROPE_DIM:]], axis=-1)


def _over_query_blocks(fn, block, qpos, *qs):
    t = qpos.shape[0]
    if t <= block or t % block:
        return fn(qpos, *qs)
    n = t // block
    split = lambda a: jnp.moveaxis(a.reshape((a.shape[0], n, block) + a.shape[2:]), 1, 0)
    out = lax.map(lambda a: fn(*a), (qpos.reshape(n, block),) + tuple(split(q) for q in qs))
    out = jnp.moveaxis(out, 0, 1)
    return out.reshape((out.shape[0], t) + out.shape[3:])


def _pad_rows(a, multiple):
    extra = (-a.shape[1]) % multiple
    if extra == 0:
        return a
    return jnp.pad(a, ((0, 0), (0, extra)) + ((0, 0),) * (a.ndim - 2))


def _diff_attention(kpos, k, v, lam, sub_g, lam_init):
    scale = D_HEAD ** -0.5
    def fn(qpos, q):
        s = jnp.einsum('btgrcd,bsgcd->bgrcts', q, k, preferred_element_type=F32) * scale
        p = _masked_softmax(s, kpos[None, :] <= qpos[:, None])
        a = p[:, :, :, 0] - lam * p[:, :, :, 1]
        o = jnp.einsum('bgrts,bsgd->btgrd', a.astype(v.dtype), v)
        return _rmsnorm(o, sub_g) * (1.0 - lam_init)
    return fn


def _stick_breaking(kpos, k, v):
    scale = D_HEAD ** -0.5
    def fn(qpos, q):
        z = jnp.einsum('btgrd,bsgd->bgrts', q, k, preferred_element_type=F32) * scale
        strict = kpos[None, :] < qpos[:, None]
        log_keep = jnp.where(strict, jax.nn.log_sigmoid(-z), 0.0)
        later = lax.cumsum(log_keep, axis=z.ndim - 1, reverse=True) - log_keep
        a = jnp.where(strict, jnp.exp(jax.nn.log_sigmoid(z) + later), 0.0)
        return jnp.einsum('bgrts,bsgd->btgrd', a.astype(v.dtype), v)
    return fn


def _nsa_compress(k, w1, w2, pe):
    b, l, g, d = k.shape
    chunks = k.reshape(b, l // CMP_STRIDE, CMP_STRIDE, g, d).transpose(0, 1, 3, 2, 4)
    chunks = chunks.reshape(b, l // CMP_STRIDE, g, CMP_STRIDE * d)
    half = CMP_STRIDE * d
    lead = chunks @ w1[:half]
    trail = chunks @ w1[half:]
    h = jax.nn.gelu(lead[:, :-1] + trail[:, 1:] + pe.reshape(-1) @ w1, approximate=True)
    return h @ w2


def _nsa_cmp_sel(kc, vc, ks, vs):
    b, n_keys, g, d = ks.shape
    n_cmp = kc.shape[1]
    n_sel = n_keys // SEL_BLOCK
    top = min(SEL_TOPK, n_sel)
    scale = D_HEAD ** -0.5
    cmp_start = jnp.arange(n_cmp) * CMP_STRIDE
    cmp_end = cmp_start + CMP_LEN - 1
    blk = jnp.arange(n_sel)
    cover = ((cmp_start[:, None] <= blk[None, :] * SEL_BLOCK + SEL_BLOCK - 1)
             & (cmp_end[:, None] >= blk[None, :] * SEL_BLOCK)).astype(F32)
    ksb = ks.reshape(b, n_sel, SEL_BLOCK, g, d).transpose(0, 3, 1, 2, 4)
    vsb = vs.reshape(b, n_sel, SEL_BLOCK, g, d).transpose(0, 3, 1, 2, 4)
    bi = jnp.arange(b)[:, None, None, None]
    gi = jnp.arange(g)[None, None, :, None]
    offs = jnp.arange(SEL_BLOCK)

    def fn(qpos, q, q_rot, g_cmp, g_sel):
        tq = qpos.shape[0]
        c_ok = (cmp_end[None, :] <= qpos[:, None])[None, :, None, None, :]
        s = jnp.einsum('btgrd,bngd->btgrn', q, kc, preferred_element_type=F32) * scale
        p = jnp.where(c_ok, _masked_softmax(s, c_ok), 0.0)
        o_cmp = jnp.einsum('btgrn,bngd->btgrd', p.astype(vc.dtype), vc)
        imp = jnp.einsum('btgn,ns->btgs', p.sum(axis=3), cover)
        cur = qpos // SEL_BLOCK
        forced = (blk[None, :] == 0) | (blk[None, :] == cur[:, None]) | (blk[None, :] == cur[:, None] - 1)
        causal = blk[None, :] * SEL_BLOCK <= qpos[:, None]
        score = jnp.where(forced[None, :, None, :], FORCED,
                          jnp.where(causal[None, :, None, :], imp, NEG_INF))
        idx = lax.top_k(score, top)[1]
        gk = ksb[bi, gi, idx].reshape(b, tq, g, top * SEL_BLOCK, d)
        gv = vsb[bi, gi, idx].reshape(b, tq, g, top * SEL_BLOCK, d)
        kpos = (idx[..., None] * SEL_BLOCK + offs).reshape(b, tq, g, 1, top * SEL_BLOCK)
        ss = jnp.einsum('btgrd,btgmd->btgrm', q_rot, gk, preferred_element_type=F32) * scale
        ps = _masked_softmax(ss, kpos <= qpos[None, :, None, None, None])
        o_sel = jnp.einsum('btgrm,btgmd->btgrd', ps.astype(gv.dtype), gv)
        return g_cmp[..., None] * o_cmp + g_sel[..., None] * o_sel
    return fn


def _banded_attn(q, kv, qpos, kpos):
    s = jnp.einsum('bnqgrd,bnkgd->bngrqk', q, kv[..., 0, :], preferred_element_type=F32) * D_HEAD ** -0.5
    dist = qpos[:, :, None] - kpos[:, None, :]
    ok = (dist >= 0) & (dist < WINDOW) & (kpos[:, None, :] >= 0)
    p = _masked_softmax(s, ok[None, :, None, None])
    return jnp.einsum('bngrqk,bnkgd->bnqgrd', p.astype(kv.dtype), kv[..., 1, :])


def _window_prompt(q, kv):
    b, t = q.shape[:2]
    nb = t // Q_BLOCK
    kvp = jnp.pad(kv, ((0, 0), (WINDOW, 0), (0, 0), (0, 0), (0, 0)))
    idx = jnp.arange(nb)[:, None] * Q_BLOCK + jnp.arange(WINDOW + Q_BLOCK)[None, :]
    qpos = jnp.arange(nb)[:, None] * Q_BLOCK + jnp.arange(Q_BLOCK)[None, :]
    o = _banded_attn(q.reshape((b, nb, Q_BLOCK) + q.shape[2:]), kvp[:, idx], qpos, idx - WINDOW)
    return o.reshape(q.shape)


def _window_sample(q, pos, kv_all, start):
    kpos = start + jnp.arange(kv_all.shape[1])
    return _banded_attn(q[:, None], kv_all[:, None], pos[None], kpos[None])[:, 0]


def _prep_weights(w_in, w_br_a, w_br_b, w_br_c, w_out, w_ff_gate, w_ff_up, w_ff_down, w_ple, w_ple_gate):
    w_main = jnp.concatenate([w_in[:, :NG_AT], w_in[:, NG_AT + NG_W:]], axis=1).astype(BF16)
    w_ng = jnp.pad(w_in[:, NG_AT:NG_AT + NG_W], ((0, 0), (0, LANE - NG_W))).astype(BF16)
    c = lambda a: a.astype(BF16)
    fpad = D_FFP - D_FF
    w_ff_gate = jnp.pad(w_ff_gate, ((0, 0), (0, fpad)))
    w_ff_up = jnp.pad(w_ff_up, ((0, 0), (0, fpad)))
    w_ff_down = jnp.pad(w_ff_down, ((0, fpad), (0, 0)))
    return (w_main, w_ng, c(w_br_a), c(w_br_b), c(w_br_c), c(w_out), c(w_ff_gate), c(w_ff_up),
            c(w_ff_down), c(w_ple), c(w_ple_gate))


def _layer(x, pe, past, li, ln1, wts, diff_lambda, diff_subln, cmp_w1, cmp_w2, cmp_pe,
           ln2, ff_conv_w, ff_conv_b, ln3):
    (w_main, w_ng, w_br_a, w_br_b, w_br_c, w_out, w_ff_gate, w_ff_up, w_ff_down, w_ple, w_ple_gate) = wts
    b, t, _ = x.shape
    m = b * t
    p_len = 0 if past is None else past[0].shape[1] * PAGE_SIZE
    pos = p_len + jnp.arange(t, dtype=jnp.int32)
    x2 = x.reshape(m, D_MODEL)
    if past is not None:
        page_table, pool_diff, pool_cmp, pool_sel, pool_sb, win_buf, conv_buf = past
        n_pool = pool_diff.shape[0]
        assert DEC_ROWS == t * NSA_HEADS // NSA_KV == 2 * t * DA_HEADS // DA_KV and DEC_ROWS >= t * SB_HEADS // SB_KV

    def dec_rows(a, rows=DEC_ROWS):
        g, d = a.shape[2], a.shape[-1]
        a = jnp.moveaxis(a, 1, -2).reshape(b, g, -1, d)
        return jnp.pad(a, ((0, 0), (0, 0), (0, rows - a.shape[2]), (0, 0))).astype(BF16)

    def dec_out(o, rn):
        g, d = o.shape[1], o.shape[-1]
        o = o[:, :, :rn * t].reshape(b, g, rn, t, d)
        return jnp.transpose(o, (0, 3, 1, 2, 4)).reshape(b, t, g * rn * d)

    proj = norm_mm(x2, ln1, w_main)
    ng = norm_mm(x2, ln1, w_ng)[:, :NG_W].reshape(b, t, NG_W)
    pos_rows = pos if t % 8 == 0 else jnp.tile(pos, b)
    (da_q, da_k, da_v, nq, nq_rot, ks, vs, kw, vw, sq, sk, sv,
     da_new, cmp_new, sel_new, win_new, sb_new) = qkv_post(proj, pos_rows)
    seq = lambda a: a.reshape(b, t, -1)
    heads = lambda a, g: a.reshape(b, t, g, -1, D_HEAD)
    da_new = da_new.reshape(b, t, DA_KV, 2, 2 * D_HEAD)
    cmp_new, sel_new, win_new = [a.reshape(b, t, NSA_KV, 2, D_HEAD) for a in (cmp_new, sel_new, win_new)]
    sb_new = sb_new.reshape(b, t, SB_KV, 2, D_HEAD)
    lam_init = 0.8 - 0.6 * math.exp(-0.3 * li)
    rn = NSA_HEADS // NSA_KV
    chunk_w = CMP_STRIDE * NSA_KV * 2 * D_HEAD
    half = CMP_STRIDE * D_HEAD
    w1lt = jnp.concatenate([cmp_w1[:, :half], cmp_w1[:, half:]], axis=2).astype(BF16)
    pe2 = jnp.pad(cmp_pe.reshape(2, 2, half), ((0, 0), (0, DEC_ROWS - 2), (0, 0))).astype(BF16)
    if past is None:
        o_da = diff_prompt(seq(da_q), seq(da_k), seq(da_v), diff_lambda, diff_subln, lam_init)
        chunks = cmp_new.reshape(b, t // CMP_STRIDE, chunk_w)
        chunks = jnp.pad(chunks, ((0, 0), (0, LANE - t // CMP_STRIDE), (0, 0)))
        kc_blk, vc_blk = cmp_prompt(chunks, w1lt, pe2, cmp_w2.astype(BF16))
        ng_pad = jnp.pad(ng.reshape(b, t, NSA_KV, 3 * rn), ((0, 0), (0, 0), (0, 0), (0, LANE - 3 * rn)))
        o_nsa = nsa_prompt(seq(nq), seq(nq_rot), kc_blk, vc_blk, seq(ks), seq(vs), seq(kw), seq(vw),
                           ng_pad.reshape(b, t, NSA_KV * LANE))
        o_sb = sb_prompt(seq(sq), seq(sk), seq(sv))
        win_all = win_new
    else:
        q_bd = (jnp.swapaxes(da_q.reshape(b, t, DA_KV, DA_HEADS // DA_KV, 2, D_HEAD), 3, 4)[..., None, :]
                * jnp.eye(2, dtype=BF16)[:, None, :, None])
        o = diff_dec(page_table, pool_diff.reshape(n_pool, PAGE_SIZE, -1),
                     dec_rows(q_bd.reshape(b, t, DA_KV, 2, DA_HEADS // DA_KV, 2 * D_HEAD)),
                     dec_rows(da_k.reshape(b, t, DA_KV, 2 * D_HEAD)), dec_rows(da_v.reshape(b, t, DA_KV, 2 * D_HEAD)),
                     diff_lambda, diff_subln, lam_init, t)
        o_da = dec_out(o, DA_HEADS // DA_KV)
        xnew = jnp.pad(cmp_new.reshape(b, 1, -1), ((0, 0), (0, 7), (0, chunk_w - t * NSA_KV * 2 * D_HEAD)))
        o_cmp, selexp = nsa_cmp_dec(page_table, pool_cmp.reshape(n_pool, PAGE_SIZE // CMP_STRIDE, chunk_w),
                                    dec_rows(heads(nq, NSA_KV)), xnew, w1lt, pe2, cmp_w2.astype(BF16), t)
        ng_rows = jnp.moveaxis(ng.reshape(b, t, NSA_KV, rn, 3), 1, 3).reshape(b, NSA_KV, rn * t, 3)
        ng_rows = jnp.pad(ng_rows, ((0, 0), (0, 0), (0, 0), (0, LANE - 3)))
        o = nsa_sel_dec(page_table, pool_sel.reshape(n_pool, PAGE_SIZE, -1), dec_rows(heads(nq_rot, NSA_KV)), selexp,
                        dec_rows(heads(ks, NSA_KV)), dec_rows(heads(vs, NSA_KV)), o_cmp, ng_rows,
                        win_buf.reshape(b, win_buf.shape[1], -1),
                        dec_rows(heads(kw, NSA_KV)), dec_rows(heads(vw, NSA_KV)), t)
        o_nsa = dec_out(o, rn)
        o = sb_dec(page_table, pool_sb.reshape(n_pool, PAGE_SIZE, -1), dec_rows(heads(sq, SB_KV)),
                   dec_rows(heads(sk, SB_KV), PAGE_SIZE), dec_rows(heads(sv, SB_KV), PAGE_SIZE), t)
        o_sb = dec_out(o, SB_HEADS // SB_KV)
        win_all = jnp.concatenate([win_buf, win_new], axis=1)
    win_state = win_all[:, win_all.shape[1] - min(WINDOW, win_all.shape[1]):]

    merged = merge_mm(o_da.reshape(m, DA_QW).astype(BF16), o_nsa.reshape(m, NSA_QW).astype(BF16),
                      o_sb.reshape(m, SB_QW).astype(BF16), proj, w_br_a, w_br_b, w_br_c,
                      g_col=proj.shape[1] - 3 * D_MODEL)
    x2 = resid_mm(x2, merged, w_out)

    fpad = D_FFP - D_FF
    conv_b = jnp.pad(ff_conv_b, (0, fpad))
    cw = jnp.pad(ff_conv_w, ((0, 0), (0, fpad)))
    if past is None:
        act, tail = ffn_act(x2, ln2, w_ff_gate, w_ff_up, cw, conv_b, t)
        conv_state = tail[:, 8 - (CONV_W - 1):, :D_FF]
    else:
        gate_in, up = ffn_up(x2, ln2, w_ff_gate, w_ff_up)
        gp = jnp.concatenate([jnp.pad(conv_buf, ((0, 0), (0, 0), (0, fpad))), gate_in.reshape(b, t, D_FFP)], axis=1)
        conv = conv_b
        for i in range(CONV_W):
            conv = conv + cw[i] * gp[:, i:i + t]
        act = (jax.nn.gelu(conv, approximate=True) * up.reshape(b, t, D_FFP)).astype(BF16).reshape(m, D_FFP)
        conv_state = gp[:, t:, :D_FF]
    x2 = resid_mm(x2, act, w_ff_down)

    x2 = ple_mm(x2, ln3, w_ple_gate, pe.reshape(m, -1).astype(BF16), w_ple)
    return x2.reshape(b, t, D_MODEL), (da_new, cmp_new, sel_new, sb_new, win_state, conv_state)


def kernel(x_prompt, x_sample, cache_diff, cache_cmp, cache_sel, cache_sb, state_win, state_conv, page_table, p_prompt, p_sample, ln1, w_in, diff_lambda, diff_subln, cmp_w1, cmp_w2, cmp_pe, w_br_a, w_br_b, w_br_c, w_out, ln2, w_ff_gate, w_ff_up, w_ff_down, ff_conv_w, ff_conv_b, ln3, w_ple, w_ple_gate, ln_f):
    n_pool = cache_diff.shape[1]
    pools = [c.reshape((DEPTH * n_pool,) + c.shape[2:]) for c in (cache_diff, cache_cmp, cache_sel, cache_sb)]
    xp, xs = x_prompt, x_sample
    st_p, st_s = [], []
    for i in range(DEPTH):
        wts = _prep_weights(w_in[i], w_br_a[i], w_br_b[i], w_br_c[i], w_out[i], w_ff_gate[i], w_ff_up[i],
                            w_ff_down[i], w_ple[i], w_ple_gate[i])
        rest = (diff_lambda[i], diff_subln[i], cmp_w1[i], cmp_w2[i], cmp_pe[i], ln2[i], ff_conv_w[i],
                ff_conv_b[i], ln3[i])
        xp, sp = _layer(xp, p_prompt[i], None, i, ln1[i], wts, *rest)
        past = (page_table + i * n_pool, *pools, state_win[i], state_conv[i])
        xs, ss = _layer(xs, p_sample[i], past, i, ln1[i], wts, *rest)
        st_p.append(sp)
        st_s.append(ss)
    diff_p, cmp_p, sel_p, sb_p, win_p, conv_p = [jnp.stack(a) for a in zip(*st_p)]
    diff_s, cmp_s, sel_s, sb_s, win_s, conv_s = [jnp.stack(a) for a in zip(*st_s)]
    y_prompt = rmsnorm_rows(xp.reshape(-1, D_MODEL), ln_f).reshape(xp.shape)
    y_sample = rmsnorm_rows(xs.reshape(-1, D_MODEL), ln_f).reshape(xs.shape)
    return (y_prompt, y_sample, diff_p, diff_s, cmp_p, cmp_s, sel_p, sel_s, sb_p, sb_s, win_p, win_s, conv_p, conv_s)
```

```python
import functools
import math

import jax
import jax.numpy as jnp
from jax import lax
from jax.experimental import pallas as pl
from jax.experimental.pallas import tpu as pltpu

F32 = jnp.float32
BF16 = jnp.bfloat16

D_MODEL = 2048
DEPTH = 2
PAGE_SIZE = 128
D_HEAD = 128
ROPE_DIM = D_HEAD // 4
ROPE_THETA = 500000.0
DA_HEADS = 4
DA_KV = 2
NSA_HEADS = 8
NSA_KV = 2
CMP_STRIDE = 16
CMP_LEN = 2 * CMP_STRIDE
SEL_BLOCK = 64
SEL_TOPK = 16
WINDOW = 512
SB_HEADS = 8
SB_KV = 4
D_FF = ((8 * D_MODEL // 3 + 127) // 128) * 128
D_FFP = ((D_FF + 511) // 512) * 512
CONV_W = 3
Q_BLOCK = 128
SEL_Q_BLOCK = 32
NEG_INF = -1e30
FORCED = 1e30
EPS = 1e-6

DA_QW = DA_HEADS * 2 * D_HEAD
DA_KW = DA_KV * 2 * D_HEAD
NSA_QW = NSA_HEADS * D_HEAD
NSA_KW = NSA_KV * D_HEAD
SB_QW = SB_HEADS * D_HEAD
SB_KW = SB_KV * D_HEAD
SPLITS = (DA_QW, DA_KW, DA_KW, NSA_QW, 6 * NSA_KW, 3 * NSA_HEADS, SB_QW, SB_KW, SB_KW, 3 * D_MODEL)
SPLIT_AT = tuple(sum(SPLITS[:i + 1]) for i in range(len(SPLITS) - 1))
NG_AT = SPLIT_AT[4]
NG_W = 3 * NSA_HEADS
LANE = 128
VMEM_LIMIT = 48 * 1024 * 1024


def _params(sem):
    return pltpu.CompilerParams(dimension_semantics=sem, vmem_limit_bytes=VMEM_LIMIT)


def _rms_rows(x, g):
    return x * lax.rsqrt(jnp.mean(x * x, axis=-1, keepdims=True) + EPS) * g


def _tile(n, pref):
    t = min(n, pref)
    assert n % t == 0, (n, pref)
    return t


def _norm_mm_kernel(x_ref, g_ref, w_ref, o_ref, h_ref):
    @pl.when(pl.program_id(1) == 0)
    def _():
        h_ref[...] = _rms_rows(x_ref[...], g_ref[...]).astype(BF16)
    o_ref[...] = jnp.dot(h_ref[...], w_ref[...], preferred_element_type=F32).astype(o_ref.dtype)


def norm_mm(x, g, w, out_dtype=F32, tm=512, tn=512):
    m, k = x.shape
    n = w.shape[1]
    tm, tn = _tile(m, tm), _tile(n, tn)
    return pl.pallas_call(
        _norm_mm_kernel,
        grid=(m // tm, n // tn),
        in_specs=[pl.BlockSpec((tm, k), lambda i, j: (i, 0)),
                  pl.BlockSpec((1, k), lambda i, j: (0, 0)),
                  pl.BlockSpec((k, tn), lambda i, j: (0, j))],
        out_specs=pl.BlockSpec((tm, tn), lambda i, j: (i, j)),
        out_shape=jax.ShapeDtypeStruct((m, n), out_dtype),
        scratch_shapes=[pltpu.VMEM((tm, k), BF16)],
        compiler_params=_params(("parallel", "arbitrary")),
        name="norm_mm",
    )(x, g.reshape(1, k), w)


def _resid_mm_kernel(x_ref, a_ref, w_ref, o_ref):
    o_ref[...] = x_ref[...] + jnp.dot(a_ref[...], w_ref[...], preferred_element_type=F32)


def resid_mm(x, a, w, tm=512, tn=512):
    m, k = a.shape
    n = w.shape[1]
    tm, tn = _tile(m, tm), _tile(n, tn)
    return pl.pallas_call(
        _resid_mm_kernel,
        grid=(m // tm, n // tn),
        in_specs=[pl.BlockSpec((tm, tn), lambda i, j: (i, j)),
                  pl.BlockSpec((tm, k), lambda i, j: (i, 0)),
                  pl.BlockSpec((k, tn), lambda i, j: (0, j))],
        out_specs=pl.BlockSpec((tm, tn), lambda i, j: (i, j)),
        out_shape=jax.ShapeDtypeStruct((m, n), F32),
        compiler_params=_params(("parallel", "arbitrary")),
        name="resid_mm",
    )(x, a, w)


def _merge_mm_kernel(oa_ref, ob_ref, oc_ref, ga_ref, gb_ref, gc_ref, wa_ref, wb_ref, wc_ref, o_ref):
    def br(o_r, g_r, w_r):
        y = jnp.dot(o_r[...], w_r[...], preferred_element_type=F32)
        return jax.nn.sigmoid(g_r[...]) * y
    o_ref[...] = (br(oa_ref, ga_ref, wa_ref) + br(ob_ref, gb_ref, wb_ref)
                  + br(oc_ref, gc_ref, wc_ref)).astype(o_ref.dtype)


def merge_mm(oa, ob, oc, bg, wa, wb, wc, g_col=0, tm=512, tn=512):
    m, k = oa.shape
    n = wa.shape[1]
    tm, tn = _tile(m, tm), _tile(n, tn)
    nj = n // tn
    assert g_col % tn == 0
    g0 = g_col // tn
    o_spec = pl.BlockSpec((tm, k), lambda i, j: (i, 0))
    w_spec = pl.BlockSpec((k, tn), lambda i, j: (0, j))
    g_specs = [pl.BlockSpec((tm, tn), lambda i, j, s=s: (i, g0 + s * nj + j)) for s in range(3)]
    return pl.pallas_call(
        _merge_mm_kernel,
        grid=(m // tm, nj),
        in_specs=[o_spec, o_spec, o_spec] + g_specs + [w_spec, w_spec, w_spec],
        out_specs=pl.BlockSpec((tm, tn), lambda i, j: (i, j)),
        out_shape=jax.ShapeDtypeStruct((m, n), BF16),
        compiler_params=_params(("parallel", "arbitrary")),
        name="merge_mm",
    )(oa, ob, oc, bg, bg, bg, wa, wb, wc)


def _ffn_up_kernel(x_ref, g_ref, wg_ref, wu_ref, og_ref, ou_ref, h_ref):
    @pl.when(pl.program_id(1) == 0)
    def _():
        h_ref[...] = _rms_rows(x_ref[...], g_ref[...]).astype(BF16)
    h = h_ref[...]
    og_ref[...] = jnp.dot(h, wg_ref[...], preferred_element_type=F32)
    ou_ref[...] = jnp.dot(h, wu_ref[...], preferred_element_type=F32)


def ffn_up(x, g, wg, wu, tm=512, tn=512):
    m, k = x.shape
    n = wg.shape[1]
    tm, tn = _tile(m, tm), _tile(n, tn)
    w_spec = pl.BlockSpec((k, tn), lambda i, j: (0, j))
    o_spec = pl.BlockSpec((tm, tn), lambda i, j: (i, j))
    return pl.pallas_call(
        _ffn_up_kernel,
        grid=(m // tm, n // tn),
        in_specs=[pl.BlockSpec((tm, k), lambda i, j: (i, 0)),
                  pl.BlockSpec((1, k), lambda i, j: (0, 0)), w_spec, w_spec],
        out_specs=[o_spec, o_spec],
        out_shape=[jax.ShapeDtypeStruct((m, n), F32), jax.ShapeDtypeStruct((m, n), F32)],
        scratch_shapes=[pltpu.VMEM((tm, k), BF16)],
        compiler_params=_params(("parallel", "arbitrary")),
        name="ffn_up",
    )(x, g.reshape(1, k), wg, wu)


def _ffn_act_kernel(x_ref, g_ref, wg_ref, wu_ref, cw_ref, cb_ref, act_ref, tail_ref, h_ref, carry_ref, *, seq_tiles):
    i, j = pl.program_id(0), pl.program_id(1)

    @pl.when(j == 0)
    def _():
        h_ref[...] = _rms_rows(x_ref[...], g_ref[...]).astype(BF16)
    h = h_ref[...]
    gate = jnp.dot(h, wg_ref[...], preferred_element_type=F32)
    up = jnp.dot(h, wu_ref[...], preferred_element_type=F32)
    tm = gate.shape[0]

    @pl.when(i % seq_tiles == 0)
    def _():
        carry_ref[j] = jnp.zeros(carry_ref.shape[1:], F32)
    carry = carry_ref[j]
    row = _iota((tm, 1), 0)
    g1 = jnp.where(row == 0, carry[7:8], pltpu.roll(gate, 1, 0))
    g2 = jnp.where(row == 0, carry[6:7], jnp.where(row == 1, carry[7:8], pltpu.roll(gate, 2, 0)))
    cw = cw_ref[...]
    conv = cb_ref[...] + cw[0:1] * g2 + cw[1:2] * g1 + cw[2:3] * gate
    act_ref[...] = (jax.nn.gelu(conv, approximate=True) * up).astype(act_ref.dtype)
    carry_ref[j] = gate[tm - 8:tm]
    tail_ref[0] = gate[tm - 8:tm]


def ffn_act(x, g, wg, wu, cw, cb, seq_len, tm=512, tn=512):
    m, k = x.shape
    n = wg.shape[1]
    tm, tn = _tile(seq_len, tm), _tile(n, tn)
    seq_tiles = seq_len // tm
    w_spec = pl.BlockSpec((k, tn), lambda i, j: (0, j))
    act, tails = pl.pallas_call(
        functools.partial(_ffn_act_kernel, seq_tiles=seq_tiles),
        grid=(m // tm, n // tn),
        in_specs=[pl.BlockSpec((tm, k), lambda i, j: (i, 0)),
                  pl.BlockSpec((1, k), lambda i, j: (0, 0)), w_spec, w_spec,
                  pl.BlockSpec((8, tn), lambda i, j: (0, j)), pl.BlockSpec((1, tn), lambda i, j: (0, j))],
        out_specs=[pl.BlockSpec((tm, tn), lambda i, j: (i, j)),
                   pl.BlockSpec((1, 8, tn), lambda i, j: (i, 0, j))],
        out_shape=[jax.ShapeDtypeStruct((m, n), BF16), jax.ShapeDtypeStruct((m // tm, 8, n), F32)],
        scratch_shapes=[pltpu.VMEM((tm, k), BF16), pltpu.VMEM((n // tn, 8, tn), F32)],
        compiler_params=_params(("arbitrary", "arbitrary")),
        name="ffn_act",
    )(x, g.reshape(1, k), wg, wu, jnp.pad(cw, ((0, 8 - cw.shape[0]), (0, 0))), cb.reshape(1, n))
    return act, tails[seq_tiles - 1::seq_tiles]


def _ple_kernel(x_ref, xt_ref, g_ref, wg_ref, pe_ref, wp_ref, o_ref, h_ref):
    @pl.when(pl.program_id(1) == 0)
    def _():
        h_ref[...] = _rms_rows(x_ref[...], g_ref[...]).astype(BF16)
    gate = jax.nn.sigmoid(jnp.dot(h_ref[...], wg_ref[...], preferred_element_type=F32))
    emb = jnp.dot(pe_ref[...], wp_ref[...], preferred_element_type=F32)
    o_ref[...] = xt_ref[...] + gate * emb


def ple_mm(x, g, wg, pe, wp, tm=512, tn=512):
    m, k = x.shape
    n = wg.shape[1]
    kp = pe.shape[1]
    tm, tn = _tile(m, tm), _tile(n, tn)
    return pl.pallas_call(
        _ple_kernel,
        grid=(m // tm, n // tn),
        in_specs=[pl.BlockSpec((tm, k), lambda i, j: (i, 0)),
                  pl.BlockSpec((tm, tn), lambda i, j: (i, j)),
                  pl.BlockSpec((1, k), lambda i, j: (0, 0)),
                  pl.BlockSpec((k, tn), lambda i, j: (0, j)),
                  pl.BlockSpec((tm, kp), lambda i, j: (i, 0)),
                  pl.BlockSpec((kp, tn), lambda i, j: (0, j))],
        out_specs=pl.BlockSpec((tm, tn), lambda i, j: (i, j)),
        out_shape=jax.ShapeDtypeStruct((m, n), F32),
        scratch_shapes=[pltpu.VMEM((tm, k), BF16)],
        compiler_params=_params(("parallel", "arbitrary")),
        name="ple_mm",
    )(x, x, g.reshape(1, k), wg, pe, wp)


def _rmsnorm_kernel(x_ref, g_ref, o_ref):
    o_ref[...] = _rms_rows(x_ref[...], g_ref[...])


def rmsnorm_rows(x, g, tm=512):
    m, k = x.shape
    tm = _tile(m, tm)
    return pl.pallas_call(
        _rmsnorm_kernel,
        grid=(m // tm,),
        in_specs=[pl.BlockSpec((tm, k), lambda i: (i, 0)), pl.BlockSpec((1, k), lambda i: (0, 0))],
        out_specs=pl.BlockSpec((tm, k), lambda i: (i, 0)),
        out_shape=jax.ShapeDtypeStruct((m, k), F32),
        compiler_params=_params(("parallel",)),
        name="final_rmsnorm",
    )(x, g.reshape(1, k))


_NT = (((1,), (1,)), ((), ()))
ATT_TQ = 256
ATT_TK = 256
NSA_TQ = 128


def _iota(shape, dim):
    return lax.broadcasted_iota(jnp.int32, shape, dim)


def _split_bf16(x):
    hi = x.astype(BF16)
    lo = (x - hi.astype(F32)).astype(BF16)
    return hi, lo


def _diff_prompt_kernel(lam_ref, q_ref, k_ref, v_ref, sub_ref, o_ref, m_ref, l_ref, acc_ref, *, tq, tk, lam_init):
    i = pl.program_id(2)
    rows = 2 * tq
    qpos = i * tq + _iota((rows, 1), 0) % tq
    n_kv = (i * tq + tq + tk - 1) // tk
    for c in range(2):
        m_ref[c] = jnp.full((rows, 1), NEG_INF, F32)
        l_ref[c] = jnp.zeros((rows, 1), F32)
        acc_ref[c] = jnp.zeros((rows, 2 * D_HEAD), F32)

        def body(j, carry, c=c):
            k0 = pl.multiple_of(j * tk, tk)
            qc = jnp.concatenate([q_ref[0, :, (r * 2 + c) * D_HEAD:(r * 2 + c + 1) * D_HEAD] for r in range(2)], axis=0)
            ks = k_ref[0, pl.ds(k0, tk), c * D_HEAD:(c + 1) * D_HEAD]
            vs = v_ref[0, pl.ds(k0, tk), :]
            s = lax.dot_general(qc, ks, _NT, preferred_element_type=F32)
            ok = (k0 + _iota((1, tk), 1)) <= qpos
            s = jnp.where(ok, s, NEG_INF)
            m_old = m_ref[c]
            m_new = jnp.maximum(m_old, jnp.max(s, axis=-1, keepdims=True))
            alpha = jnp.exp(m_old - m_new)
            p = jnp.where(ok, jnp.exp(s - m_new), 0.0)
            l_ref[c] = alpha * l_ref[c] + jnp.sum(p, axis=-1, keepdims=True)
            acc_ref[c] = alpha * acc_ref[c] + jnp.dot(p.astype(BF16), vs, preferred_element_type=F32)
            m_ref[c] = m_new
            return carry

        lax.fori_loop(0, n_kv, body, 0)
    lp = lam_ref[...]
    lam = (jnp.exp(jnp.sum(lp[0:1] * lp[1:2], axis=-1, keepdims=True))
           - jnp.exp(jnp.sum(lp[2:3] * lp[3:4], axis=-1, keepdims=True)) + lam_init)
    o = acc_ref[0] / l_ref[0] - lam * (acc_ref[1] / l_ref[1])
    o = _rms_rows(o, sub_ref[...]) * (1.0 - lam_init)
    for r in range(2):
        o_ref[0, :, r * 2 * D_HEAD:(r + 1) * 2 * D_HEAD] = o[r * tq:(r + 1) * tq].astype(o_ref.dtype)


def diff_prompt(q, k, v, diff_lambda, diff_subln, lam_init, tq=ATT_TQ, tk=ATT_TK):
    b, t, _ = q.shape
    tq, tk = _tile(t, tq), _tile(t, tk)
    rows = 2 * tq
    return pl.pallas_call(
        functools.partial(_diff_prompt_kernel, tq=tq, tk=tk, lam_init=lam_init),
        grid=(b, DA_KV, t // tq),
        in_specs=[pl.BlockSpec((4, D_HEAD), lambda bi, g, i: (0, 0)),
                  pl.BlockSpec((1, tq, 4 * D_HEAD), lambda bi, g, i: (bi, i, g)),
                  pl.BlockSpec((1, t, 2 * D_HEAD), lambda bi, g, i: (bi, 0, g)),
                  pl.BlockSpec((1, t, 2 * D_HEAD), lambda bi, g, i: (bi, 0, g)),
                  pl.BlockSpec((1, 2 * D_HEAD), lambda bi, g, i: (0, 0))],
        out_specs=pl.BlockSpec((1, tq, 4 * D_HEAD), lambda bi, g, i: (bi, i, g)),
        out_shape=jax.ShapeDtypeStruct((b, t, DA_QW), BF16),
        scratch_shapes=[pltpu.VMEM((2, rows, 1), F32), pltpu.VMEM((2, rows, 1), F32),
                        pltpu.VMEM((2, rows, 2 * D_HEAD), F32)],
        compiler_params=_params(("parallel", "parallel", "arbitrary")),
        name="diff_prompt",
    )(diff_lambda.astype(F32), q, k, v, diff_subln.reshape(1, -1).astype(F32))


def _sb_prompt_kernel(q_ref, k_ref, v_ref, u_ref, o_ref, c_ref, acc_ref, *, tq, tk):
    i = pl.program_id(2)
    rows = 2 * tq
    qpos = i * tq + _iota((rows, 1), 0) % tq
    n_kv = (i * tq + tq + tk - 1) // tk
    c_ref[...] = jnp.zeros((rows, 1), F32)
    acc_ref[...] = jnp.zeros((rows, D_HEAD), F32)

    def body(jj, carry):
        j = n_kv - 1 - jj
        k0 = pl.multiple_of(j * tk, tk)
        qs = jnp.concatenate([q_ref[0, :, r * D_HEAD:(r + 1) * D_HEAD] for r in range(2)], axis=0)
        ks = k_ref[0, pl.ds(k0, tk), :]
        vs = v_ref[0, pl.ds(k0, tk), :]
        z = lax.dot_general(qs, ks, _NT, preferred_element_type=F32)
        strict = (k0 + _iota((1, tk), 1)) < qpos
        sp = jnp.maximum(z, 0.0) + jnp.log(1.0 + jnp.exp(-jnp.abs(z)))
        lk = jnp.where(strict, -sp, 0.0)
        hi, lo = _split_bf16(lk)
        u = u_ref[...]
        later = (jnp.dot(hi, u, preferred_element_type=F32) + jnp.dot(lo, u, preferred_element_type=F32)
                 + c_ref[...])
        a = jnp.where(strict, jnp.exp(z - sp + later), 0.0)
        acc_ref[...] += jnp.dot(a.astype(BF16), vs, preferred_element_type=F32)
        c_ref[...] += jnp.sum(lk, axis=-1, keepdims=True)
        return carry

    lax.fori_loop(0, n_kv, body, 0)
    for r in range(2):
        o_ref[0, :, r * D_HEAD:(r + 1) * D_HEAD] = acc_ref[r * tq:(r + 1) * tq, :].astype(o_ref.dtype)


def _suffix_matrix(tk):
    return (_iota((tk, tk), 0) > _iota((tk, tk), 1)).astype(BF16)


def sb_prompt(q, k, v, tq=ATT_TQ, tk=ATT_TK):
    b, t, _ = q.shape
    tq, tk = _tile(t, tq), _tile(t, tk)
    rows = 2 * tq
    return pl.pallas_call(
        functools.partial(_sb_prompt_kernel, tq=tq, tk=tk),
        grid=(b, SB_KV, t // tq),
        in_specs=[pl.BlockSpec((1, tq, 2 * D_HEAD), lambda bi, g, i: (bi, i, g)),
                  pl.BlockSpec((1, t, D_HEAD), lambda bi, g, i: (bi, 0, g)),
                  pl.BlockSpec((1, t, D_HEAD), lambda bi, g, i: (bi, 0, g)),
                  pl.BlockSpec((tk, tk), lambda bi, g, i: (0, 0))],
        out_specs=pl.BlockSpec((1, tq, 2 * D_HEAD), lambda bi, g, i: (bi, i, g)),
        out_shape=jax.ShapeDtypeStruct((b, t, SB_QW), BF16),
        scratch_shapes=[pltpu.VMEM((rows, 1), F32), pltpu.VMEM((rows, D_HEAD), F32)],
        compiler_params=_params(("parallel", "parallel", "arbitrary")),
        name="sb_prompt",
    )(q, k, v, _suffix_matrix(tk))


def _nsa_prompt_kernel(q_ref, qr_ref, kc_ref, vc_ref, ks_ref, vs_ref, kw_ref, vw_ref, ng_ref, cov_ref, exp_ref,
                       o_ref, sel_ref, m_ref, l_ref, acc_ref, *, tq, tk, t, n_cmp, n_sel):
    i = pl.program_id(2)
    rn = NSA_HEADS // NSA_KV
    rows = rn * tq
    q0 = i * tq
    qpos_t = q0 + _iota((tq, 1), 0)
    qpos = q0 + _iota((rows, 1), 0) % tq
    lane = _iota((1, LANE), 1)
    stack = lambda ref: jnp.concatenate([ref[0, :, r * D_HEAD:(r + 1) * D_HEAD] for r in range(rn)], axis=0)

    s = lax.dot_general(stack(q_ref), kc_ref[0, 0], _NT, preferred_element_type=F32)
    c_ok = ((lane * CMP_STRIDE + (CMP_LEN - 1)) <= qpos) & (lane < n_cmp)
    s = jnp.where(c_ok, s, NEG_INF)
    e = jnp.where(c_ok, jnp.exp(s - jnp.max(s, axis=-1, keepdims=True)), 0.0)
    den = jnp.sum(e, axis=-1, keepdims=True)
    p = e / jnp.where(den > 0.0, den, 1.0)
    o_cmp = jnp.dot(p.astype(BF16), vc_ref[0, 0], preferred_element_type=F32)
    psum = p[0:tq]
    for r in range(1, rn):
        psum = psum + p[r * tq:(r + 1) * tq]
    hi, lo = _split_bf16(psum)
    cov = cov_ref[...]
    imp = jnp.dot(hi, cov, preferred_element_type=F32) + jnp.dot(lo, cov, preferred_element_type=F32)

    cur = qpos_t // SEL_BLOCK
    forced = (lane == 0) | (lane == cur) | (lane == cur - 1)
    causal = lane * SEL_BLOCK <= qpos_t
    score = jnp.where(forced, FORCED, jnp.where(causal, imp, NEG_INF))
    score = jnp.where(lane < n_sel, score, -3e38)
    cnt = jnp.zeros((tq, LANE), F32)
    for sp in range(n_sel):
        col = score[:, sp:sp + 1]
        tie = jnp.where(lane > sp, 1.0, 0.0)
        cnt = cnt + jnp.where(col > score, 1.0, jnp.where(col == score, tie, 0.0))
    selm = jnp.where((cnt < float(min(SEL_TOPK, n_sel))) & (lane < n_sel), 1.0, 0.0).astype(BF16)
    for jj in range(t // tk):
        sel_ref[jj] = jnp.dot(selm, exp_ref[:, jj * tk:(jj + 1) * tk], preferred_element_type=F32)

    m_ref[...] = jnp.full((rows, 1), NEG_INF, F32)
    l_ref[...] = jnp.zeros((rows, 1), F32)
    acc_ref[...] = jnp.zeros((rows, D_HEAD), F32)
    n_kv = (q0 + tq + tk - 1) // tk

    def body(j, carry):
        k0 = pl.multiple_of(j * tk, tk)
        ks = ks_ref[0, pl.ds(k0, tk), :]
        vs = vs_ref[0, pl.ds(k0, tk), :]
        sj = lax.dot_general(stack(qr_ref), ks, _NT, preferred_element_type=F32)
        selx = sel_ref[j]
        ok = (jnp.concatenate([selx] * rn, axis=0) > 0.5) & ((k0 + _iota((1, tk), 1)) <= qpos)
        sj = jnp.where(ok, sj, NEG_INF)
        m_old = m_ref[...]
        m_new = jnp.maximum(m_old, jnp.max(sj, axis=-1, keepdims=True))
        alpha = jnp.exp(m_old - m_new)
        pj = jnp.where(ok, jnp.exp(sj - m_new), 0.0)
        l_ref[...] = alpha * l_ref[...] + jnp.sum(pj, axis=-1, keepdims=True)
        acc_ref[...] = alpha * acc_ref[...] + jnp.dot(pj.astype(BF16), vs, preferred_element_type=F32)
        m_ref[...] = m_new
        return carry

    lax.fori_loop(0, n_kv, body, 0)

    wlen = WINDOW + tq
    w0 = pl.multiple_of(jnp.maximum(q0 - WINDOW, 0), tq)
    kwin = kw_ref[0, pl.ds(w0, wlen), :]
    vwin = vw_ref[0, pl.ds(w0, wlen), :]
    dist = qpos_t - (w0 + _iota((1, wlen), 1))
    w_ok = (dist >= 0) & (dist < WINDOW)
    gates = jax.nn.sigmoid(ng_ref[0])
    for r in range(rn):
        sw = lax.dot_general(qr_ref[0, :, r * D_HEAD:(r + 1) * D_HEAD], kwin, _NT, preferred_element_type=F32)
        sw = jnp.where(w_ok, sw, NEG_INF)
        ew = jnp.where(w_ok, jnp.exp(sw - jnp.max(sw, axis=-1, keepdims=True)), 0.0)
        o_win = jnp.dot(ew.astype(BF16), vwin, preferred_element_type=F32) / jnp.sum(ew, axis=-1, keepdims=True)
        rs = slice(r * tq, (r + 1) * tq)
        o_sel = acc_ref[rs, :] / l_ref[rs, :]
        out = (gates[:, 3 * r:3 * r + 1] * o_cmp[rs] + gates[:, 3 * r + 1:3 * r + 2] * o_sel
               + gates[:, 3 * r + 2:3 * r + 3] * o_win)
        o_ref[0, :, r * D_HEAD:(r + 1) * D_HEAD] = out.astype(o_ref.dtype)


def nsa_prompt(q, qr, kc, vc, ks, vs, kw, vw, ng, tq=NSA_TQ, tk=ATT_TK):
    b, t, _ = q.shape
    tq, tk = _tile(t, tq), _tile(t, tk)
    n_cmp = t // CMP_STRIDE - 1
    n_sel = t // SEL_BLOCK
    assert t % SEL_BLOCK == 0 and n_cmp <= LANE and n_sel <= LANE and t >= WINDOW + tq and WINDOW % tq == 0
    rn = NSA_HEADS // NSA_KV
    rows = rn * tq
    cmp_i = _iota((LANE, LANE), 0)
    sel_i = _iota((LANE, LANE), 1)
    cover = ((cmp_i * CMP_STRIDE <= sel_i * SEL_BLOCK + SEL_BLOCK - 1)
             & (cmp_i * CMP_STRIDE + CMP_LEN - 1 >= sel_i * SEL_BLOCK)
             & (cmp_i < n_cmp) & (sel_i < n_sel)).astype(BF16)
    expand = (_iota((LANE, t), 0) == _iota((LANE, t), 1) // SEL_BLOCK).astype(BF16)
    qspec = pl.BlockSpec((1, tq, rn * D_HEAD), lambda bi, g, i: (bi, i, g))
    cspec = pl.BlockSpec((1, 1, LANE, D_HEAD), lambda bi, g, i: (bi, g, 0, 0))
    kspec = pl.BlockSpec((1, t, D_HEAD), lambda bi, g, i: (bi, 0, g))
    return pl.pallas_call(
        functools.partial(_nsa_prompt_kernel, tq=tq, tk=tk, t=t, n_cmp=n_cmp, n_sel=n_sel),
        grid=(b, NSA_KV, t // tq),
        in_specs=[qspec, qspec, cspec, cspec, kspec, kspec, kspec, kspec,
                  pl.BlockSpec((1, tq, LANE), lambda bi, g, i: (bi, i, g)),
                  pl.BlockSpec((LANE, LANE), lambda bi, g, i: (0, 0)),
                  pl.BlockSpec((LANE, t), lambda bi, g, i: (0, 0))],
        out_specs=qspec,
        out_shape=jax.ShapeDtypeStruct((b, t, NSA_QW), BF16),
        scratch_shapes=[pltpu.VMEM((t // tk, tq, tk), F32), pltpu.VMEM((rows, 1), F32), pltpu.VMEM((rows, 1), F32),
                        pltpu.VMEM((rows, D_HEAD), F32)],
        compiler_params=_params(("parallel", "parallel", "arbitrary")),
        name="nsa_prompt",
    )(q, qr, kc, vc, ks, vs, kw, vw, ng, cover, expand)


QKV_W = DA_QW + 2 * DA_KW + NSA_QW + 6 * NSA_KW + SB_QW + 2 * SB_KW
POST_BF16 = (DA_QW, DA_KW, DA_KW, NSA_QW, NSA_QW, NSA_KW, NSA_KW, NSA_KW, NSA_KW, SB_QW, SB_KW, SB_KW)
POST_F32 = (2 * DA_KW, 2 * NSA_KW, 2 * NSA_KW, 2 * NSA_KW, 2 * SB_KW)


def _qkv_post_kernel(p_ref, c_ref, s1_ref, s2_ref, daq_ref, dak_ref, dav_ref, nq_ref, nqr_ref, ks_ref, vs_ref, kw_ref,
                     vw_ref, sq_ref, sk_ref, sv_ref, dst_ref, cst_ref, sst_ref, wst_ref, bst_ref):
    scale = D_HEAD ** -0.5
    cos, s1, s2 = c_ref[...], s1_ref[...], s2_ref[...]
    head = lambda col: p_ref[:, col:col + D_HEAD]

    def rot(x):
        return x * cos + pltpu.roll(x, ROPE_DIM // 2, 1) * s1 + pltpu.roll(x, D_HEAD - ROPE_DIM // 2, 1) * s2

    def put(ref, j, x):
        ref[:, j * D_HEAD:(j + 1) * D_HEAD] = x.astype(ref.dtype)

    col = 0
    for j in range(DA_QW // D_HEAD):
        put(daq_ref, j, rot(head(col + j * D_HEAD)) * scale)
    col += DA_QW
    for g in range(DA_KV):
        for c in range(2):
            k = rot(head(col + (2 * g + c) * D_HEAD))
            put(dak_ref, 2 * g + c, k)
            put(dst_ref, 4 * g + c, k)
            v = head(col + DA_KW + (2 * g + c) * D_HEAD)
            put(dav_ref, 2 * g + c, v)
            put(dst_ref, 4 * g + 2 + c, v)
    col += 2 * DA_KW
    for j in range(NSA_QW // D_HEAD):
        x = head(col + j * D_HEAD)
        put(nq_ref, j, x * scale)
        put(nqr_ref, j, rot(x) * scale)
    col += NSA_QW
    for g in range(NSA_KV):
        kc, vc, ks, vs, kw, vw = [head(col + (NSA_KV * i + g) * D_HEAD) for i in range(6)]
        ks, kw = rot(ks), rot(kw)
        put(cst_ref, 2 * g, kc)
        put(cst_ref, 2 * g + 1, vc)
        put(ks_ref, g, ks)
        put(vs_ref, g, vs)
        put(sst_ref, 2 * g, ks)
        put(sst_ref, 2 * g + 1, vs)
        put(kw_ref, g, kw)
        put(vw_ref, g, vw)
        put(wst_ref, 2 * g, kw)
        put(wst_ref, 2 * g + 1, vw)
    col += 6 * NSA_KW
    for j in range(SB_QW // D_HEAD):
        put(sq_ref, j, head(col + j * D_HEAD) * scale)
    col += SB_QW
    for g in range(SB_KV):
        k = head(col + g * D_HEAD)
        v = head(col + SB_KW + g * D_HEAD)
        put(sk_ref, g, k)
        put(sv_ref, g, v)
        put(bst_ref, 2 * g, k)
        put(bst_ref, 2 * g + 1, v)


def qkv_post(proj, pos, tm=256):
    m = proj.shape[0]
    r = pos.shape[0]
    tm = min(tm, r) if r % min(tm, r) == 0 else r
    if m % tm:
        tm = m
    assert r % tm == 0 and m % tm == 0, (m, r, tm)
    half = ROPE_DIM // 2
    inv = ROPE_THETA ** (-(jnp.arange(half, dtype=F32) * 2.0 / ROPE_DIM))
    ang = pos.astype(F32)[:, None] * inv[None, :]
    cos, sin = jnp.cos(ang), jnp.sin(ang)
    rest = D_HEAD - ROPE_DIM
    c_tab = jnp.concatenate([cos, cos, jnp.ones((r, rest), F32)], axis=1)
    s1_tab = jnp.concatenate([jnp.zeros((r, half), F32), sin, jnp.zeros((r, rest), F32)], axis=1)
    s2_tab = jnp.concatenate([-sin, jnp.zeros((r, half + rest), F32)], axis=1)
    period = r // tm
    t_spec = pl.BlockSpec((tm, D_HEAD), lambda i: (i % period, 0))
    outs = [(w, BF16) for w in POST_BF16] + [(w, F32) for w in POST_F32]
    return pl.pallas_call(
        _qkv_post_kernel,
        grid=(m // tm,),
        in_specs=[pl.BlockSpec((tm, QKV_W), lambda i: (i, 0)), t_spec, t_spec, t_spec],
        out_specs=[pl.BlockSpec((tm, w), lambda i: (i, 0)) for w, _ in outs],
        out_shape=[jax.ShapeDtypeStruct((m, w), dt) for w, dt in outs],
        compiler_params=_params(("parallel",)),
        name="qkv_post",
    )(proj, c_tab, s1_tab, s2_tab)


def _cmp_prompt_kernel(x_ref, w1_ref, pe_ref, w2_ref, kc_ref, vc_ref):
    x = x_ref[0]
    n = x.shape[0]
    row_w = NSA_KV * 2 * D_HEAD
    for kv, out_ref in ((0, kc_ref), (1, vc_ref)):
        pw = jnp.dot(pe_ref[kv], w1_ref[kv], preferred_element_type=F32)
        pe_w1 = pw[0:1, :D_HEAD] + pw[1:2, D_HEAD:]
        for g in range(NSA_KV):
            c4 = 2 * g + kv
            xc = jnp.concatenate([x[:, r * row_w + c4 * D_HEAD:r * row_w + (c4 + 1) * D_HEAD]
                                  for r in range(CMP_STRIDE)], axis=1).astype(BF16)
            lt = jnp.dot(xc, w1_ref[kv], preferred_element_type=F32)
            h = jax.nn.gelu(lt[:, :D_HEAD] + pltpu.roll(lt[:, D_HEAD:], n - 1, 0) + pe_w1, approximate=True)
            out_ref[0, g] = jnp.dot(h.astype(BF16), w2_ref[kv], preferred_element_type=F32).astype(out_ref.dtype)


def cmp_prompt(chunks, w1lt, pe2, w2):
    b, n, w = chunks.shape
    assert n == LANE
    full = lambda shape: pl.BlockSpec(shape, lambda bi: (0,) * len(shape))
    o_spec = pl.BlockSpec((1, NSA_KV, n, D_HEAD), lambda bi: (bi, 0, 0, 0))
    return pl.pallas_call(
        _cmp_prompt_kernel,
        grid=(b,),
        in_specs=[pl.BlockSpec((1, n, w), lambda bi: (bi, 0, 0)), full(w1lt.shape), full(pe2.shape), full(w2.shape)],
        out_specs=[o_spec, o_spec],
        out_shape=[jax.ShapeDtypeStruct((b, NSA_KV, n, D_HEAD), BF16)] * 2,
        compiler_params=_params(("parallel",)),
        name="cmp_prompt",
    )(chunks, w1lt, pe2, w2)


DEC_ROWS = 16
MAX_PAGES_PER_STEP = 8


def _pages_per_step(n_pages):
    return max(d for d in range(1, MAX_PAGES_PER_STEP + 1) if n_pages % d == 0)


def _page_specs(n_per, pool, n_pages=None):
    blk = (1,) + pool.shape[1:]
    zeros = (0,) * (pool.ndim - 1)

    def spec(p):
        if n_pages is None:
            return pl.BlockSpec(blk, lambda b, c, pt: (pt[b, c * n_per + p],) + zeros)
        return pl.BlockSpec(blk, lambda b, c, pt: (pt[b, n_pages - (c + 1) * n_per + p],) + zeros)
    return [spec(p) for p in range(n_per)]


def _page_rows(pages, g, kv, n_groups=None):
    if n_groups is None:
        tiles = [pg[0, :, g, kv, :] for pg in pages]
    else:
        tiles = [pg[0, pl.ds(2 * g + kv, PAGE_SIZE, stride=2 * n_groups), :] for pg in pages]
    return jnp.concatenate(tiles, axis=0).astype(BF16)


def _softmax_steps(m_ref, l_ref, acc_ref, items):
    masked = items[0][3] is not None
    assert all((ok is not None) == masked for _, _, _, ok in items)
    rows = items[0][1].shape[0]
    cat = lambda xs: jnp.concatenate(xs, axis=0) if len(xs) > 1 else xs[0]
    s = cat([s for _, s, _, _ in items])
    if masked:
        ok = cat([ok for _, _, _, ok in items])
        s = jnp.where(ok, s, NEG_INF)
    m_old = cat([m_ref[g] for g, _, _, _ in items])
    l_old = cat([l_ref[g] for g, _, _, _ in items])
    m_new = jnp.maximum(m_old, jnp.max(s, axis=-1, keepdims=True))
    alpha = jnp.exp(m_old - m_new)
    p = jnp.exp(s - m_new)
    if masked:
        p = jnp.where(ok, p, 0.0)
    l_new = alpha * l_old + jnp.sum(p, axis=-1, keepdims=True)
    p = p.astype(BF16)
    for i, (g, _, v, _) in enumerate(items):
        sl = slice(i * rows, (i + 1) * rows)
        acc_ref[g] = alpha[sl] * acc_ref[g] + jnp.dot(p[sl], v, preferred_element_type=F32)
        m_ref[g] = m_new[sl]
        l_ref[g] = l_new[sl]


def _softmax_step(m_ref, l_ref, acc_ref, g, s, v, ok):
    _softmax_steps(m_ref, l_ref, acc_ref, [(g, s, v, ok)])


def _new_causal(t_new):
    return _iota((DEC_ROWS, DEC_ROWS), 1) <= _iota((DEC_ROWS, DEC_ROWS), 0) % t_new


def _diff_dec_kernel(pt_ref, lam_ref, q_ref, kn_ref, vn_ref, sub_ref, *rest, n_per, lam_init, t_new):
    pages, o_ref, (m_ref, l_ref, acc_ref) = rest[:n_per], rest[n_per], rest[n_per + 1:]
    c = pl.program_id(1)

    @pl.when(c == 0)
    def _():
        m_ref[...] = jnp.full(m_ref.shape, NEG_INF, F32)
        l_ref[...] = jnp.zeros(l_ref.shape, F32)
        acc_ref[...] = jnp.zeros(acc_ref.shape, F32)

    _softmax_steps(m_ref, l_ref, acc_ref, [
        (g, lax.dot_general(q_ref[0, g], _page_rows(pages, g, 0), _NT, preferred_element_type=F32),
         _page_rows(pages, g, 1), None) for g in range(DA_KV)])

    @pl.when(c == pl.num_programs(1) - 1)
    def _():
        lp = lam_ref[...]
        lam = (jnp.exp(jnp.sum(lp[0:1] * lp[1:2], axis=-1, keepdims=True))
               - jnp.exp(jnp.sum(lp[2:3] * lp[3:4], axis=-1, keepdims=True)) + lam_init)
        for g in range(DA_KV):
            s = lax.dot_general(q_ref[0, g], kn_ref[0, g], _NT, preferred_element_type=F32)
            _softmax_step(m_ref, l_ref, acc_ref, g, s, vn_ref[0, g], _new_causal(t_new))
            a = acc_ref[g] / l_ref[g]
            half = DEC_ROWS // 2
            o = a[0:half] - lam * a[half:DEC_ROWS]
            o_ref[0, g] = _rms_rows(o, sub_ref[...]) * (1.0 - lam_init)


def diff_dec(page_table, pool, q, kn, vn, diff_lambda, diff_subln, lam_init, t_new):
    b, n_pages = page_table.shape
    n_per = _pages_per_step(n_pages)
    blk = lambda w: pl.BlockSpec((1, DA_KV, DEC_ROWS, w), lambda bi, c, pt: (bi, 0, 0, 0))
    grid_spec = pltpu.PrefetchScalarGridSpec(
        num_scalar_prefetch=1, grid=(b, n_pages // n_per),
        in_specs=[pl.BlockSpec((4, D_HEAD), lambda bi, c, pt: (0, 0)), blk(2 * D_HEAD), blk(2 * D_HEAD), blk(2 * D_HEAD),
                  pl.BlockSpec((1, 2 * D_HEAD), lambda bi, c, pt: (0, 0))] + _page_specs(n_per, pool),
        out_specs=pl.BlockSpec((1, DA_KV, DEC_ROWS // 2, 2 * D_HEAD), lambda bi, c, pt: (bi, 0, 0, 0)),
        scratch_shapes=[pltpu.VMEM((DA_KV, DEC_ROWS, 1), F32), pltpu.VMEM((DA_KV, DEC_ROWS, 1), F32),
                        pltpu.VMEM((DA_KV, DEC_ROWS, 2 * D_HEAD), F32)])
    return pl.pallas_call(
        functools.partial(_diff_dec_kernel, n_per=n_per, lam_init=lam_init, t_new=t_new),
        grid_spec=grid_spec,
        out_shape=jax.ShapeDtypeStruct((b, DA_KV, DEC_ROWS // 2, 2 * D_HEAD), F32),
        compiler_params=_params(("parallel", "arbitrary")),
        name="diff_dec",
    )(page_table, diff_lambda.astype(F32), q, kn, vn, diff_subln.reshape(1, -1).astype(F32), *([pool] * n_per))


def _sb_dec_kernel(pt_ref, q_ref, kn_ref, vn_ref, *rest, n_per, t_new):
    pages, o_ref, (c_ref, acc_ref) = rest[:n_per], rest[n_per], rest[n_per + 1:]
    c = pl.program_id(1)

    def tiles(kv_of, strict):
        kvs = [kv_of(g) for g in range(SB_KV)]
        n = kvs[0][0].shape[0]
        lane = _iota((1, n), 1)
        z = jnp.concatenate([lax.dot_general(q_ref[0, g], k, _NT, preferred_element_type=F32)
                             for g, (k, _) in enumerate(kvs)], axis=0)
        sp = jnp.maximum(z, 0.0) + jnp.log(1.0 + jnp.exp(-jnp.abs(z)))
        lk = -sp if strict is None else jnp.where(strict(lane), -sp, 0.0)
        y = lk
        d = 1
        while d < n:
            y = y + jnp.where(lane < n - d, pltpu.roll(y, n - d, 1), 0.0)
            d *= 2
        a = jnp.exp(z - sp + (y - lk) + c_ref[...])
        if strict is not None:
            a = jnp.where(strict(lane), a, 0.0)
        a = a.astype(BF16)
        c_ref[...] += y[:, 0:1]
        for g, (_, v) in enumerate(kvs):
            acc_ref[g] += jnp.dot(a[g * DEC_ROWS:(g + 1) * DEC_ROWS], v, preferred_element_type=F32)

    @pl.when(c == 0)
    def _():
        c_ref[...] = jnp.zeros(c_ref.shape, F32)
        acc_ref[...] = jnp.zeros(acc_ref.shape, F32)
        t_row = _iota((SB_KV * DEC_ROWS, 1), 0) % t_new
        tiles(lambda g: (kn_ref[0, g], vn_ref[0, g]), lambda lane: lane < t_row)

    tiles(lambda g: (_page_rows(pages, g, 0, SB_KV), _page_rows(pages, g, 1, SB_KV)), None)

    @pl.when(c == pl.num_programs(1) - 1)
    def _():
        o_ref[0] = acc_ref[...]


def sb_dec(page_table, pool, q, kn, vn, t_new):
    b, n_pages = page_table.shape
    n_per = _pages_per_step(n_pages)
    blk = lambda r: pl.BlockSpec((1, SB_KV, r, D_HEAD), lambda bi, c, pt: (bi, 0, 0, 0))
    grid_spec = pltpu.PrefetchScalarGridSpec(
        num_scalar_prefetch=1, grid=(b, n_pages // n_per),
        in_specs=[blk(DEC_ROWS), blk(PAGE_SIZE), blk(PAGE_SIZE)]
        + _page_specs(n_per, pool, n_pages=n_pages),
        out_specs=blk(DEC_ROWS),
        scratch_shapes=[pltpu.VMEM((SB_KV * DEC_ROWS, 1), F32), pltpu.VMEM((SB_KV, DEC_ROWS, D_HEAD), F32)])
    return pl.pallas_call(
        functools.partial(_sb_dec_kernel, n_per=n_per, t_new=t_new),
        grid_spec=grid_spec,
        out_shape=jax.ShapeDtypeStruct((b, SB_KV, DEC_ROWS, D_HEAD), F32),
        compiler_params=_params(("parallel", "arbitrary")),
        name="sb_dec",
    )(page_table, q, kn, vn, *([pool] * n_per))


def _nsa_cmp_dec_kernel(pt_ref, q_ref, xn_ref, w1_ref, pe_ref, w2_ref, cov_ref, exp_ref, *rest,
                        n_per, t_new, p_len, n_cmp, n_sel):
    pages = rest[:n_per]
    ocmp_ref, sel_ref = rest[n_per], rest[n_per + 1]
    lead_ref, trail_ref = rest[n_per + 2:]
    c = pl.program_id(1)
    row_w = NSA_KV * 2 * D_HEAD
    page_ch = PAGE_SIZE // CMP_STRIDE
    n_ch = page_ch * n_per
    cb = cov_ref.shape[0]
    sl = cov_ref.shape[1]

    @pl.when(c == 0)
    def _():
        lead_ref[...] = jnp.zeros(lead_ref.shape, F32)
        trail_ref[...] = jnp.zeros(trail_ref.shape, F32)

    def project(chunk_rows, base):
        for c4 in range(2 * NSA_KV):
            xc = jnp.concatenate([chunk_rows(c4 // 2, c4 % 2, r) for r in range(CMP_STRIDE)], axis=1).astype(BF16)
            lt = jnp.dot(xc, w1_ref[c4 % 2], preferred_element_type=F32)
            lead_ref[c4, pl.ds(base, xc.shape[0]), :] = lt[:, :D_HEAD]
            trail_ref[c4, pl.ds(base, xc.shape[0]), :] = lt[:, D_HEAD:]

    project(lambda g, kv, r: jnp.concatenate(
        [pg[0, pl.ds(2 * NSA_KV * r + 2 * g + kv, page_ch, stride=2 * NSA_KV * CMP_STRIDE), :] for pg in pages], axis=0),
        pl.multiple_of(c * n_ch, n_ch))

    @pl.when(c == pl.num_programs(1) - 1)
    def _():
        project(lambda g, kv, r: xn_ref[0, :, r * row_w + (2 * g + kv) * D_HEAD:r * row_w + (2 * g + kv + 1) * D_HEAD],
                p_len // CMP_STRIDE)
        qpos = p_len + _iota((DEC_ROWS, 1), 0) % t_new
        n_idx = _iota((1, cb), 1)
        c_ok = ((n_idx * CMP_STRIDE + (CMP_LEN - 1)) <= qpos) & (n_idx < n_cmp)
        rr = (_iota((DEC_ROWS, DEC_ROWS), 0) % t_new == _iota((DEC_ROWS, DEC_ROWS), 1) % t_new)
        rr = jnp.where(rr, 1.0, 0.0).astype(BF16)
        lane = _iota((1, sl), 1)
        cur = qpos // SEL_BLOCK
        forced = (lane == 0) | (lane == cur) | (lane == cur - 1)
        causal = lane * SEL_BLOCK <= qpos
        for g in range(NSA_KV):
            blocks = []
            for kv in range(2):
                c4 = g * 2 + kv
                pw = jnp.dot(pe_ref[kv], w1_ref[kv], preferred_element_type=F32)
                pe_w1 = pw[0:1, :D_HEAD] + pw[1:2, D_HEAD:]
                h = jax.nn.gelu(lead_ref[c4, 0:cb, :] + trail_ref[c4, 1:cb + 1, :] + pe_w1, approximate=True)
                blocks.append(jnp.dot(h.astype(BF16), w2_ref[kv], preferred_element_type=F32).astype(BF16))
            s = lax.dot_general(q_ref[0, g], blocks[0], _NT, preferred_element_type=F32)
            s = jnp.where(c_ok, s, NEG_INF)
            e = jnp.where(c_ok, jnp.exp(s - jnp.max(s, axis=-1, keepdims=True)), 0.0)
            den = jnp.sum(e, axis=-1, keepdims=True)
            p = e / jnp.where(den > 0.0, den, 1.0)
            ocmp_ref[0, g] = jnp.dot(p.astype(BF16), blocks[1], preferred_element_type=F32)
            hi, lo = _split_bf16(p)
            psum = jnp.dot(rr, hi, preferred_element_type=F32) + jnp.dot(rr, lo, preferred_element_type=F32)
            hi, lo = _split_bf16(psum)
            cov = cov_ref[...]
            imp = jnp.dot(hi, cov, preferred_element_type=F32) + jnp.dot(lo, cov, preferred_element_type=F32)
            score = jnp.where(forced, FORCED, jnp.where(causal, imp, NEG_INF))
            score = jnp.where(lane < n_sel, score, -3e38)
            cnt = jnp.zeros((DEC_ROWS, sl), F32)
            for sp in range(n_sel):
                col = score[:, sp:sp + 1]
                tie = jnp.where(lane > sp, 1.0, 0.0)
                cnt = cnt + jnp.where(col > score, 1.0, jnp.where(col == score, tie, 0.0))
            selm = jnp.where((cnt < float(min(SEL_TOPK, n_sel))) & (lane < n_sel), 1.0, 0.0).astype(BF16)
            sel_ref[0, g] = jnp.dot(selm, exp_ref[...], preferred_element_type=F32)


def nsa_cmp_dec(page_table, pool, q, xnew, w1lt, pe2, w2, t_new):
    b, n_pages = page_table.shape
    n_per = _pages_per_step(n_pages)
    p_len = n_pages * PAGE_SIZE
    total = p_len + SEL_BLOCK
    n_cmp = total // CMP_STRIDE - 1
    n_sel = total // SEL_BLOCK
    cb = ((p_len // CMP_STRIDE + 8 + LANE - 1) // LANE) * LANE
    sl = ((n_sel + LANE - 1) // LANE) * LANE
    assert t_new <= CMP_STRIDE and DEC_ROWS % t_new == 0
    cmp_i = _iota((cb, sl), 0)
    sel_i = _iota((cb, sl), 1)
    cover = ((cmp_i * CMP_STRIDE <= sel_i * SEL_BLOCK + SEL_BLOCK - 1)
             & (cmp_i * CMP_STRIDE + CMP_LEN - 1 >= sel_i * SEL_BLOCK)
             & (cmp_i < n_cmp) & (sel_i < n_sel)).astype(BF16)
    klen = p_len + LANE
    expand = (_iota((sl, klen), 0) == _iota((sl, klen), 1) // SEL_BLOCK).astype(BF16)
    full = lambda shape: pl.BlockSpec(shape, lambda bi, c, pt: (0,) * len(shape))
    qspec = pl.BlockSpec((1, NSA_KV, DEC_ROWS, D_HEAD), lambda bi, c, pt: (bi, 0, 0, 0))
    grid_spec = pltpu.PrefetchScalarGridSpec(
        num_scalar_prefetch=1, grid=(b, n_pages // n_per),
        in_specs=[qspec, pl.BlockSpec((1,) + xnew.shape[1:], lambda bi, c, pt: (bi, 0, 0)),
                  full(w1lt.shape), full(pe2.shape), full(w2.shape), full(cover.shape), full(expand.shape)]
        + _page_specs(n_per, pool),
        out_specs=[qspec, pl.BlockSpec((1, NSA_KV, DEC_ROWS, klen), lambda bi, c, pt: (bi, 0, 0, 0))],
        scratch_shapes=[pltpu.VMEM((2 * NSA_KV, cb + 8, D_HEAD), F32), pltpu.VMEM((2 * NSA_KV, cb + 8, D_HEAD), F32)])
    return pl.pallas_call(
        functools.partial(_nsa_cmp_dec_kernel, n_per=n_per, t_new=t_new, p_len=p_len, n_cmp=n_cmp, n_sel=n_sel),
        grid_spec=grid_spec,
        out_shape=[jax.ShapeDtypeStruct((b, NSA_KV, DEC_ROWS, D_HEAD), F32),
                   jax.ShapeDtypeStruct((b, NSA_KV, DEC_ROWS, klen), F32)],
        compiler_params=_params(("parallel", "arbitrary")),
        name="nsa_cmp_dec",
    )(page_table, q, xnew, w1lt, pe2, w2, cover, expand, *([pool] * n_per))


def _nsa_sel_dec_kernel(pt_ref, q_ref, selp_ref, seln_ref, ksn_ref, vsn_ref, ocmp_ref, ng_ref, win_ref, kwn_ref,
                        vwn_ref, *rest, n_per, t_new):
    pages, o_ref, (m_ref, l_ref, acc_ref) = rest[:n_per], rest[n_per], rest[n_per + 1:]
    c = pl.program_id(1)

    @pl.when(c == 0)
    def _():
        m_ref[...] = jnp.full(m_ref.shape, NEG_INF, F32)
        l_ref[...] = jnp.zeros(l_ref.shape, F32)
        acc_ref[...] = jnp.zeros(acc_ref.shape, F32)

    _softmax_steps(m_ref, l_ref, acc_ref, [
        (g, lax.dot_general(q_ref[0, g], _page_rows(pages, g, 0, NSA_KV), _NT, preferred_element_type=F32),
         _page_rows(pages, g, 1, NSA_KV), selp_ref[0, g] > 0.5) for g in range(NSA_KV)])

    @pl.when(c == pl.num_programs(1) - 1)
    def _():
        causal = _new_causal(t_new)
        wb = win_ref.shape[1] // (2 * NSA_KV)
        t_row = _iota((DEC_ROWS, 1), 0) % t_new
        w_ok = _iota((1, wb), 1) > t_row
        for g in range(NSA_KV):
            q = q_ref[0, g]
            s = lax.dot_general(q, ksn_ref[0, g], _NT, preferred_element_type=F32)
            _softmax_step(m_ref, l_ref, acc_ref, g, s, vsn_ref[0, g], (seln_ref[0, g, :, 0:DEC_ROWS] > 0.5) & causal)
            o_sel = acc_ref[g] / l_ref[g]
            kw = win_ref[0, pl.ds(2 * g, wb, stride=2 * NSA_KV), :].astype(BF16)
            vw = win_ref[0, pl.ds(2 * g + 1, wb, stride=2 * NSA_KV), :].astype(BF16)
            sw = jnp.where(w_ok, lax.dot_general(q, kw, _NT, preferred_element_type=F32), NEG_INF)
            sn = jnp.where(causal, lax.dot_general(q, kwn_ref[0, g], _NT, preferred_element_type=F32), NEG_INF)
            mx = jnp.maximum(jnp.max(sw, axis=-1, keepdims=True), jnp.max(sn, axis=-1, keepdims=True))
            ew = jnp.where(w_ok, jnp.exp(sw - mx), 0.0)
            en = jnp.where(causal, jnp.exp(sn - mx), 0.0)
            den = jnp.sum(ew, axis=-1, keepdims=True) + jnp.sum(en, axis=-1, keepdims=True)
            o_win = (jnp.dot(ew.astype(BF16), vw, preferred_element_type=F32)
                     + jnp.dot(en.astype(BF16), vwn_ref[0, g], preferred_element_type=F32)) / den
            gates = jax.nn.sigmoid(ng_ref[0, g])
            o_ref[0, g] = gates[:, 0:1] * ocmp_ref[0, g] + gates[:, 1:2] * o_sel + gates[:, 2:3] * o_win


def nsa_sel_dec(page_table, pool, q, selexp, ksn, vsn, o_cmp, ng, win, kwn, vwn, t_new):
    b, n_pages = page_table.shape
    n_per = _pages_per_step(n_pages)
    assert win.shape[1] == WINDOW * 2 * NSA_KV
    blk = pl.BlockSpec((1, NSA_KV, DEC_ROWS, D_HEAD), lambda bi, c, pt: (bi, 0, 0, 0))
    grid_spec = pltpu.PrefetchScalarGridSpec(
        num_scalar_prefetch=1, grid=(b, n_pages // n_per),
        in_specs=[blk, pl.BlockSpec((1, NSA_KV, DEC_ROWS, n_per * PAGE_SIZE), lambda bi, c, pt: (bi, 0, 0, c)),
                  pl.BlockSpec((1, NSA_KV, DEC_ROWS, LANE), lambda bi, c, pt: (bi, 0, 0, n_pages)),
                  blk, blk, blk, blk,
                  pl.BlockSpec((1,) + win.shape[1:], lambda bi, c, pt: (bi, 0, 0)), blk, blk]
        + _page_specs(n_per, pool),
        out_specs=blk,
        scratch_shapes=[pltpu.VMEM((NSA_KV, DEC_ROWS, 1), F32), pltpu.VMEM((NSA_KV, DEC_ROWS, 1), F32),
                        pltpu.VMEM((NSA_KV, DEC_ROWS, D_HEAD), F32)])
    return pl.pallas_call(
        functools.partial(_nsa_sel_dec_kernel, n_per=n_per, t_new=t_new),
        grid_spec=grid_spec,
        out_shape=jax.ShapeDtypeStruct((b, NSA_KV, DEC_ROWS, D_HEAD), F32),
        compiler_params=_params(("parallel", "arbitrary")),
        name="nsa_sel_dec",
    )(page_table, q, selexp, selexp, ksn, vsn, o_cmp, ng, win, kwn, vwn, *([pool] * n_per))


def _rmsnorm(x, g):
    xf = x.astype(F32)
    y = xf * lax.rsqrt(jnp.mean(xf * xf, axis=-1, keepdims=True) + EPS)
    return (y * g.astype(F32)).astype(x.dtype)


def _masked_softmax(s, mask):
    return jax.nn.softmax(jnp.where(mask, s, NEG_INF), axis=-1)


def _rope(x, pos):
    half = ROPE_DIM // 2
    inv = ROPE_THETA ** (-(jnp.arange(half, dtype=F32) * 2.0 / ROPE_DIM))
    ang = pos.astype(F32)[:, None] * inv[None, :]
    ang = ang.reshape((1, ang.shape[0]) + (1,) * (x.ndim - 3) + (half,))
    cos, sin = jnp.cos(ang), jnp.sin(ang)
    x1, x2 = x[..., :half], x[..., half:ROPE_DIM]
    return jnp.concatenate([x1 * cos - x2 * sin, x2 * cos + x1 * sin, x[..., ROPE_DIM:]], axis=-1)


def _over_query_blocks(fn, block, qpos, *qs):
    t = qpos.shape[0]
    if t <= block or t % block:
        return fn(qpos, *qs)
    n = t // block
    split = lambda a: jnp.moveaxis(a.reshape((a.shape[0], n, block) + a.shape[2:]), 1, 0)
    out = lax.map(lambda a: fn(*a), (qpos.reshape(n, block),) + tuple(split(q) for q in qs))
    out = jnp.moveaxis(out, 0, 1)
    return out.reshape((out.shape[0], t) + out.shape[3:])


def _pad_rows(a, multiple):
    extra = (-a.shape[1]) % multiple
    if extra == 0:
        return a
    return jnp.pad(a, ((0, 0), (0, extra)) + ((0, 0),) * (a.ndim - 2))


def _diff_attention(kpos, k, v, lam, sub_g, lam_init):
    scale = D_HEAD ** -0.5
    def fn(qpos, q):
        s = jnp.einsum('btgrcd,bsgcd->bgrcts', q, k, preferred_element_type=F32) * scale
        p = _masked_softmax(s, kpos[None, :] <= qpos[:, None])
        a = p[:, :, :, 0] - lam * p[:, :, :, 1]
        o = jnp.einsum('bgrts,bsgd->btgrd', a.astype(v.dtype), v)
        return _rmsnorm(o, sub_g) * (1.0 - lam_init)
    return fn


def _stick_breaking(kpos, k, v):
    scale = D_HEAD ** -0.5
    def fn(qpos, q):
        z = jnp.einsum('btgrd,bsgd->bgrts', q, k, preferred_element_type=F32) * scale
        strict = kpos[None, :] < qpos[:, None]
        log_keep = jnp.where(strict, jax.nn.log_sigmoid(-z), 0.0)
        later = lax.cumsum(log_keep, axis=z.ndim - 1, reverse=True) - log_keep
        a = jnp.where(strict, jnp.exp(jax.nn.log_sigmoid(z) + later), 0.0)
        return jnp.einsum('bgrts,bsgd->btgrd', a.astype(v.dtype), v)
    return fn


def _nsa_compress(k, w1, w2, pe):
    b, l, g, d = k.shape
    chunks = k.reshape(b, l // CMP_STRIDE, CMP_STRIDE, g, d).transpose(0, 1, 3, 2, 4)
    chunks = chunks.reshape(b, l // CMP_STRIDE, g, CMP_STRIDE * d)
    half = CMP_STRIDE * d
    lead = chunks @ w1[:half]
    trail = chunks @ w1[half:]
    h = jax.nn.gelu(lead[:, :-1] + trail[:, 1:] + pe.reshape(-1) @ w1, approximate=True)
    return h @ w2


def _nsa_cmp_sel(kc, vc, ks, vs):
    b, n_keys, g, d = ks.shape
    n_cmp = kc.shape[1]
    n_sel = n_keys // SEL_BLOCK
    top = min(SEL_TOPK, n_sel)
    scale = D_HEAD ** -0.5
    cmp_start = jnp.arange(n_cmp) * CMP_STRIDE
    cmp_end = cmp_start + CMP_LEN - 1
    blk = jnp.arange(n_sel)
    cover = ((cmp_start[:, None] <= blk[None, :] * SEL_BLOCK + SEL_BLOCK - 1)
             & (cmp_end[:, None] >= blk[None, :] * SEL_BLOCK)).astype(F32)
    ksb = ks.reshape(b, n_sel, SEL_BLOCK, g, d).transpose(0, 3, 1, 2, 4)
    vsb = vs.reshape(b, n_sel, SEL_BLOCK, g, d).transpose(0, 3, 1, 2, 4)
    bi = jnp.arange(b)[:, None, None, None]
    gi = jnp.arange(g)[None, None, :, None]
    offs = jnp.arange(SEL_BLOCK)

    def fn(qpos, q, q_rot, g_cmp, g_sel):
        tq = qpos.shape[0]
        c_ok = (cmp_end[None, :] <= qpos[:, None])[None, :, None, None, :]
        s = jnp.einsum('btgrd,bngd->btgrn', q, kc, preferred_element_type=F32) * scale
        p = jnp.where(c_ok, _masked_softmax(s, c_ok), 0.0)
        o_cmp = jnp.einsum('btgrn,bngd->btgrd', p.astype(vc.dtype), vc)
        imp = jnp.einsum('btgn,ns->btgs', p.sum(axis=3), cover)
        cur = qpos // SEL_BLOCK
        forced = (blk[None, :] == 0) | (blk[None, :] == cur[:, None]) | (blk[None, :] == cur[:, None] - 1)
        causal = blk[None, :] * SEL_BLOCK <= qpos[:, None]
        score = jnp.where(forced[None, :, None, :], FORCED,
                          jnp.where(causal[None, :, None, :], imp, NEG_INF))
        idx = lax.top_k(score, top)[1]
        gk = ksb[bi, gi, idx].reshape(b, tq, g, top * SEL_BLOCK, d)
        gv = vsb[bi, gi, idx].reshape(b, tq, g, top * SEL_BLOCK, d)
        kpos = (idx[..., None] * SEL_BLOCK + offs).reshape(b, tq, g, 1, top * SEL_BLOCK)
        ss = jnp.einsum('btgrd,btgmd->btgrm', q_rot, gk, preferred_element_type=F32) * scale
        ps = _masked_softmax(ss, kpos <= qpos[None, :, None, None, None])
        o_sel = jnp.einsum('btgrm,btgmd->btgrd', ps.astype(gv.dtype), gv)
        return g_cmp[..., None] * o_cmp + g_sel[..., None] * o_sel
    return fn


def _banded_attn(q, kv, qpos, kpos):
    s = jnp.einsum('bnqgrd,bnkgd->bngrqk', q, kv[..., 0, :], preferred_element_type=F32) * D_HEAD ** -0.5
    dist = qpos[:, :, None] - kpos[:, None, :]
    ok = (dist >= 0) & (dist < WINDOW) & (kpos[:, None, :] >= 0)
    p = _masked_softmax(s, ok[None, :, None, None])
    return jnp.einsum('bngrqk,bnkgd->bnqgrd', p.astype(kv.dtype), kv[..., 1, :])


def _window_prompt(q, kv):
    b, t = q.shape[:2]
    nb = t // Q_BLOCK
    kvp = jnp.pad(kv, ((0, 0), (WINDOW, 0), (0, 0), (0, 0), (0, 0)))
    idx = jnp.arange(nb)[:, None] * Q_BLOCK + jnp.arange(WINDOW + Q_BLOCK)[None, :]
    qpos = jnp.arange(nb)[:, None] * Q_BLOCK + jnp.arange(Q_BLOCK)[None, :]
    o = _banded_attn(q.reshape((b, nb, Q_BLOCK) + q.shape[2:]), kvp[:, idx], qpos, idx - WINDOW)
    return o.reshape(q.shape)


def _window_sample(q, pos, kv_all, start):
    kpos = start + jnp.arange(kv_all.shape[1])
    return _banded_attn(q[:, None], kv_all[:, None], pos[None], kpos[None])[:, 0]


def _prep_weights(w_in, w_br_a, w_br_b, w_br_c, w_out, w_ff_gate, w_ff_up, w_ff_down, w_ple, w_ple_gate):
    w_main = jnp.concatenate([w_in[:, :NG_AT], w_in[:, NG_AT + NG_W:]], axis=1).astype(BF16)
    w_ng = jnp.pad(w_in[:, NG_AT:NG_AT + NG_W], ((0, 0), (0, LANE - NG_W))).astype(BF16)
    c = lambda a: a.astype(BF16)
    fpad = D_FFP - D_FF
    w_ff_gate = jnp.pad(w_ff_gate, ((0, 0), (0, fpad)))
    w_ff_up = jnp.pad(w_ff_up, ((0, 0), (0, fpad)))
    w_ff_down = jnp.pad(w_ff_down, ((0, fpad), (0, 0)))
    return (w_main, w_ng, c(w_br_a), c(w_br_b), c(w_br_c), c(w_out), c(w_ff_gate), c(w_ff_up),
            c(w_ff_down), c(w_ple), c(w_ple_gate))


def _layer(x, pe, past, li, ln1, wts, diff_lambda, diff_subln, cmp_w1, cmp_w2, cmp_pe,
           ln2, ff_conv_w, ff_conv_b, ln3):
    (w_main, w_ng, w_br_a, w_br_b, w_br_c, w_out, w_ff_gate, w_ff_up, w_ff_down, w_ple, w_ple_gate) = wts
    b, t, _ = x.shape
    m = b * t
    p_len = 0 if past is None else past[0].shape[1] * PAGE_SIZE
    pos = p_len + jnp.arange(t, dtype=jnp.int32)
    x2 = x.reshape(m, D_MODEL)
    if past is not None:
        page_table, pool_diff, pool_cmp, pool_sel, pool_sb, win_buf, conv_buf = past
        n_pool = pool_diff.shape[0]
        assert DEC_ROWS == t * NSA_HEADS // NSA_KV == 2 * t * DA_HEADS // DA_KV and DEC_ROWS >= t * SB_HEADS // SB_KV

    def dec_rows(a, rows=DEC_ROWS):
        g, d = a.shape[2], a.shape[-1]
        a = jnp.moveaxis(a, 1, -2).reshape(b, g, -1, d)
        return jnp.pad(a, ((0, 0), (0, 0), (0, rows - a.shape[2]), (0, 0))).astype(BF16)

    def dec_out(o, rn):
        g, d = o.shape[1], o.shape[-1]
        o = o[:, :, :rn * t].reshape(b, g, rn, t, d)
        return jnp.transpose(o, (0, 3, 1, 2, 4)).reshape(b, t, g * rn * d)

    proj = norm_mm(x2, ln1, w_main)
    ng = norm_mm(x2, ln1, w_ng)[:, :NG_W].reshape(b, t, NG_W)
    pos_rows = pos if t % 8 == 0 else jnp.tile(pos, b)
    (da_q, da_k, da_v, nq, nq_rot, ks, vs, kw, vw, sq, sk, sv,
     da_new, cmp_new, sel_new, win_new, sb_new) = qkv_post(proj, pos_rows)
    seq = lambda a: a.reshape(b, t, -1)
    heads = lambda a, g: a.reshape(b, t, g, -1, D_HEAD)
    da_new = da_new.reshape(b, t, DA_KV, 2, 2 * D_HEAD)
    cmp_new, sel_new, win_new = [a.reshape(b, t, NSA_KV, 2, D_HEAD) for a in (cmp_new, sel_new, win_new)]
    sb_new = sb_new.reshape(b, t, SB_KV, 2, D_HEAD)
    lam_init = 0.8 - 0.6 * math.exp(-0.3 * li)
    rn = NSA_HEADS // NSA_KV
    chunk_w = CMP_STRIDE * NSA_KV * 2 * D_HEAD
    half = CMP_STRIDE * D_HEAD
    w1lt = jnp.concatenate([cmp_w1[:, :half], cmp_w1[:, half:]], axis=2).astype(BF16)
    pe2 = jnp.pad(cmp_pe.reshape(2, 2, half), ((0, 0), (0, DEC_ROWS - 2), (0, 0))).astype(BF16)
    if past is None:
        o_da = diff_prompt(seq(da_q), seq(da_k), seq(da_v), diff_lambda, diff_subln, lam_init)
        chunks = cmp_new.reshape(b, t // CMP_STRIDE, chunk_w)
        chunks = jnp.pad(chunks, ((0, 0), (0, LANE - t // CMP_STRIDE), (0, 0)))
        kc_blk, vc_blk = cmp_prompt(chunks, w1lt, pe2, cmp_w2.astype(BF16))
        ng_pad = jnp.pad(ng.reshape(b, t, NSA_KV, 3 * rn), ((0, 0), (0, 0), (0, 0), (0, LANE - 3 * rn)))
        o_nsa = nsa_prompt(seq(nq), seq(nq_rot), kc_blk, vc_blk, seq(ks), seq(vs), seq(kw), seq(vw),
                           ng_pad.reshape(b, t, NSA_KV * LANE))
        o_sb = sb_prompt(seq(sq), seq(sk), seq(sv))
        win_all = win_new
    else:
        rows_view = lambda a: a.reshape(a.shape[0], -1, D_HEAD)
        pool_cmp, pool_sel, pool_sb, win_rows = [rows_view(a) for a in (pool_cmp, pool_sel, pool_sb, win_buf)]
        q_bd =(jnp.swapaxes(da_q.reshape(b, t, DA_KV, DA_HEADS // DA_KV, 2, D_HEAD), 3, 4)[..., None, :]
                * jnp.eye(2, dtype=BF16)[:, None, :, None])
        o = diff_dec(page_table, pool_diff,
                     dec_rows(q_bd.reshape(b, t, DA_KV, 2, DA_HEADS // DA_KV, 2 * D_HEAD)),
                     dec_rows(da_k.reshape(b, t, DA_KV, 2 * D_HEAD)), dec_rows(da_v.reshape(b, t, DA_KV, 2 * D_HEAD)),
                     diff_lambda, diff_subln, lam_init, t)
        o_da = dec_out(o, DA_HEADS // DA_KV)
        xnew = jnp.pad(cmp_new.reshape(b, 1, -1), ((0, 0), (0, 7), (0, chunk_w - t * NSA_KV * 2 * D_HEAD)))
        o_cmp, selexp = nsa_cmp_dec(page_table, pool_cmp,
                                    dec_rows(heads(nq, NSA_KV)), xnew, w1lt, pe2, cmp_w2.astype(BF16), t)
        ng_rows = jnp.moveaxis(ng.reshape(b, t, NSA_KV, rn, 3), 1, 3).reshape(b, NSA_KV, rn * t, 3)
        ng_rows = jnp.pad(ng_rows, ((0, 0), (0, 0), (0, 0), (0, LANE - 3)))
        o = nsa_sel_dec(page_table, pool_sel, dec_rows(heads(nq_rot, NSA_KV)), selexp,
                        dec_rows(heads(ks, NSA_KV)), dec_rows(heads(vs, NSA_KV)), o_cmp, ng_rows,
                        win_rows,
                        dec_rows(heads(kw, NSA_KV)), dec_rows(heads(vw, NSA_KV)), t)
        o_nsa = dec_out(o, rn)
        o = sb_dec(page_table, pool_sb, dec_rows(heads(sq, SB_KV)),
                   dec_rows(heads(sk, SB_KV), PAGE_SIZE), dec_rows(heads(sv, SB_KV), PAGE_SIZE), t)
        o_sb = dec_out(o, SB_HEADS // SB_KV)
        win_all = jnp.concatenate([win_buf, win_new], axis=1)
    win_state = win_all[:, win_all.shape[1] - min(WINDOW, win_all.shape[1]):]

    merged = merge_mm(o_da.reshape(m, DA_QW).astype(BF16), o_nsa.reshape(m, NSA_QW).astype(BF16),
                      o_sb.reshape(m, SB_QW).astype(BF16), proj, w_br_a, w_br_b, w_br_c,
                      g_col=proj.shape[1] - 3 * D_MODEL)
    x2 = resid_mm(x2, merged, w_out)

    fpad = D_FFP - D_FF
    conv_b = jnp.pad(ff_conv_b, (0, fpad))
    cw = jnp.pad(ff_conv_w, ((0, 0), (0, fpad)))
    if past is None:
        act, tail = ffn_act(x2, ln2, w_ff_gate, w_ff_up, cw, conv_b, t)
        conv_state = tail[:, 8 - (CONV_W - 1):, :D_FF]
    else:
        gate_in, up = ffn_up(x2, ln2, w_ff_gate, w_ff_up)
        gp = jnp.concatenate([jnp.pad(conv_buf, ((0, 0), (0, 0), (0, fpad))), gate_in.reshape(b, t, D_FFP)], axis=1)
        conv = conv_b
        for i in range(CONV_W):
            conv = conv + cw[i] * gp[:, i:i + t]
        act = (jax.nn.gelu(conv, approximate=True) * up.reshape(b, t, D_FFP)).astype(BF16).reshape(m, D_FFP)
        conv_state = gp[:, t:, :D_FF]
    x2 = resid_mm(x2, act, w_ff_down)

    x2 = ple_mm(x2, ln3, w_ple_gate, pe.reshape(m, -1).astype(BF16), w_ple)
    return x2.reshape(b, t, D_MODEL), (da_new, cmp_new, sel_new, sb_new, win_state, conv_state)


def kernel(x_prompt, x_sample, cache_diff, cache_cmp, cache_sel, cache_sb, state_win, state_conv, page_table, p_prompt, p_sample, ln1, w_in, diff_lambda, diff_subln, cmp_w1, cmp_w2, cmp_pe, w_br_a, w_br_b, w_br_c, w_out, ln2, w_ff_gate, w_ff_up, w_ff_down, ff_conv_w, ff_conv_b, ln3, w_ple, w_ple_gate, ln_f):
    n_pool = cache_diff.shape[1]
    pools = [c.reshape((DEPTH * n_pool,) + c.shape[2:]) for c in (cache_diff, cache_cmp, cache_sel, cache_sb)]
    xp, xs = x_prompt, x_sample
    st_p, st_s = [], []
    for i in range(DEPTH):
        wts = _prep_weights(w_in[i], w_br_a[i], w_br_b[i], w_br_c[i], w_out[i], w_ff_gate[i], w_ff_up[i],
                            w_ff_down[i], w_ple[i], w_ple_gate[i])
        rest = (diff_lambda[i], diff_subln[i], cmp_w1[i], cmp_w2[i], cmp_pe[i], ln2[i], ff_conv_w[i],
                ff_conv_b[i], ln3[i])
        xp, sp = _layer(xp, p_prompt[i], None, i, ln1[i], wts, *rest)
        past = (page_table + i * n_pool, *pools, state_win[i], state_conv[i])
        xs, ss = _layer(xs, p_sample[i], past, i, ln1[i], wts, *rest)
        st_p.append(sp)
        st_s.append(ss)
    diff_p, cmp_p, sel_p, sb_p, win_p, conv_p = [jnp.stack(a) for a in zip(*st_p)]
    diff_s, cmp_s, sel_s, sb_s, win_s, conv_s = [jnp.stack(a) for a in zip(*st_s)]
    y_prompt = rmsnorm_rows(xp.reshape(-1, D_MODEL), ln_f).reshape(xp.shape)
    y_sample = rmsnorm_rows(xs.reshape(-1, D_MODEL), ln_f).reshape(xs.shape)
    return (y_prompt, y_sample, diff_p, diff_s, cmp_p, cmp_s, sel_p, sel_s, sb_p, sb_s, win_p, win_s, conv_p, conv_s)
```

```python
import functools
import math

import jax
import jax.numpy as jnp
from jax import lax
from jax.experimental import pallas as pl
from jax.experimental.pallas import tpu as pltpu

F32 = jnp.float32
BF16 = jnp.bfloat16

D_MODEL = 2048
DEPTH = 2
PAGE_SIZE = 128
D_HEAD = 128
ROPE_DIM = D_HEAD // 4
ROPE_THETA = 500000.0
DA_HEADS = 4
DA_KV = 2
NSA_HEADS = 8
NSA_KV = 2
CMP_STRIDE = 16
CMP_LEN = 2 * CMP_STRIDE
SEL_BLOCK = 64
SEL_TOPK = 16
WINDOW = 512
SB_HEADS = 8
SB_KV = 4
D_FF = ((8 * D_MODEL // 3 + 127) // 128) * 128
D_FFP = ((D_FF + 511) // 512) * 512
CONV_W = 3
Q_BLOCK = 128
SEL_Q_BLOCK = 32
NEG_INF = -1e30
FORCED = 1e30
EPS = 1e-6

DA_QW = DA_HEADS * 2 * D_HEAD
DA_KW = DA_KV * 2 * D_HEAD
NSA_QW = NSA_HEADS * D_HEAD
NSA_KW = NSA_KV * D_HEAD
SB_QW = SB_HEADS * D_HEAD
SB_KW = SB_KV * D_HEAD
SPLITS = (DA_QW, DA_KW, DA_KW, NSA_QW, 6 * NSA_KW, 3 * NSA_HEADS, SB_QW, SB_KW, SB_KW, 3 * D_MODEL)
SPLIT_AT = tuple(sum(SPLITS[:i + 1]) for i in range(len(SPLITS) - 1))
NG_AT = SPLIT_AT[4]
NG_W = 3 * NSA_HEADS
LANE = 128
VMEM_LIMIT = 48 * 1024 * 1024


def _params(sem):
    return pltpu.CompilerParams(dimension_semantics=sem, vmem_limit_bytes=VMEM_LIMIT)


def _rms_rows(x, g):
    return x * lax.rsqrt(jnp.mean(x * x, axis=-1, keepdims=True) + EPS) * g


def _tile(n, pref):
    t = min(n, pref)
    while n % t and t % 2 == 0:
        t //= 2
    assert n % t == 0, (n, pref)
    return t


def _norm_mm_kernel(x_ref, g_ref, w_ref, o_ref, h_ref):
    @pl.when(pl.program_id(1) == 0)
    def _():
        h_ref[...] = _rms_rows(x_ref[...], g_ref[...]).astype(BF16)
    o_ref[...] = jnp.dot(h_ref[...], w_ref[...], preferred_element_type=F32).astype(o_ref.dtype)


def norm_mm(x, g, w, out_dtype=F32, tm=1024, tn=512):
    m, k = x.shape
    n = w.shape[1]
    tm, tn = _tile(m, tm), _tile(n, tn)
    return pl.pallas_call(
        _norm_mm_kernel,
        grid=(m // tm, n // tn),
        in_specs=[pl.BlockSpec((tm, k), lambda i, j: (i, 0)),
                  pl.BlockSpec((1, k), lambda i, j: (0, 0)),
                  pl.BlockSpec((k, tn), lambda i, j: (0, j))],
        out_specs=pl.BlockSpec((tm, tn), lambda i, j: (i, j)),
        out_shape=jax.ShapeDtypeStruct((m, n), out_dtype),
        scratch_shapes=[pltpu.VMEM((tm, k), BF16)],
        compiler_params=_params(("parallel", "arbitrary")),
        name="norm_mm",
    )(x, g.reshape(1, k), w)


def _resid_mm_kernel(x_ref, a_ref, w_ref, o_ref):
    o_ref[...] = x_ref[...] + jnp.dot(a_ref[...], w_ref[...], preferred_element_type=F32)


def resid_mm(x, a, w, tm=512, tn=512):
    m, k = a.shape
    n = w.shape[1]
    tm, tn = _tile(m, tm), _tile(n, tn)
    return pl.pallas_call(
        _resid_mm_kernel,
        grid=(m // tm, n // tn),
        in_specs=[pl.BlockSpec((tm, tn), lambda i, j: (i, j)),
                  pl.BlockSpec((tm, k), lambda i, j: (i, 0)),
                  pl.BlockSpec((k, tn), lambda i, j: (0, j))],
        out_specs=pl.BlockSpec((tm, tn), lambda i, j: (i, j)),
        out_shape=jax.ShapeDtypeStruct((m, n), F32),
        compiler_params=_params(("parallel", "arbitrary")),
        name="resid_mm",
    )(x, a, w)


def _merge_mm_kernel(oa_ref, ob_ref, oc_ref, ga_ref, gb_ref, gc_ref, wa_ref, wb_ref, wc_ref, o_ref):
    def br(o_r, g_r, w_r):
        y = jnp.dot(o_r[...], w_r[...], preferred_element_type=F32)
        return jax.nn.sigmoid(g_r[...]) * y
    o_ref[...] = (br(oa_ref, ga_ref, wa_ref) + br(ob_ref, gb_ref, wb_ref)
                  + br(oc_ref, gc_ref, wc_ref)).astype(o_ref.dtype)


def merge_mm(oa, ob, oc, bg, wa, wb, wc, g_col=0, tm=512, tn=512):
    m, k = oa.shape
    n = wa.shape[1]
    tm, tn = _tile(m, tm), _tile(n, tn)
    nj = n // tn
    assert g_col % tn == 0
    g0 = g_col // tn
    o_spec = pl.BlockSpec((tm, k), lambda i, j: (i, 0))
    w_spec = pl.BlockSpec((k, tn), lambda i, j: (0, j))
    g_specs = [pl.BlockSpec((tm, tn), lambda i, j, s=s: (i, g0 + s * nj + j)) for s in range(3)]
    return pl.pallas_call(
        _merge_mm_kernel,
        grid=(m // tm, nj),
        in_specs=[o_spec, o_spec, o_spec] + g_specs + [w_spec, w_spec, w_spec],
        out_specs=pl.BlockSpec((tm, tn), lambda i, j: (i, j)),
        out_shape=jax.ShapeDtypeStruct((m, n), BF16),
        compiler_params=_params(("parallel", "arbitrary")),
        name="merge_mm",
    )(oa, ob, oc, bg, bg, bg, wa, wb, wc)


def _ffn_up_kernel(x_ref, g_ref, wg_ref, wu_ref, og_ref, ou_ref, h_ref):
    @pl.when(pl.program_id(1) == 0)
    def _():
        h_ref[...] = _rms_rows(x_ref[...], g_ref[...]).astype(BF16)
    h = h_ref[...]
    og_ref[...] = jnp.dot(h, wg_ref[...], preferred_element_type=F32)
    ou_ref[...] = jnp.dot(h, wu_ref[...], preferred_element_type=F32)


def ffn_up(x, g, wg, wu, tm=512, tn=512):
    m, k = x.shape
    n = wg.shape[1]
    tm, tn = _tile(m, tm), _tile(n, tn)
    w_spec = pl.BlockSpec((k, tn), lambda i, j: (0, j))
    o_spec = pl.BlockSpec((tm, tn), lambda i, j: (i, j))
    return pl.pallas_call(
        _ffn_up_kernel,
        grid=(m // tm, n // tn),
        in_specs=[pl.BlockSpec((tm, k), lambda i, j: (i, 0)),
                  pl.BlockSpec((1, k), lambda i, j: (0, 0)), w_spec, w_spec],
        out_specs=[o_spec, o_spec],
        out_shape=[jax.ShapeDtypeStruct((m, n), F32), jax.ShapeDtypeStruct((m, n), F32)],
        scratch_shapes=[pltpu.VMEM((tm, k), BF16)],
        compiler_params=_params(("parallel", "arbitrary")),
        name="ffn_up",
    )(x, g.reshape(1, k), wg, wu)


def _ffn_act_kernel(x_ref, g_ref, wg_ref, wu_ref, cw_ref, cb_ref, act_ref, tail_ref, h_ref, carry_ref, *, seq_tiles):
    i, j = pl.program_id(0), pl.program_id(1)

    @pl.when(j == 0)
    def _():
        h_ref[...] = _rms_rows(x_ref[...], g_ref[...]).astype(BF16)
    h = h_ref[...]
    gate = jnp.dot(h, wg_ref[...], preferred_element_type=F32)
    up = jnp.dot(h, wu_ref[...], preferred_element_type=F32)
    tm = gate.shape[0]

    @pl.when(i % seq_tiles == 0)
    def _():
        carry_ref[j] = jnp.zeros(carry_ref.shape[1:], F32)
    carry = carry_ref[j]
    row = _iota((tm, 1), 0)
    g1 = jnp.where(row == 0, carry[7:8], pltpu.roll(gate, 1, 0))
    g2 = jnp.where(row == 0, carry[6:7], jnp.where(row == 1, carry[7:8], pltpu.roll(gate, 2, 0)))
    cw = cw_ref[...]
    conv = cb_ref[...] + cw[0:1] * g2 + cw[1:2] * g1 + cw[2:3] * gate
    act_ref[...] = (jax.nn.gelu(conv, approximate=True) * up).astype(act_ref.dtype)
    carry_ref[j] = gate[tm - 8:tm]
    tail_ref[0] = gate[tm - 8:tm]


def ffn_act(x, g, wg, wu, cw, cb, seq_len, tm=1024, tn=512):
    m, k = x.shape
    n = wg.shape[1]
    tm, tn = _tile(seq_len, tm), _tile(n, tn)
    seq_tiles = seq_len // tm
    w_spec = pl.BlockSpec((k, tn), lambda i, j: (0, j))
    act, tails = pl.pallas_call(
        functools.partial(_ffn_act_kernel, seq_tiles=seq_tiles),
        grid=(m // tm, n // tn),
        in_specs=[pl.BlockSpec((tm, k), lambda i, j: (i, 0)),
                  pl.BlockSpec((1, k), lambda i, j: (0, 0)), w_spec, w_spec,
                  pl.BlockSpec((8, tn), lambda i, j: (0, j)), pl.BlockSpec((1, tn), lambda i, j: (0, j))],
        out_specs=[pl.BlockSpec((tm, tn), lambda i, j: (i, j)),
                   pl.BlockSpec((1, 8, tn), lambda i, j: (i, 0, j))],
        out_shape=[jax.ShapeDtypeStruct((m, n), BF16), jax.ShapeDtypeStruct((m // tm, 8, n), F32)],
        scratch_shapes=[pltpu.VMEM((tm, k), BF16), pltpu.VMEM((n // tn, 8, tn), F32)],
        compiler_params=_params(("arbitrary", "arbitrary")),
        name="ffn_act",
    )(x, g.reshape(1, k), wg, wu, jnp.pad(cw, ((0, 8 - cw.shape[0]), (0, 0))), cb.reshape(1, n))
    return act, tails[seq_tiles - 1::seq_tiles]


def _ple_kernel(x_ref, xt_ref, g_ref, wg_ref, pe_ref, wp_ref, o_ref, h_ref):
    @pl.when(pl.program_id(1) == 0)
    def _():
        h_ref[...] = _rms_rows(x_ref[...], g_ref[...]).astype(BF16)
    gate = jax.nn.sigmoid(jnp.dot(h_ref[...], wg_ref[...], preferred_element_type=F32))
    emb = jnp.dot(pe_ref[...], wp_ref[...], preferred_element_type=F32)
    o_ref[...] = xt_ref[...] + gate * emb


def ple_mm(x, g, wg, pe, wp, tm=512, tn=512):
    m, k = x.shape
    n = wg.shape[1]
    kp = pe.shape[1]
    tm, tn = _tile(m, tm), _tile(n, tn)
    return pl.pallas_call(
        _ple_kernel,
        grid=(m // tm, n // tn),
        in_specs=[pl.BlockSpec((tm, k), lambda i, j: (i, 0)),
                  pl.BlockSpec((tm, tn), lambda i, j: (i, j)),
                  pl.BlockSpec((1, k), lambda i, j: (0, 0)),
                  pl.BlockSpec((k, tn), lambda i, j: (0, j)),
                  pl.BlockSpec((tm, kp), lambda i, j: (i, 0)),
                  pl.BlockSpec((kp, tn), lambda i, j: (0, j))],
        out_specs=pl.BlockSpec((tm, tn), lambda i, j: (i, j)),
        out_shape=jax.ShapeDtypeStruct((m, n), F32),
        scratch_shapes=[pltpu.VMEM((tm, k), BF16)],
        compiler_params=_params(("parallel", "arbitrary")),
        name="ple_mm",
    )(x, x, g.reshape(1, k), wg, pe, wp)


def _rmsnorm_kernel(x_ref, g_ref, o_ref):
    o_ref[...] = _rms_rows(x_ref[...], g_ref[...])


def rmsnorm_rows(x, g, tm=512):
    m, k = x.shape
    tm = _tile(m, tm)
    return pl.pallas_call(
        _rmsnorm_kernel,
        grid=(m // tm,),
        in_specs=[pl.BlockSpec((tm, k), lambda i: (i, 0)), pl.BlockSpec((1, k), lambda i: (0, 0))],
        out_specs=pl.BlockSpec((tm, k), lambda i: (i, 0)),
        out_shape=jax.ShapeDtypeStruct((m, k), F32),
        compiler_params=_params(("parallel",)),
        name="final_rmsnorm",
    )(x, g.reshape(1, k))


_NT = (((1,), (1,)), ((), ()))
ATT_TQ = 256
ATT_TK = 256
NSA_TQ = 128
ROW_BLOCK = 128


def _iota(shape, dim):
    return lax.broadcasted_iota(jnp.int32, shape, dim)


def _lanes(x, width):
    return x if width == LANE else jnp.concatenate([x] * (width // LANE), axis=1)


def _split_bf16(x):
    hi = x.astype(BF16)
    lo = (x - hi.astype(F32)).astype(BF16)
    return hi, lo


def _diff_prompt_kernel(lam_ref, q_ref, k_ref, v_ref, sub_ref, o_ref, m_ref, l_ref, acc_ref, *, tq, tk, lam_init):
    i = pl.program_id(2)
    rows = 2 * tq
    n_kv = (i * tq + tq + tk - 1) // tk
    for c in range(2):
        m_ref[c] = jnp.full((rows, LANE), NEG_INF, F32)
        l_ref[c] = jnp.zeros((rows, LANE), F32)
        acc_ref[c] = jnp.zeros((rows, 2 * D_HEAD), F32)

        def step(j, diagonal, c=c):
            k0 = pl.multiple_of(j * tk, tk)
            ks = k_ref[0, pl.ds(k0, tk), c * D_HEAD:(c + 1) * D_HEAD]
            vs = v_ref[0, pl.ds(k0, tk), :]
            kpos = k0 + _iota((1, tk), 1)
            for b0 in range(0, rows, ROW_BLOCK):
                r, off = divmod(b0, tq)
                rs = slice(b0, b0 + ROW_BLOCK)
                qc = q_ref[0, off:off + ROW_BLOCK, (r * 2 + c) * D_HEAD:(r * 2 + c + 1) * D_HEAD]
                s = lax.dot_general(qc, ks, _NT, preferred_element_type=F32)
                if diagonal:
                    ok = kpos <= i * tq + off + _iota((ROW_BLOCK, 1), 0)
                    s = jnp.where(ok, s, NEG_INF)
                m_old = m_ref[c, rs]
                m_new = jnp.maximum(m_old, jnp.max(s, axis=-1, keepdims=True))
                alpha = jnp.exp(m_old - m_new)
                p = jnp.exp(s - _lanes(m_new, tk))
                if diagonal:
                    p = jnp.where(ok, p, 0.0)
                l_ref[c, rs] = alpha * l_ref[c, rs] + jnp.sum(p, axis=-1, keepdims=True)
                acc_ref[c, rs] = (_lanes(alpha, 2 * D_HEAD) * acc_ref[c, rs]
                                  + jnp.dot(p.astype(BF16), vs, preferred_element_type=F32))
                m_ref[c, rs] = m_new

        def body(j, carry, step=step):
            step(j, False)
            return carry

        lax.fori_loop(0, n_kv - 1, body, 0)
        step(n_kv - 1, True)
    lp = lam_ref[...]
    lam = (jnp.exp(jnp.sum(lp[0:1] * lp[1:2], axis=-1, keepdims=True))
           - jnp.exp(jnp.sum(lp[2:3] * lp[3:4], axis=-1, keepdims=True)) + lam_init)
    o = (acc_ref[0] / _lanes(l_ref[0], 2 * D_HEAD) - lam * (acc_ref[1] / _lanes(l_ref[1], 2 * D_HEAD)))
    o = _rms_rows(o, sub_ref[...]) * (1.0 - lam_init)
    for r in range(2):
        o_ref[0, :, r * 2 * D_HEAD:(r + 1) * 2 * D_HEAD] = o[r * tq:(r + 1) * tq].astype(o_ref.dtype)


def diff_prompt(q, k, v, diff_lambda, diff_subln, lam_init, tq=ATT_TQ, tk=ATT_TK):
    b, t, _ = q.shape
    tq, tk = _tile(t, tq), _tile(t, tk)
    assert tk % tq == 0
    rows = 2 * tq
    return pl.pallas_call(
        functools.partial(_diff_prompt_kernel, tq=tq, tk=tk, lam_init=lam_init),
        grid=(b, DA_KV, t // tq),
        in_specs=[pl.BlockSpec((4, D_HEAD), lambda bi, g, i: (0, 0)),
                  pl.BlockSpec((1, tq, 4 * D_HEAD), lambda bi, g, i: (bi, i, g)),
                  pl.BlockSpec((1, t, 2 * D_HEAD), lambda bi, g, i: (bi, 0, g)),
                  pl.BlockSpec((1, t, 2 * D_HEAD), lambda bi, g, i: (bi, 0, g)),
                  pl.BlockSpec((1, 2 * D_HEAD), lambda bi, g, i: (0, 0))],
        out_specs=pl.BlockSpec((1, tq, 4 * D_HEAD), lambda bi, g, i: (bi, i, g)),
        out_shape=jax.ShapeDtypeStruct((b, t, DA_QW), BF16),
        scratch_shapes=[pltpu.VMEM((2, rows, LANE), F32), pltpu.VMEM((2, rows, LANE), F32),
                        pltpu.VMEM((2, rows, 2 * D_HEAD), F32)],
        compiler_params=_params(("parallel", "parallel", "arbitrary")),
        name="diff_prompt",
    )(diff_lambda.astype(F32), q, k, v, diff_subln.reshape(1, -1).astype(F32))


def _sb_prompt_kernel(q_ref, k_ref, v_ref, u_ref, o_ref, c_ref, acc_ref, *, tq, tk):
    i = pl.program_id(2)
    rows = 2 * tq
    n_kv = (i * tq + tq + tk - 1) // tk
    c_ref[...] = jnp.zeros((rows, LANE), F32)
    acc_ref[...] = jnp.zeros((rows, D_HEAD), F32)

    def step(j, diagonal):
        k0 = pl.multiple_of(j * tk, tk)
        ks = k_ref[0, pl.ds(k0, tk), :]
        vs = v_ref[0, pl.ds(k0, tk), :]
        u = u_ref[...]
        qs = jnp.concatenate([q_ref[0, :, r * D_HEAD:(r + 1) * D_HEAD] for r in range(2)], axis=0)
        z = lax.dot_general(qs, ks, _NT, preferred_element_type=F32)
        sp = jnp.maximum(z, 0.0) + jnp.log(1.0 + jnp.exp(-jnp.abs(z)))
        lk = -sp
        if diagonal:
            strict = (k0 + _iota((1, tk), 1)) < i * tq + _iota((rows, 1), 0) % tq
            lk = jnp.where(strict, lk, 0.0)
        hi, lo = _split_bf16(lk)
        later = (jnp.dot(hi, u, preferred_element_type=F32) + jnp.dot(lo, u, preferred_element_type=F32)
                 + _lanes(c_ref[...], tk))
        a = jnp.exp(z - sp + later)
        if diagonal:
            a = jnp.where(strict, a, 0.0)
        acc_ref[...] += jnp.dot(a.astype(BF16), vs, preferred_element_type=F32)
        c_ref[...] += jnp.sum(lk, axis=-1, keepdims=True)

    def body(jj, carry):
        step(n_kv - 2 - jj, False)
        return carry

    step(n_kv - 1, True)
    lax.fori_loop(0, n_kv - 1, body, 0)
    for r in range(2):
        o_ref[0, :, r * D_HEAD:(r + 1) * D_HEAD] = acc_ref[r * tq:(r + 1) * tq, :].astype(o_ref.dtype)


def _suffix_matrix(tk):
    return (_iota((tk, tk), 0) > _iota((tk, tk), 1)).astype(BF16)


def sb_prompt(q, k, v, tq=ATT_TQ, tk=ATT_TK):
    b, t, _ = q.shape
    tq, tk = _tile(t, tq), _tile(t, tk)
    assert tk % tq == 0
    rows = 2 * tq
    return pl.pallas_call(
        functools.partial(_sb_prompt_kernel, tq=tq, tk=tk),
        grid=(b, SB_KV, t // tq),
        in_specs=[pl.BlockSpec((1, tq, 2 * D_HEAD), lambda bi, g, i: (bi, i, g)),
                  pl.BlockSpec((1, t, D_HEAD), lambda bi, g, i: (bi, 0, g)),
                  pl.BlockSpec((1, t, D_HEAD), lambda bi, g, i: (bi, 0, g)),
                  pl.BlockSpec((tk, tk), lambda bi, g, i: (0, 0))],
        out_specs=pl.BlockSpec((1, tq, 2 * D_HEAD), lambda bi, g, i: (bi, i, g)),
        out_shape=jax.ShapeDtypeStruct((b, t, SB_QW), BF16),
        scratch_shapes=[pltpu.VMEM((rows, LANE), F32), pltpu.VMEM((rows, D_HEAD), F32)],
        compiler_params=_params(("parallel", "parallel", "arbitrary")),
        name="sb_prompt",
    )(q, k, v, _suffix_matrix(tk))


def _nsa_prompt_kernel(q_ref, qr_ref, kc_ref, vc_ref, ks_ref, vs_ref, kw_ref, vw_ref, ng_ref, cov_ref, exp_ref,
                       o_ref, sel_ref, m_ref, l_ref, acc_ref, *, tq, tk, t, n_cmp, n_sel):
    i = pl.program_id(2)
    rn = NSA_HEADS // NSA_KV
    rows = rn * tq
    q0 = i * tq
    qpos_t = q0 + _iota((tq, 1), 0)
    qpos = q0 + _iota((rows, 1), 0) % tq
    lane = _iota((1, LANE), 1)
    stack = lambda ref: jnp.concatenate([ref[0, :, r * D_HEAD:(r + 1) * D_HEAD] for r in range(rn)], axis=0)

    s = lax.dot_general(stack(q_ref), kc_ref[0, 0], _NT, preferred_element_type=F32)
    c_ok = ((lane * CMP_STRIDE + (CMP_LEN - 1)) <= qpos) & (lane < n_cmp)
    s = jnp.where(c_ok, s, NEG_INF)
    e = jnp.where(c_ok, jnp.exp(s - jnp.max(s, axis=-1, keepdims=True)), 0.0)
    den = jnp.sum(e, axis=-1, keepdims=True)
    p = e / jnp.where(den > 0.0, den, 1.0)
    o_cmp = jnp.dot(p.astype(BF16), vc_ref[0, 0], preferred_element_type=F32)
    psum = p[0:tq]
    for r in range(1, rn):
        psum = psum + p[r * tq:(r + 1) * tq]
    hi, lo = _split_bf16(psum)
    cov = cov_ref[...]
    imp = jnp.dot(hi, cov, preferred_element_type=F32) + jnp.dot(lo, cov, preferred_element_type=F32)

    cur = qpos_t // SEL_BLOCK
    forced = (lane == 0) | (lane == cur) | (lane == cur - 1)
    causal = lane * SEL_BLOCK <= qpos_t
    score = jnp.where(forced, FORCED, jnp.where(causal, imp, NEG_INF))
    score = jnp.where(lane < n_sel, score, -3e38)
    cnt = jnp.zeros((tq, LANE), F32)
    for sp in range(n_sel):
        col = score[:, sp:sp + 1]
        tie = jnp.where(lane > sp, 1.0, 0.0)
        cnt = cnt + jnp.where(col > score, 1.0, jnp.where(col == score, tie, 0.0))
    selm = jnp.where((cnt < float(min(SEL_TOPK, n_sel))) & (lane < n_sel), 1.0, 0.0).astype(BF16)
    for jj in range(t // tk):
        sel_ref[jj] = jnp.dot(selm, exp_ref[:, jj * tk:(jj + 1) * tk], preferred_element_type=F32)

    m_ref[...] = jnp.full((rows, LANE), NEG_INF, F32)
    l_ref[...] = jnp.zeros((rows, LANE), F32)
    acc_ref[...] = jnp.zeros((rows, D_HEAD), F32)
    n_kv = (q0 + tq + tk - 1) // tk

    def body(j, carry):
        k0 = pl.multiple_of(j * tk, tk)
        ks = ks_ref[0, pl.ds(k0, tk), :]
        vs = vs_ref[0, pl.ds(k0, tk), :]
        ok = (sel_ref[j] > 0.5) & ((k0 + _iota((1, tk), 1)) <= qpos_t)
        for r in range(rn):
            rs = slice(r * tq, (r + 1) * tq)
            sj = lax.dot_general(qr_ref[0, :, r * D_HEAD:(r + 1) * D_HEAD], ks, _NT, preferred_element_type=F32)
            sj = jnp.where(ok, sj, NEG_INF)
            m_old = m_ref[rs]
            m_new = jnp.maximum(m_old, jnp.max(sj, axis=-1, keepdims=True))
            alpha = jnp.exp(m_old - m_new)
            pj = jnp.where(ok, jnp.exp(sj - _lanes(m_new, tk)), 0.0)
            l_ref[rs] = alpha * l_ref[rs] + jnp.sum(pj, axis=-1, keepdims=True)
            acc_ref[rs] = alpha * acc_ref[rs] + jnp.dot(pj.astype(BF16), vs, preferred_element_type=F32)
            m_ref[rs] = m_new
        return carry

    lax.fori_loop(0, n_kv, body, 0)

    wlen = WINDOW + tq
    w0 = pl.multiple_of(jnp.maximum(q0 - WINDOW, 0), tq)
    kwin = kw_ref[0, pl.ds(w0, wlen), :]
    vwin = vw_ref[0, pl.ds(w0, wlen), :]
    dist = qpos_t - (w0 + _iota((1, wlen), 1))
    w_ok = (dist >= 0) & (dist < WINDOW)
    gates = jax.nn.sigmoid(ng_ref[0])
    for r in range(rn):
        sw = lax.dot_general(qr_ref[0, :, r * D_HEAD:(r + 1) * D_HEAD], kwin, _NT, preferred_element_type=F32)
        sw = jnp.where(w_ok, sw, NEG_INF)
        ew = jnp.where(w_ok, jnp.exp(sw - jnp.max(sw, axis=-1, keepdims=True)), 0.0)
        o_win = jnp.dot(ew.astype(BF16), vwin, preferred_element_type=F32) / jnp.sum(ew, axis=-1, keepdims=True)
        rs = slice(r * tq, (r + 1) * tq)
        o_sel = acc_ref[rs, :] / l_ref[rs, :]
        out = (gates[:, 3 * r:3 * r + 1] * o_cmp[rs] + gates[:, 3 * r + 1:3 * r + 2] * o_sel
               + gates[:, 3 * r + 2:3 * r + 3] * o_win)
        o_ref[0, :, r * D_HEAD:(r + 1) * D_HEAD] = out.astype(o_ref.dtype)


def nsa_prompt(q, qr, kc, vc, ks, vs, kw, vw, ng, tq=NSA_TQ, tk=ATT_TK):
    b, t, _ = q.shape
    tq, tk = _tile(t, tq), _tile(t, tk)
    assert tk % tq == 0
    n_cmp = t // CMP_STRIDE - 1
    n_sel = t // SEL_BLOCK
    assert t % SEL_BLOCK == 0 and n_cmp <= LANE and n_sel <= LANE and t >= WINDOW + tq and WINDOW % tq == 0
    rn = NSA_HEADS // NSA_KV
    rows = rn * tq
    cmp_i = _iota((LANE, LANE), 0)
    sel_i = _iota((LANE, LANE), 1)
    cover = ((cmp_i * CMP_STRIDE <= sel_i * SEL_BLOCK + SEL_BLOCK - 1)
             & (cmp_i * CMP_STRIDE + CMP_LEN - 1 >= sel_i * SEL_BLOCK)
             & (cmp_i < n_cmp) & (sel_i < n_sel)).astype(BF16)
    expand = (_iota((LANE, t), 0) == _iota((LANE, t), 1) // SEL_BLOCK).astype(BF16)
    qspec = pl.BlockSpec((1, tq, rn * D_HEAD), lambda bi, g, i: (bi, i, g))
    cspec = pl.BlockSpec((1, 1, LANE, D_HEAD), lambda bi, g, i: (bi, g, 0, 0))
    kspec = pl.BlockSpec((1, t, D_HEAD), lambda bi, g, i: (bi, 0, g))
    return pl.pallas_call(
        functools.partial(_nsa_prompt_kernel, tq=tq, tk=tk, t=t, n_cmp=n_cmp, n_sel=n_sel),
        grid=(b, NSA_KV, t // tq),
        in_specs=[qspec, qspec, cspec, cspec, kspec, kspec, kspec, kspec,
                  pl.BlockSpec((1, tq, LANE), lambda bi, g, i: (bi, i, g)),
                  pl.BlockSpec((LANE, LANE), lambda bi, g, i: (0, 0)),
                  pl.BlockSpec((LANE, t), lambda bi, g, i: (0, 0))],
        out_specs=qspec,
        out_shape=jax.ShapeDtypeStruct((b, t, NSA_QW), BF16),
        scratch_shapes=[pltpu.VMEM((t // tk, tq, tk), F32), pltpu.VMEM((rows, LANE), F32), pltpu.VMEM((rows, LANE), F32),
                        pltpu.VMEM((rows, D_HEAD), F32)],
        compiler_params=_params(("parallel", "parallel", "arbitrary")),
        name="nsa_prompt",
    )(q, qr, kc, vc, ks, vs, kw, vw, ng, cover, expand)


QKV_W = DA_QW + 2 * DA_KW + NSA_QW + 6 * NSA_KW + SB_QW + 2 * SB_KW
POST_BF16 = (DA_QW, DA_KW, DA_KW, NSA_QW, NSA_QW, NSA_KW, NSA_KW, NSA_KW, NSA_KW, SB_QW, SB_KW, SB_KW)
POST_F32 = (2 * DA_KW, 2 * NSA_KW, 2 * NSA_KW, 2 * NSA_KW, 2 * SB_KW)


def _qkv_post_kernel(p_ref, c_ref, s1_ref, s2_ref, daq_ref, dak_ref, dav_ref, nq_ref, nqr_ref, ks_ref, vs_ref, kw_ref,
                     vw_ref, sq_ref, sk_ref, sv_ref, dst_ref, cst_ref, sst_ref, wst_ref, bst_ref):
    scale = D_HEAD ** -0.5
    cos, s1, s2 = c_ref[...], s1_ref[...], s2_ref[...]
    head = lambda col: p_ref[:, col:col + D_HEAD]

    def rot(x):
        return x * cos + pltpu.roll(x, ROPE_DIM // 2, 1) * s1 + pltpu.roll(x, D_HEAD - ROPE_DIM // 2, 1) * s2

    def put(ref, j, x):
        ref[:, j * D_HEAD:(j + 1) * D_HEAD] = x.astype(ref.dtype)

    col = 0
    for j in range(DA_QW // D_HEAD):
        put(daq_ref, j, rot(head(col + j * D_HEAD)) * scale)
    col += DA_QW
    for g in range(DA_KV):
        for c in range(2):
            k = rot(head(col + (2 * g + c) * D_HEAD))
            put(dak_ref, 2 * g + c, k)
            put(dst_ref, 4 * g + c, k)
            v = head(col + DA_KW + (2 * g + c) * D_HEAD)
            put(dav_ref, 2 * g + c, v)
            put(dst_ref, 4 * g + 2 + c, v)
    col += 2 * DA_KW
    for j in range(NSA_QW // D_HEAD):
        x = head(col + j * D_HEAD)
        put(nq_ref, j, x * scale)
        put(nqr_ref, j, rot(x) * scale)
    col += NSA_QW
    for g in range(NSA_KV):
        kc, vc, ks, vs, kw, vw = [head(col + (NSA_KV * i + g) * D_HEAD) for i in range(6)]
        ks, kw = rot(ks), rot(kw)
        put(cst_ref, 2 * g, kc)
        put(cst_ref, 2 * g + 1, vc)
        put(ks_ref, g, ks)
        put(vs_ref, g, vs)
        put(sst_ref, 2 * g, ks)
        put(sst_ref, 2 * g + 1, vs)
        put(kw_ref, g, kw)
        put(vw_ref, g, vw)
        put(wst_ref, 2 * g, kw)
        put(wst_ref, 2 * g + 1, vw)
    col += 6 * NSA_KW
    for j in range(SB_QW // D_HEAD):
        put(sq_ref, j, head(col + j * D_HEAD) * scale)
    col += SB_QW
    for g in range(SB_KV):
        k = head(col + g * D_HEAD)
        v = head(col + SB_KW + g * D_HEAD)
        put(sk_ref, g, k)
        put(sv_ref, g, v)
        put(bst_ref, 2 * g, k)
        put(bst_ref, 2 * g + 1, v)


def qkv_post(proj, pos, tm=256):
    m = proj.shape[0]
    r = pos.shape[0]
    tm = min(tm, r) if r % min(tm, r) == 0 else r
    if m % tm:
        tm = m
    assert r % tm == 0 and m % tm == 0, (m, r, tm)
    half = ROPE_DIM // 2
    inv = ROPE_THETA ** (-(jnp.arange(half, dtype=F32) * 2.0 / ROPE_DIM))
    ang = pos.astype(F32)[:, None] * inv[None, :]
    cos, sin = jnp.cos(ang), jnp.sin(ang)
    rest = D_HEAD - ROPE_DIM
    c_tab = jnp.concatenate([cos, cos, jnp.ones((r, rest), F32)], axis=1)
    s1_tab = jnp.concatenate([jnp.zeros((r, half), F32), sin, jnp.zeros((r, rest), F32)], axis=1)
    s2_tab = jnp.concatenate([-sin, jnp.zeros((r, half + rest), F32)], axis=1)
    period = r // tm
    t_spec = pl.BlockSpec((tm, D_HEAD), lambda i: (i % period, 0))
    outs = [(w, BF16) for w in POST_BF16] + [(w, F32) for w in POST_F32]
    return pl.pallas_call(
        _qkv_post_kernel,
        grid=(m // tm,),
        in_specs=[pl.BlockSpec((tm, QKV_W), lambda i: (i, 0)), t_spec, t_spec, t_spec],
        out_specs=[pl.BlockSpec((tm, w), lambda i: (i, 0)) for w, _ in outs],
        out_shape=[jax.ShapeDtypeStruct((m, w), dt) for w, dt in outs],
        compiler_params=_params(("parallel",)),
        name="qkv_post",
    )(proj, c_tab, s1_tab, s2_tab)


def _cmp_prompt_kernel(x_ref, w1_ref, pe_ref, w2_ref, kc_ref, vc_ref):
    x = x_ref[0]
    n = x.shape[0]
    row_w = NSA_KV * 2 * D_HEAD
    for kv, out_ref in ((0, kc_ref), (1, vc_ref)):
        pw = jnp.dot(pe_ref[kv], w1_ref[kv], preferred_element_type=F32)
        pe_w1 = pw[0:1, :D_HEAD] + pw[1:2, D_HEAD:]
        for g in range(NSA_KV):
            c4 = 2 * g + kv
            xc = jnp.concatenate([x[:, r * row_w + c4 * D_HEAD:r * row_w + (c4 + 1) * D_HEAD]
                                  for r in range(CMP_STRIDE)], axis=1).astype(BF16)
            lt = jnp.dot(xc, w1_ref[kv], preferred_element_type=F32)
            h = jax.nn.gelu(lt[:, :D_HEAD] + pltpu.roll(lt[:, D_HEAD:], n - 1, 0) + pe_w1, approximate=True)
            out_ref[0, g] = jnp.dot(h.astype(BF16), w2_ref[kv], preferred_element_type=F32).astype(out_ref.dtype)


def cmp_prompt(chunks, w1lt, pe2, w2):
    b, n, w = chunks.shape
    assert n == LANE
    full = lambda shape: pl.BlockSpec(shape, lambda bi: (0,) * len(shape))
    o_spec = pl.BlockSpec((1, NSA_KV, n, D_HEAD), lambda bi: (bi, 0, 0, 0))
    return pl.pallas_call(
        _cmp_prompt_kernel,
        grid=(b,),
        in_specs=[pl.BlockSpec((1, n, w), lambda bi: (bi, 0, 0)), full(w1lt.shape), full(pe2.shape), full(w2.shape)],
        out_specs=[o_spec, o_spec],
        out_shape=[jax.ShapeDtypeStruct((b, NSA_KV, n, D_HEAD), BF16)] * 2,
        compiler_params=_params(("parallel",)),
        name="cmp_prompt",
    )(chunks, w1lt, pe2, w2)


DEC_ROWS = 16
MAX_PAGES_PER_STEP = 8


def _pages_per_step(n_pages):
    return max(d for d in range(1, MAX_PAGES_PER_STEP + 1) if n_pages % d == 0)


def _page_specs(n_per, pool, n_pages=None):
    blk = (1,) + pool.shape[1:]
    zeros = (0,) * (pool.ndim - 1)

    def spec(p):
        if n_pages is None:
            return pl.BlockSpec(blk, lambda b, c, pt: (pt[b, c * n_per + p],) + zeros)
        return pl.BlockSpec(blk, lambda b, c, pt: (pt[b, n_pages - (c + 1) * n_per + p],) + zeros)
    return [spec(p) for p in range(n_per)]


def _page_rows(pages, g, kv, n_groups=None):
    if n_groups is None:
        tiles = [pg[0, :, g, kv, :] for pg in pages]
    else:
        tiles = [pg[0, pl.ds(2 * g + kv, PAGE_SIZE, stride=2 * n_groups), :] for pg in pages]
    return jnp.concatenate(tiles, axis=0).astype(BF16)


def _softmax_steps(m_ref, l_ref, acc_ref, items):
    masked = items[0][3] is not None
    assert all((ok is not None) == masked for _, _, _, ok in items)
    rows = items[0][1].shape[0]
    cat = lambda xs: jnp.concatenate(xs, axis=0) if len(xs) > 1 else xs[0]
    s = cat([s for _, s, _, _ in items])
    if masked:
        ok = cat([ok for _, _, _, ok in items])
        s = jnp.where(ok, s, NEG_INF)
    m_old = cat([m_ref[g] for g, _, _, _ in items])
    l_old = cat([l_ref[g] for g, _, _, _ in items])
    m_new = jnp.maximum(m_old, jnp.max(s, axis=-1, keepdims=True))
    alpha = jnp.exp(m_old - m_new)
    p = jnp.exp(s - m_new)
    if masked:
        p = jnp.where(ok, p, 0.0)
    l_new = alpha * l_old + jnp.sum(p, axis=-1, keepdims=True)
    p = p.astype(BF16)
    for i, (g, _, v, _) in enumerate(items):
        sl = slice(i * rows, (i + 1) * rows)
        acc_ref[g] = alpha[sl] * acc_ref[g] + jnp.dot(p[sl], v, preferred_element_type=F32)
        m_ref[g] = m_new[sl]
        l_ref[g] = l_new[sl]


def _softmax_step(m_ref, l_ref, acc_ref, g, s, v, ok):
    _softmax_steps(m_ref, l_ref, acc_ref, [(g, s, v, ok)])


def _new_causal(t_new):
    return _iota((DEC_ROWS, DEC_ROWS), 1) <= _iota((DEC_ROWS, DEC_ROWS), 0) % t_new


def _diff_dec_kernel(pt_ref, lam_ref, q_ref, kn_ref, vn_ref, sub_ref, *rest, n_per, lam_init, t_new):
    pages, o_ref, (m_ref, l_ref, acc_ref) = rest[:n_per], rest[n_per], rest[n_per + 1:]
    c = pl.program_id(1)

    @pl.when(c == 0)
    def _():
        m_ref[...] = jnp.full(m_ref.shape, NEG_INF, F32)
        l_ref[...] = jnp.zeros(l_ref.shape, F32)
        acc_ref[...] = jnp.zeros(acc_ref.shape, F32)

    _softmax_steps(m_ref, l_ref, acc_ref, [
        (g, lax.dot_general(q_ref[0, g], _page_rows(pages, g, 0), _NT, preferred_element_type=F32),
         _page_rows(pages, g, 1), None) for g in range(DA_KV)])

    @pl.when(c == pl.num_programs(1) - 1)
    def _():
        lp = lam_ref[...]
        lam = (jnp.exp(jnp.sum(lp[0:1] * lp[1:2], axis=-1, keepdims=True))
               - jnp.exp(jnp.sum(lp[2:3] * lp[3:4], axis=-1, keepdims=True)) + lam_init)
        for g in range(DA_KV):
            s = lax.dot_general(q_ref[0, g], kn_ref[0, g], _NT, preferred_element_type=F32)
            _softmax_step(m_ref, l_ref, acc_ref, g, s, vn_ref[0, g], _new_causal(t_new))
            a = acc_ref[g] / l_ref[g]
            half = DEC_ROWS // 2
            o = a[0:half] - lam * a[half:DEC_ROWS]
            o_ref[0, g] = _rms_rows(o, sub_ref[...]) * (1.0 - lam_init)


def diff_dec(page_table, pool, q, kn, vn, diff_lambda, diff_subln, lam_init, t_new):
    b, n_pages = page_table.shape
    n_per = _pages_per_step(n_pages)
    blk = lambda w: pl.BlockSpec((1, DA_KV, DEC_ROWS, w), lambda bi, c, pt: (bi, 0, 0, 0))
    grid_spec = pltpu.PrefetchScalarGridSpec(
        num_scalar_prefetch=1, grid=(b, n_pages // n_per),
        in_specs=[pl.BlockSpec((4, D_HEAD), lambda bi, c, pt: (0, 0)), blk(2 * D_HEAD), blk(2 * D_HEAD), blk(2 * D_HEAD),
                  pl.BlockSpec((1, 2 * D_HEAD), lambda bi, c, pt: (0, 0))] + _page_specs(n_per, pool),
        out_specs=pl.BlockSpec((1, DA_KV, DEC_ROWS // 2, 2 * D_HEAD), lambda bi, c, pt: (bi, 0, 0, 0)),
        scratch_shapes=[pltpu.VMEM((DA_KV, DEC_ROWS, 1), F32), pltpu.VMEM((DA_KV, DEC_ROWS, 1), F32),
                        pltpu.VMEM((DA_KV, DEC_ROWS, 2 * D_HEAD), F32)])
    return pl.pallas_call(
        functools.partial(_diff_dec_kernel, n_per=n_per, lam_init=lam_init, t_new=t_new),
        grid_spec=grid_spec,
        out_shape=jax.ShapeDtypeStruct((b, DA_KV, DEC_ROWS // 2, 2 * D_HEAD), F32),
        compiler_params=_params(("parallel", "arbitrary")),
        name="diff_dec",
    )(page_table, diff_lambda.astype(F32), q, kn, vn, diff_subln.reshape(1, -1).astype(F32), *([pool] * n_per))


def _sb_dec_kernel(pt_ref, q_ref, kn_ref, vn_ref, *rest, n_per, t_new):
    pages, o_ref, (c_ref, acc_ref) = rest[:n_per], rest[n_per], rest[n_per + 1:]
    c = pl.program_id(1)

    def tiles(kv_of, strict):
        kvs = [kv_of(g) for g in range(SB_KV)]
        n = kvs[0][0].shape[0]
        lane = _iota((1, n), 1)
        z = jnp.concatenate([lax.dot_general(q_ref[0, g], k, _NT, preferred_element_type=F32)
                             for g, (k, _) in enumerate(kvs)], axis=0)
        sp = jnp.maximum(z, 0.0) + jnp.log(1.0 + jnp.exp(-jnp.abs(z)))
        lk = -sp if strict is None else jnp.where(strict(lane), -sp, 0.0)
        y = lk
        d = 1
        while d < n:
            y = y + jnp.where(lane < n - d, pltpu.roll(y, n - d, 1), 0.0)
            d *= 2
        a = jnp.exp(z - sp + (y - lk) + c_ref[...])
        if strict is not None:
            a = jnp.where(strict(lane), a, 0.0)
        a = a.astype(BF16)
        c_ref[...] += y[:, 0:1]
        for g, (_, v) in enumerate(kvs):
            acc_ref[g] += jnp.dot(a[g * DEC_ROWS:(g + 1) * DEC_ROWS], v, preferred_element_type=F32)

    @pl.when(c == 0)
    def _():
        c_ref[...] = jnp.zeros(c_ref.shape, F32)
        acc_ref[...] = jnp.zeros(acc_ref.shape, F32)
        t_row = _iota((SB_KV * DEC_ROWS, 1), 0) % t_new
        tiles(lambda g: (kn_ref[0, g], vn_ref[0, g]), lambda lane: lane < t_row)

    tiles(lambda g: (_page_rows(pages, g, 0, SB_KV), _page_rows(pages, g, 1, SB_KV)), None)

    @pl.when(c == pl.num_programs(1) - 1)
    def _():
        o_ref[0] = acc_ref[...]


def sb_dec(page_table, pool, q, kn, vn, t_new):
    b, n_pages = page_table.shape
    n_per = _pages_per_step(n_pages)
    blk = lambda r: pl.BlockSpec((1, SB_KV, r, D_HEAD), lambda bi, c, pt: (bi, 0, 0, 0))
    grid_spec = pltpu.PrefetchScalarGridSpec(
        num_scalar_prefetch=1, grid=(b, n_pages // n_per),
        in_specs=[blk(DEC_ROWS), blk(PAGE_SIZE), blk(PAGE_SIZE)]
        + _page_specs(n_per, pool, n_pages=n_pages),
        out_specs=blk(DEC_ROWS),
        scratch_shapes=[pltpu.VMEM((SB_KV * DEC_ROWS, 1), F32), pltpu.VMEM((SB_KV, DEC_ROWS, D_HEAD), F32)])
    return pl.pallas_call(
        functools.partial(_sb_dec_kernel, n_per=n_per, t_new=t_new),
        grid_spec=grid_spec,
        out_shape=jax.ShapeDtypeStruct((b, SB_KV, DEC_ROWS, D_HEAD), F32),
        compiler_params=_params(("parallel", "arbitrary")),
        name="sb_dec",
    )(page_table, q, kn, vn, *([pool] * n_per))


def _nsa_cmp_dec_kernel(pt_ref, q_ref, xn_ref, w1_ref, pe_ref, w2_ref, cov_ref, exp_ref, *rest,
                        n_per, t_new, p_len, n_cmp, n_sel):
    pages = rest[:n_per]
    ocmp_ref, sel_ref = rest[n_per], rest[n_per + 1]
    lead_ref, trail_ref = rest[n_per + 2:]
    c = pl.program_id(1)
    row_w = NSA_KV * 2 * D_HEAD
    page_ch = PAGE_SIZE // CMP_STRIDE
    n_ch = page_ch * n_per
    cb = cov_ref.shape[0]
    sl = cov_ref.shape[1]

    @pl.when(c == 0)
    def _():
        lead_ref[...] = jnp.zeros(lead_ref.shape, F32)
        trail_ref[...] = jnp.zeros(trail_ref.shape, F32)

    def project(chunk_rows, base):
        for c4 in range(2 * NSA_KV):
            xc = jnp.concatenate([chunk_rows(c4 // 2, c4 % 2, r) for r in range(CMP_STRIDE)], axis=1).astype(BF16)
            lt = jnp.dot(xc, w1_ref[c4 % 2], preferred_element_type=F32)
            lead_ref[c4, pl.ds(base, xc.shape[0]), :] = lt[:, :D_HEAD]
            trail_ref[c4, pl.ds(base, xc.shape[0]), :] = lt[:, D_HEAD:]

    project(lambda g, kv, r: jnp.concatenate(
        [pg[0, pl.ds(2 * NSA_KV * r + 2 * g + kv, page_ch, stride=2 * NSA_KV * CMP_STRIDE), :] for pg in pages], axis=0),
        pl.multiple_of(c * n_ch, n_ch))

    @pl.when(c == pl.num_programs(1) - 1)
    def _():
        project(lambda g, kv, r: xn_ref[0, :, r * row_w + (2 * g + kv) * D_HEAD:r * row_w + (2 * g + kv + 1) * D_HEAD],
                p_len // CMP_STRIDE)
        qpos = p_len + _iota((DEC_ROWS, 1), 0) % t_new
        n_idx = _iota((1, cb), 1)
        c_ok = ((n_idx * CMP_STRIDE + (CMP_LEN - 1)) <= qpos) & (n_idx < n_cmp)
        rr = (_iota((DEC_ROWS, DEC_ROWS), 0) % t_new == _iota((DEC_ROWS, DEC_ROWS), 1) % t_new)
        rr = jnp.where(rr, 1.0, 0.0).astype(BF16)
        lane = _iota((1, sl), 1)
        cur = qpos // SEL_BLOCK
        forced = (lane == 0) | (lane == cur) | (lane == cur - 1)
        causal = lane * SEL_BLOCK <= qpos
        for g in range(NSA_KV):
            blocks = []
            for kv in range(2):
                c4 = g * 2 + kv
                pw = jnp.dot(pe_ref[kv], w1_ref[kv], preferred_element_type=F32)
                pe_w1 = pw[0:1, :D_HEAD] + pw[1:2, D_HEAD:]
                h = jax.nn.gelu(lead_ref[c4, 0:cb, :] + trail_ref[c4, 1:cb + 1, :] + pe_w1, approximate=True)
                blocks.append(jnp.dot(h.astype(BF16), w2_ref[kv], preferred_element_type=F32).astype(BF16))
            s = lax.dot_general(q_ref[0, g], blocks[0], _NT, preferred_element_type=F32)
            s = jnp.where(c_ok, s, NEG_INF)
            e = jnp.where(c_ok, jnp.exp(s - jnp.max(s, axis=-1, keepdims=True)), 0.0)
            den = jnp.sum(e, axis=-1, keepdims=True)
            p = e / jnp.where(den > 0.0, den, 1.0)
            ocmp_ref[0, g] = jnp.dot(p.astype(BF16), blocks[1], preferred_element_type=F32)
            hi, lo = _split_bf16(p)
            psum = jnp.dot(rr, hi, preferred_element_type=F32) + jnp.dot(rr, lo, preferred_element_type=F32)
            hi, lo = _split_bf16(psum)
            cov = cov_ref[...]
            imp = jnp.dot(hi, cov, preferred_element_type=F32) + jnp.dot(lo, cov, preferred_element_type=F32)
            score = jnp.where(forced, FORCED, jnp.where(causal, imp, NEG_INF))
            score = jnp.where(lane < n_sel, score, -3e38)
            cnt = jnp.zeros((DEC_ROWS, sl), F32)
            for sp in range(n_sel):
                col = score[:, sp:sp + 1]
                tie = jnp.where(lane > sp, 1.0, 0.0)
                cnt = cnt + jnp.where(col > score, 1.0, jnp.where(col == score, tie, 0.0))
            selm = jnp.where((cnt < float(min(SEL_TOPK, n_sel))) & (lane < n_sel), 1.0, 0.0).astype(BF16)
            sel_ref[0, g] = jnp.dot(selm, exp_ref[...], preferred_element_type=F32)


def nsa_cmp_dec(page_table, pool, q, xnew, w1lt, pe2, w2, t_new):
    b, n_pages = page_table.shape
    n_per = _pages_per_step(n_pages)
    p_len = n_pages * PAGE_SIZE
    total = p_len + SEL_BLOCK
    n_cmp = total // CMP_STRIDE - 1
    n_sel = total // SEL_BLOCK
    cb = ((p_len // CMP_STRIDE + 8 + LANE - 1) // LANE) * LANE
    sl = ((n_sel + LANE - 1) // LANE) * LANE
    assert t_new <= CMP_STRIDE and DEC_ROWS % t_new == 0
    cmp_i = _iota((cb, sl), 0)
    sel_i = _iota((cb, sl), 1)
    cover = ((cmp_i * CMP_STRIDE <= sel_i * SEL_BLOCK + SEL_BLOCK - 1)
             & (cmp_i * CMP_STRIDE + CMP_LEN - 1 >= sel_i * SEL_BLOCK)
             & (cmp_i < n_cmp) & (sel_i < n_sel)).astype(BF16)
    klen = p_len + LANE
    expand = (_iota((sl, klen), 0) == _iota((sl, klen), 1) // SEL_BLOCK).astype(BF16)
    full = lambda shape: pl.BlockSpec(shape, lambda bi, c, pt: (0,) * len(shape))
    qspec = pl.BlockSpec((1, NSA_KV, DEC_ROWS, D_HEAD), lambda bi, c, pt: (bi, 0, 0, 0))
    grid_spec = pltpu.PrefetchScalarGridSpec(
        num_scalar_prefetch=1, grid=(b, n_pages // n_per),
        in_specs=[qspec, pl.BlockSpec((1,) + xnew.shape[1:], lambda bi, c, pt: (bi, 0, 0)),
                  full(w1lt.shape), full(pe2.shape), full(w2.shape), full(cover.shape), full(expand.shape)]
        + _page_specs(n_per, pool),
        out_specs=[qspec, pl.BlockSpec((1, NSA_KV, DEC_ROWS, klen), lambda bi, c, pt: (bi, 0, 0, 0))],
        scratch_shapes=[pltpu.VMEM((2 * NSA_KV, cb + 8, D_HEAD), F32), pltpu.VMEM((2 * NSA_KV, cb + 8, D_HEAD), F32)])
    return pl.pallas_call(
        functools.partial(_nsa_cmp_dec_kernel, n_per=n_per, t_new=t_new, p_len=p_len, n_cmp=n_cmp, n_sel=n_sel),
        grid_spec=grid_spec,
        out_shape=[jax.ShapeDtypeStruct((b, NSA_KV, DEC_ROWS, D_HEAD), F32),
                   jax.ShapeDtypeStruct((b, NSA_KV, DEC_ROWS, klen), F32)],
        compiler_params=_params(("parallel", "arbitrary")),
        name="nsa_cmp_dec",
    )(page_table, q, xnew, w1lt, pe2, w2, cover, expand, *([pool] * n_per))


def _nsa_sel_dec_kernel(pt_ref, q_ref, selp_ref, seln_ref, ksn_ref, vsn_ref, ocmp_ref, ng_ref, win_ref, kwn_ref,
                        vwn_ref, *rest, n_per, t_new):
    pages, o_ref, (m_ref, l_ref, acc_ref) = rest[:n_per], rest[n_per], rest[n_per + 1:]
    c = pl.program_id(1)

    @pl.when(c == 0)
    def _():
        m_ref[...] = jnp.full(m_ref.shape, NEG_INF, F32)
        l_ref[...] = jnp.zeros(l_ref.shape, F32)
        acc_ref[...] = jnp.zeros(acc_ref.shape, F32)

    _softmax_steps(m_ref, l_ref, acc_ref, [
        (g, lax.dot_general(q_ref[0, g], _page_rows(pages, g, 0, NSA_KV), _NT, preferred_element_type=F32),
         _page_rows(pages, g, 1, NSA_KV), selp_ref[0, g] > 0.5) for g in range(NSA_KV)])

    @pl.when(c == pl.num_programs(1) - 1)
    def _():
        causal = _new_causal(t_new)
        wb = win_ref.shape[1] // (2 * NSA_KV)
        t_row = _iota((DEC_ROWS, 1), 0) % t_new
        w_ok = _iota((1, wb), 1) > t_row
        for g in range(NSA_KV):
            q = q_ref[0, g]
            s = lax.dot_general(q, ksn_ref[0, g], _NT, preferred_element_type=F32)
            _softmax_step(m_ref, l_ref, acc_ref, g, s, vsn_ref[0, g], (seln_ref[0, g, :, 0:DEC_ROWS] > 0.5) & causal)
            o_sel = acc_ref[g] / l_ref[g]
            kw = win_ref[0, pl.ds(2 * g, wb, stride=2 * NSA_KV), :].astype(BF16)
            vw = win_ref[0, pl.ds(2 * g + 1, wb, stride=2 * NSA_KV), :].astype(BF16)
            sw = jnp.where(w_ok, lax.dot_general(q, kw, _NT, preferred_element_type=F32), NEG_INF)
            sn = jnp.where(causal, lax.dot_general(q, kwn_ref[0, g], _NT, preferred_element_type=F32), NEG_INF)
            mx = jnp.maximum(jnp.max(sw, axis=-1, keepdims=True), jnp.max(sn, axis=-1, keepdims=True))
            ew = jnp.where(w_ok, jnp.exp(sw - mx), 0.0)
            en = jnp.where(causal, jnp.exp(sn - mx), 0.0)
            den = jnp.sum(ew, axis=-1, keepdims=True) + jnp.sum(en, axis=-1, keepdims=True)
            o_win = (jnp.dot(ew.astype(BF16), vw, preferred_element_type=F32)
                     + jnp.dot(en.astype(BF16), vwn_ref[0, g], preferred_element_type=F32)) / den
            gates = jax.nn.sigmoid(ng_ref[0, g])
            o_ref[0, g] = gates[:, 0:1] * ocmp_ref[0, g] + gates[:, 1:2] * o_sel + gates[:, 2:3] * o_win


def nsa_sel_dec(page_table, pool, q, selexp, ksn, vsn, o_cmp, ng, win, kwn, vwn, t_new):
    b, n_pages = page_table.shape
    n_per = _pages_per_step(n_pages)
    assert win.shape[1] == WINDOW * 2 * NSA_KV
    blk = pl.BlockSpec((1, NSA_KV, DEC_ROWS, D_HEAD), lambda bi, c, pt: (bi, 0, 0, 0))
    grid_spec = pltpu.PrefetchScalarGridSpec(
        num_scalar_prefetch=1, grid=(b, n_pages // n_per),
        in_specs=[blk, pl.BlockSpec((1, NSA_KV, DEC_ROWS, n_per * PAGE_SIZE), lambda bi, c, pt: (bi, 0, 0, c)),
                  pl.BlockSpec((1, NSA_KV, DEC_ROWS, LANE), lambda bi, c, pt: (bi, 0, 0, n_pages)),
                  blk, blk, blk, blk,
                  pl.BlockSpec((1,) + win.shape[1:], lambda bi, c, pt: (bi, 0, 0)), blk, blk]
        + _page_specs(n_per, pool),
        out_specs=blk,
        scratch_shapes=[pltpu.VMEM((NSA_KV, DEC_ROWS, 1), F32), pltpu.VMEM((NSA_KV, DEC_ROWS, 1), F32),
                        pltpu.VMEM((NSA_KV, DEC_ROWS, D_HEAD), F32)])
    return pl.pallas_call(
        functools.partial(_nsa_sel_dec_kernel, n_per=n_per, t_new=t_new),
        grid_spec=grid_spec,
        out_shape=jax.ShapeDtypeStruct((b, NSA_KV, DEC_ROWS, D_HEAD), F32),
        compiler_params=_params(("parallel", "arbitrary")),
        name="nsa_sel_dec",
    )(page_table, q, selexp, selexp, ksn, vsn, o_cmp, ng, win, kwn, vwn, *([pool] * n_per))


def _rmsnorm(x, g):
    xf = x.astype(F32)
    y = xf * lax.rsqrt(jnp.mean(xf * xf, axis=-1, keepdims=True) + EPS)
    return (y * g.astype(F32)).astype(x.dtype)


def _masked_softmax(s, mask):
    return jax.nn.softmax(jnp.where(mask, s, NEG_INF), axis=-1)


def _rope(x, pos):
    half = ROPE_DIM // 2
    inv = ROPE_THETA ** (-(jnp.arange(half, dtype=F32) * 2.0 / ROPE_DIM))
    ang = pos.astype(F32)[:, None] * inv[None, :]
    ang = ang.reshape((1, ang.shape[0]) + (1,) * (x.ndim - 3) + (half,))
    cos, sin = jnp.cos(ang), jnp.sin(ang)
    x1, x2 = x[..., :half], x[..., half:ROPE_DIM]
    return jnp.concatenate([x1 * cos - x2 * sin, x2 * cos + x1 * sin, x[..., ROPE_DIM:]], axis=-1)


def _over_query_blocks(fn, block, qpos, *qs):
    t = qpos.shape[0]
    if t <= block or t % block:
        return fn(qpos, *qs)
    n = t // block
    split = lambda a: jnp.moveaxis(a.reshape((a.shape[0], n, block) + a.shape[2:]), 1, 0)
    out = lax.map(lambda a: fn(*a), (qpos.reshape(n, block),) + tuple(split(q) for q in qs))
    out = jnp.moveaxis(out, 0, 1)
    return out.reshape((out.shape[0], t) + out.shape[3:])


def _pad_rows(a, multiple):
    extra = (-a.shape[1]) % multiple
    if extra == 0:
        return a
    return jnp.pad(a, ((0, 0), (0, extra)) + ((0, 0),) * (a.ndim - 2))


def _diff_attention(kpos, k, v, lam, sub_g, lam_init):
    scale = D_HEAD ** -0.5
    def fn(qpos, q):
        s = jnp.einsum('btgrcd,bsgcd->bgrcts', q, k, preferred_element_type=F32) * scale
        p = _masked_softmax(s, kpos[None, :] <= qpos[:, None])
        a = p[:, :, :, 0] - lam * p[:, :, :, 1]
        o = jnp.einsum('bgrts,bsgd->btgrd', a.astype(v.dtype), v)
        return _rmsnorm(o, sub_g) * (1.0 - lam_init)
    return fn


def _stick_breaking(kpos, k, v):
    scale = D_HEAD ** -0.5
    def fn(qpos, q):
        z = jnp.einsum('btgrd,bsgd->bgrts', q, k, preferred_element_type=F32) * scale
        strict = kpos[None, :] < qpos[:, None]
        log_keep = jnp.where(strict, jax.nn.log_sigmoid(-z), 0.0)
        later = lax.cumsum(log_keep, axis=z.ndim - 1, reverse=True) - log_keep
        a = jnp.where(strict, jnp.exp(jax.nn.log_sigmoid(z) + later), 0.0)
        return jnp.einsum('bgrts,bsgd->btgrd', a.astype(v.dtype), v)
    return fn


def _nsa_compress(k, w1, w2, pe):
    b, l, g, d = k.shape
    chunks = k.reshape(b, l // CMP_STRIDE, CMP_STRIDE, g, d).transpose(0, 1, 3, 2, 4)
    chunks = chunks.reshape(b, l // CMP_STRIDE, g, CMP_STRIDE * d)
    half = CMP_STRIDE * d
    lead = chunks @ w1[:half]
    trail = chunks @ w1[half:]
    h = jax.nn.gelu(lead[:, :-1] + trail[:, 1:] + pe.reshape(-1) @ w1, approximate=True)
    return h @ w2


def _nsa_cmp_sel(kc, vc, ks, vs):
    b, n_keys, g, d = ks.shape
    n_cmp = kc.shape[1]
    n_sel = n_keys // SEL_BLOCK
    top = min(SEL_TOPK, n_sel)
    scale = D_HEAD ** -0.5
    cmp_start = jnp.arange(n_cmp) * CMP_STRIDE
    cmp_end = cmp_start + CMP_LEN - 1
    blk = jnp.arange(n_sel)
    cover = ((cmp_start[:, None] <= blk[None, :] * SEL_BLOCK + SEL_BLOCK - 1)
             & (cmp_end[:, None] >= blk[None, :] * SEL_BLOCK)).astype(F32)
    ksb = ks.reshape(b, n_sel, SEL_BLOCK, g, d).transpose(0, 3, 1, 2, 4)
    vsb = vs.reshape(b, n_sel, SEL_BLOCK, g, d).transpose(0, 3, 1, 2, 4)
    bi = jnp.arange(b)[:, None, None, None]
    gi = jnp.arange(g)[None, None, :, None]
    offs = jnp.arange(SEL_BLOCK)

    def fn(qpos, q, q_rot, g_cmp, g_sel):
        tq = qpos.shape[0]
        c_ok = (cmp_end[None, :] <= qpos[:, None])[None, :, None, None, :]
        s = jnp.einsum('btgrd,bngd->btgrn', q, kc, preferred_element_type=F32) * scale
        p = jnp.where(c_ok, _masked_softmax(s, c_ok), 0.0)
        o_cmp = jnp.einsum('btgrn,bngd->btgrd', p.astype(vc.dtype), vc)
        imp = jnp.einsum('btgn,ns->btgs', p.sum(axis=3), cover)
        cur = qpos // SEL_BLOCK
        forced = (blk[None, :] == 0) | (blk[None, :] == cur[:, None]) | (blk[None, :] == cur[:, None] - 1)
        causal = blk[None, :] * SEL_BLOCK <= qpos[:, None]
        score = jnp.where(forced[None, :, None, :], FORCED,
                          jnp.where(causal[None, :, None, :], imp, NEG_INF))
        idx = lax.top_k(score, top)[1]
        gk = ksb[bi, gi, idx].reshape(b, tq, g, top * SEL_BLOCK, d)
        gv = vsb[bi, gi, idx].reshape(b, tq, g, top * SEL_BLOCK, d)
        kpos = (idx[..., None] * SEL_BLOCK + offs).reshape(b, tq, g, 1, top * SEL_BLOCK)
        ss = jnp.einsum('btgrd,btgmd->btgrm', q_rot, gk, preferred_element_type=F32) * scale
        ps = _masked_softmax(ss, kpos <= qpos[None, :, None, None, None])
        o_sel = jnp.einsum('btgrm,btgmd->btgrd', ps.astype(gv.dtype), gv)
        return g_cmp[..., None] * o_cmp + g_sel[..., None] * o_sel
    return fn


def _banded_attn(q, kv, qpos, kpos):
    s = jnp.einsum('bnqgrd,bnkgd->bngrqk', q, kv[..., 0, :], preferred_element_type=F32) * D_HEAD ** -0.5
    dist = qpos[:, :, None] - kpos[:, None, :]
    ok = (dist >= 0) & (dist < WINDOW) & (kpos[:, None, :] >= 0)
    p = _masked_softmax(s, ok[None, :, None, None])
    return jnp.einsum('bngrqk,bnkgd->bnqgrd', p.astype(kv.dtype), kv[..., 1, :])


def _window_prompt(q, kv):
    b, t = q.shape[:2]
    nb = t // Q_BLOCK
    kvp = jnp.pad(kv, ((0, 0), (WINDOW, 0), (0, 0), (0, 0), (0, 0)))
    idx = jnp.arange(nb)[:, None] * Q_BLOCK + jnp.arange(WINDOW + Q_BLOCK)[None, :]
    qpos = jnp.arange(nb)[:, None] * Q_BLOCK + jnp.arange(Q_BLOCK)[None, :]
    o = _banded_attn(q.reshape((b, nb, Q_BLOCK) + q.shape[2:]), kvp[:, idx], qpos, idx - WINDOW)
    return o.reshape(q.shape)


def _window_sample(q, pos, kv_all, start):
    kpos = start + jnp.arange(kv_all.shape[1])
    return _banded_attn(q[:, None], kv_all[:, None], pos[None], kpos[None])[:, 0]


def _prep_weights(w_in, w_br_a, w_br_b, w_br_c, w_out, w_ff_gate, w_ff_up, w_ff_down, w_ple, w_ple_gate):
    w_main = jnp.concatenate([w_in[:, :NG_AT], w_in[:, NG_AT + NG_W:]], axis=1).astype(BF16)
    w_ng = jnp.pad(w_in[:, NG_AT:NG_AT + NG_W], ((0, 0), (0, LANE - NG_W))).astype(BF16)
    c = lambda a: a.astype(BF16)
    fpad = D_FFP - D_FF
    w_ff_gate = jnp.pad(w_ff_gate, ((0, 0), (0, fpad)))
    w_ff_up = jnp.pad(w_ff_up, ((0, 0), (0, fpad)))
    w_ff_down = jnp.pad(w_ff_down, ((0, fpad), (0, 0)))
    return (w_main, w_ng, c(w_br_a), c(w_br_b), c(w_br_c), c(w_out), c(w_ff_gate), c(w_ff_up),
            c(w_ff_down), c(w_ple), c(w_ple_gate))


def _layer(x, pe, past, li, ln1, wts, diff_lambda, diff_subln, cmp_w1, cmp_w2, cmp_pe,
           ln2, ff_conv_w, ff_conv_b, ln3):
    (w_main, w_ng, w_br_a, w_br_b, w_br_c, w_out, w_ff_gate, w_ff_up, w_ff_down, w_ple, w_ple_gate) = wts
    b, t, _ = x.shape
    m = b * t
    p_len = 0 if past is None else past[0].shape[1] * PAGE_SIZE
    pos = p_len + jnp.arange(t, dtype=jnp.int32)
    x2 = x.reshape(m, D_MODEL)
    if past is not None:
        page_table, pool_diff, pool_cmp, pool_sel, pool_sb, win_buf, conv_buf = past
        n_pool = pool_diff.shape[0]
        assert DEC_ROWS == t * NSA_HEADS // NSA_KV == 2 * t * DA_HEADS // DA_KV and DEC_ROWS >= t * SB_HEADS // SB_KV

    def dec_rows(a, rows=DEC_ROWS):
        g, d = a.shape[2], a.shape[-1]
        a = jnp.moveaxis(a, 1, -2).reshape(b, g, -1, d)
        return jnp.pad(a, ((0, 0), (0, 0), (0, rows - a.shape[2]), (0, 0))).astype(BF16)

    def dec_out(o, rn):
        g, d = o.shape[1], o.shape[-1]
        o = o[:, :, :rn * t].reshape(b, g, rn, t, d)
        return jnp.transpose(o, (0, 3, 1, 2, 4)).reshape(b, t, g * rn * d)

    proj = norm_mm(x2, ln1, w_main)
    ng = norm_mm(x2, ln1, w_ng)[:, :NG_W].reshape(b, t, NG_W)
    pos_rows = pos if t % 8 == 0 else jnp.tile(pos, b)
    (da_q, da_k, da_v, nq, nq_rot, ks, vs, kw, vw, sq, sk, sv,
     da_new, cmp_new, sel_new, win_new, sb_new) = qkv_post(proj, pos_rows)
    seq = lambda a: a.reshape(b, t, -1)
    heads = lambda a, g: a.reshape(b, t, g, -1, D_HEAD)
    da_new = da_new.reshape(b, t, DA_KV, 2, 2 * D_HEAD)
    cmp_new, sel_new, win_new = [a.reshape(b, t, NSA_KV, 2, D_HEAD) for a in (cmp_new, sel_new, win_new)]
    sb_new = sb_new.reshape(b, t, SB_KV, 2, D_HEAD)
    lam_init = 0.8 - 0.6 * math.exp(-0.3 * li)
    rn = NSA_HEADS // NSA_KV
    chunk_w = CMP_STRIDE * NSA_KV * 2 * D_HEAD
    half = CMP_STRIDE * D_HEAD
    w1lt = jnp.concatenate([cmp_w1[:, :half], cmp_w1[:, half:]], axis=2).astype(BF16)
    pe2 = jnp.pad(cmp_pe.reshape(2, 2, half), ((0, 0), (0, DEC_ROWS - 2), (0, 0))).astype(BF16)
    if past is None:
        o_da = diff_prompt(seq(da_q), seq(da_k), seq(da_v), diff_lambda, diff_subln, lam_init)
        chunks = cmp_new.reshape(b, t // CMP_STRIDE, chunk_w)
        chunks = jnp.pad(chunks, ((0, 0), (0, LANE - t // CMP_STRIDE), (0, 0)))
        kc_blk, vc_blk = cmp_prompt(chunks, w1lt, pe2, cmp_w2.astype(BF16))
        ng_pad = jnp.pad(ng.reshape(b, t, NSA_KV, 3 * rn), ((0, 0), (0, 0), (0, 0), (0, LANE - 3 * rn)))
        o_nsa = nsa_prompt(seq(nq), seq(nq_rot), kc_blk, vc_blk, seq(ks), seq(vs), seq(kw), seq(vw),
                           ng_pad.reshape(b, t, NSA_KV * LANE))
        o_sb = sb_prompt(seq(sq), seq(sk), seq(sv))
        win_all = win_new
    else:
        rows_view = lambda a: a.reshape(a.shape[0], -1, D_HEAD)
        pool_cmp, pool_sel, pool_sb, win_rows = [rows_view(a) for a in (pool_cmp, pool_sel, pool_sb, win_buf)]
        q_bd =(jnp.swapaxes(da_q.reshape(b, t, DA_KV, DA_HEADS // DA_KV, 2, D_HEAD), 3, 4)[..., None, :]
                * jnp.eye(2, dtype=BF16)[:, None, :, None])
        o = diff_dec(page_table, pool_diff,
                     dec_rows(q_bd.reshape(b, t, DA_KV, 2, DA_HEADS // DA_KV, 2 * D_HEAD)),
                     dec_rows(da_k.reshape(b, t, DA_KV, 2 * D_HEAD)), dec_rows(da_v.reshape(b, t, DA_KV, 2 * D_HEAD)),
                     diff_lambda, diff_subln, lam_init, t)
        o_da = dec_out(o, DA_HEADS // DA_KV)
        xnew = jnp.pad(cmp_new.reshape(b, 1, -1), ((0, 0), (0, 7), (0, chunk_w - t * NSA_KV * 2 * D_HEAD)))
        o_cmp, selexp = nsa_cmp_dec(page_table, pool_cmp,
                                    dec_rows(heads(nq, NSA_KV)), xnew, w1lt, pe2, cmp_w2.astype(BF16), t)
        ng_rows = jnp.moveaxis(ng.reshape(b, t, NSA_KV, rn, 3), 1, 3).reshape(b, NSA_KV, rn * t, 3)
        ng_rows = jnp.pad(ng_rows, ((0, 0), (0, 0), (0, 0), (0, LANE - 3)))
        o = nsa_sel_dec(page_table, pool_sel, dec_rows(heads(nq_rot, NSA_KV)), selexp,
                        dec_rows(heads(ks, NSA_KV)), dec_rows(heads(vs, NSA_KV)), o_cmp, ng_rows,
                        win_rows,
                        dec_rows(heads(kw, NSA_KV)), dec_rows(heads(vw, NSA_KV)), t)
        o_nsa = dec_out(o, rn)
        o = sb_dec(page_table, pool_sb, dec_rows(heads(sq, SB_KV)),
                   dec_rows(heads(sk, SB_KV), PAGE_SIZE), dec_rows(heads(sv, SB_KV), PAGE_SIZE), t)
        o_sb = dec_out(o, SB_HEADS // SB_KV)
        win_all = jnp.concatenate([win_buf, win_new], axis=1)
    win_state = win_all[:, win_all.shape[1] - min(WINDOW, win_all.shape[1]):]

    merged = merge_mm(o_da.reshape(m, DA_QW).astype(BF16), o_nsa.reshape(m, NSA_QW).astype(BF16),
                      o_sb.reshape(m, SB_QW).astype(BF16), proj, w_br_a, w_br_b, w_br_c,
                      g_col=proj.shape[1] - 3 * D_MODEL)
    x2 = resid_mm(x2, merged, w_out)

    fpad = D_FFP - D_FF
    conv_b = jnp.pad(ff_conv_b, (0, fpad))
    cw = jnp.pad(ff_conv_w, ((0, 0), (0, fpad)))
    if past is None:
        act, tail = ffn_act(x2, ln2, w_ff_gate, w_ff_up, cw, conv_b, t)
        conv_state = tail[:, 8 - (CONV_W - 1):, :D_FF]
    else:
        gate_in, up = ffn_up(x2, ln2, w_ff_gate, w_ff_up)
        gp = jnp.concatenate([jnp.pad(conv_buf, ((0, 0), (0, 0), (0, fpad))), gate_in.reshape(b, t, D_FFP)], axis=1)
        conv = conv_b
        for i in range(CONV_W):
            conv = conv + cw[i] * gp[:, i:i + t]
        act = (jax.nn.gelu(conv, approximate=True) * up.reshape(b, t, D_FFP)).astype(BF16).reshape(m, D_FFP)
        conv_state = gp[:, t:, :D_FF]
    x2 = resid_mm(x2, act, w_ff_down)

    x2 = ple_mm(x2, ln3, w_ple_gate, pe.reshape(m, -1).astype(BF16), w_ple)
    return x2.reshape(b, t, D_MODEL), (da_new, cmp_new, sel_new, sb_new, win_state, conv_state)


def kernel(x_prompt, x_sample, cache_diff, cache_cmp, cache_sel, cache_sb, state_win, state_conv, page_table, p_prompt, p_sample, ln1, w_in, diff_lambda, diff_subln, cmp_w1, cmp_w2, cmp_pe, w_br_a, w_br_b, w_br_c, w_out, ln2, w_ff_gate, w_ff_up, w_ff_down, ff_conv_w, ff_conv_b, ln3, w_ple, w_ple_gate, ln_f):
    n_pool = cache_diff.shape[1]
    pools = [c.reshape((DEPTH * n_pool,) + c.shape[2:]) for c in (cache_diff, cache_cmp, cache_sel, cache_sb)]
    xp, xs = x_prompt, x_sample
    st_p, st_s = [], []
    for i in range(DEPTH):
        wts = _prep_weights(w_in[i], w_br_a[i], w_br_b[i], w_br_c[i], w_out[i], w_ff_gate[i], w_ff_up[i],
                            w_ff_down[i], w_ple[i], w_ple_gate[i])
        rest = (diff_lambda[i], diff_subln[i], cmp_w1[i], cmp_w2[i], cmp_pe[i], ln2[i], ff_conv_w[i],
                ff_conv_b[i], ln3[i])
        xp, sp = _layer(xp, p_prompt[i], None, i, ln1[i], wts, *rest)
        past = (page_table + i * n_pool, *pools, state_win[i], state_conv[i])
        xs, ss = _layer(xs, p_sample[i], past, i, ln1[i], wts, *rest)
        st_p.append(sp)
        st_s.append(ss)
    diff_p, cmp_p, sel_p, sb_p, win_p, conv_p = [jnp.stack(a) for a in zip(*st_p)]
    diff_s, cmp_s, sel_s, sb_s, win_s, conv_s = [jnp.stack(a) for a in zip(*st_s)]
    y_prompt = rmsnorm_rows(xp.reshape(-1, D_MODEL), ln_f).reshape(xp.shape)
    y_sample = rmsnorm_rows(xs.reshape(-1, D_MODEL), ln_f).reshape(xs.shape)
    return (y_prompt, y_sample, diff_p, diff_s, cmp_p, cmp_s, sel_p, sel_s, sb_p, sb_s, win_p, win_s, conv_p, conv_s)
```

```python
import functools
import math

import jax
import jax.numpy as jnp
from jax import lax
from jax.experimental import pallas as pl
from jax.experimental.pallas import tpu as pltpu

F32 = jnp.float32
BF16 = jnp.bfloat16

D_MODEL = 2048
DEPTH = 2
PAGE_SIZE = 128
D_HEAD = 128
ROPE_DIM = D_HEAD // 4
ROPE_THETA = 500000.0
DA_HEADS = 4
DA_KV = 2
NSA_HEADS = 8
NSA_KV = 2
CMP_STRIDE = 16
CMP_LEN = 2 * CMP_STRIDE
SEL_BLOCK = 64
SEL_TOPK = 16
WINDOW = 512
SB_HEADS = 8
SB_KV = 4
D_FF = ((8 * D_MODEL // 3 + 127) // 128) * 128
D_FFP = ((D_FF + 511) // 512) * 512
CONV_W = 3
NEG_INF = -1e30
FORCED = 1e30
EPS = 1e-6

DA_QW = DA_HEADS * 2 * D_HEAD
DA_KW = DA_KV * 2 * D_HEAD
NSA_QW = NSA_HEADS * D_HEAD
NSA_KW = NSA_KV * D_HEAD
SB_QW = SB_HEADS * D_HEAD
SB_KW = SB_KV * D_HEAD
SPLITS = (DA_QW, DA_KW, DA_KW, NSA_QW, 6 * NSA_KW, 3 * NSA_HEADS, SB_QW, SB_KW, SB_KW, 3 * D_MODEL)
SPLIT_AT = tuple(sum(SPLITS[:i + 1]) for i in range(len(SPLITS) - 1))
NG_AT = SPLIT_AT[4]
NG_W = 3 * NSA_HEADS
LANE = 128
VMEM_LIMIT = 48 * 1024 * 1024


def _params(sem):
    return pltpu.CompilerParams(dimension_semantics=sem, vmem_limit_bytes=VMEM_LIMIT)


def _rms_rows(x, g):
    return x * lax.rsqrt(jnp.mean(x * x, axis=-1, keepdims=True) + EPS) * g


def _tile(n, pref):
    t = min(n, pref)
    while n % t and t % 2 == 0:
        t //= 2
    assert n % t == 0, (n, pref)
    return t


def _norm_mm_kernel(x_ref, g_ref, w_ref, o_ref, h_ref):
    @pl.when(pl.program_id(1) == 0)
    def _():
        h_ref[...] = _rms_rows(x_ref[...], g_ref[...]).astype(BF16)
    o_ref[...] = jnp.dot(h_ref[...], w_ref[...], preferred_element_type=F32).astype(o_ref.dtype)


def norm_mm(x, g, w, out_dtype=F32, tm=1024, tn=512):
    m, k = x.shape
    n = w.shape[1]
    tm, tn = _tile(m, tm), _tile(n, tn)
    return pl.pallas_call(
        _norm_mm_kernel,
        grid=(m // tm, n // tn),
        in_specs=[pl.BlockSpec((tm, k), lambda i, j: (i, 0)),
                  pl.BlockSpec((1, k), lambda i, j: (0, 0)),
                  pl.BlockSpec((k, tn), lambda i, j: (0, j))],
        out_specs=pl.BlockSpec((tm, tn), lambda i, j: (i, j)),
        out_shape=jax.ShapeDtypeStruct((m, n), out_dtype),
        scratch_shapes=[pltpu.VMEM((tm, k), BF16)],
        compiler_params=_params(("parallel", "arbitrary")),
        name="norm_mm",
    )(x, g.reshape(1, k), w)


def _resid_mm_kernel(x_ref, a_ref, w_ref, o_ref):
    o_ref[...] = x_ref[...] + jnp.dot(a_ref[...], w_ref[...], preferred_element_type=F32)


def resid_mm(x, a, w, tm=512, tn=512):
    m, k = a.shape
    n = w.shape[1]
    tm, tn = _tile(m, tm), _tile(n, tn)
    return pl.pallas_call(
        _resid_mm_kernel,
        grid=(m // tm, n // tn),
        in_specs=[pl.BlockSpec((tm, tn), lambda i, j: (i, j)),
                  pl.BlockSpec((tm, k), lambda i, j: (i, 0)),
                  pl.BlockSpec((k, tn), lambda i, j: (0, j))],
        out_specs=pl.BlockSpec((tm, tn), lambda i, j: (i, j)),
        out_shape=jax.ShapeDtypeStruct((m, n), F32),
        compiler_params=_params(("parallel", "arbitrary")),
        name="resid_mm",
    )(x, a, w)


def _merge_mm_kernel(oa_ref, ob_ref, oc_ref, ga_ref, gb_ref, gc_ref, wa_ref, wb_ref, wc_ref, o_ref):
    def br(o_r, g_r, w_r):
        y = jnp.dot(o_r[...], w_r[...], preferred_element_type=F32)
        return jax.nn.sigmoid(g_r[...]) * y
    o_ref[...] = (br(oa_ref, ga_ref, wa_ref) + br(ob_ref, gb_ref, wb_ref)
                  + br(oc_ref, gc_ref, wc_ref)).astype(o_ref.dtype)


def merge_mm(oa, ob, oc, bg, wa, wb, wc, g_col=0, tm=512, tn=512):
    m, k = oa.shape
    n = wa.shape[1]
    tm, tn = _tile(m, tm), _tile(n, tn)
    nj = n // tn
    assert g_col % tn == 0
    g0 = g_col // tn
    o_spec = pl.BlockSpec((tm, k), lambda i, j: (i, 0))
    w_spec = pl.BlockSpec((k, tn), lambda i, j: (0, j))
    g_specs = [pl.BlockSpec((tm, tn), lambda i, j, s=s: (i, g0 + s * nj + j)) for s in range(3)]
    return pl.pallas_call(
        _merge_mm_kernel,
        grid=(m // tm, nj),
        in_specs=[o_spec, o_spec, o_spec] + g_specs + [w_spec, w_spec, w_spec],
        out_specs=pl.BlockSpec((tm, tn), lambda i, j: (i, j)),
        out_shape=jax.ShapeDtypeStruct((m, n), BF16),
        compiler_params=_params(("parallel", "arbitrary")),
        name="merge_mm",
    )(oa, ob, oc, bg, bg, bg, wa, wb, wc)


def _ffn_up_kernel(x_ref, g_ref, wg_ref, wu_ref, og_ref, ou_ref, h_ref):
    @pl.when(pl.program_id(1) == 0)
    def _():
        h_ref[...] = _rms_rows(x_ref[...], g_ref[...]).astype(BF16)
    h = h_ref[...]
    og_ref[...] = jnp.dot(h, wg_ref[...], preferred_element_type=F32)
    ou_ref[...] = jnp.dot(h, wu_ref[...], preferred_element_type=F32)


def ffn_up(x, g, wg, wu, tm=512, tn=512):
    m, k = x.shape
    n = wg.shape[1]
    tm, tn = _tile(m, tm), _tile(n, tn)
    w_spec = pl.BlockSpec((k, tn), lambda i, j: (0, j))
    o_spec = pl.BlockSpec((tm, tn), lambda i, j: (i, j))
    return pl.pallas_call(
        _ffn_up_kernel,
        grid=(m // tm, n // tn),
        in_specs=[pl.BlockSpec((tm, k), lambda i, j: (i, 0)),
                  pl.BlockSpec((1, k), lambda i, j: (0, 0)), w_spec, w_spec],
        out_specs=[o_spec, o_spec],
        out_shape=[jax.ShapeDtypeStruct((m, n), F32), jax.ShapeDtypeStruct((m, n), F32)],
        scratch_shapes=[pltpu.VMEM((tm, k), BF16)],
        compiler_params=_params(("parallel", "arbitrary")),
        name="ffn_up",
    )(x, g.reshape(1, k), wg, wu)


def _ffn_act_kernel(x_ref, g_ref, wg_ref, wu_ref, cw_ref, cb_ref, act_ref, tail_ref, h_ref, carry_ref, *, seq_tiles):
    i, j = pl.program_id(0), pl.program_id(1)

    @pl.when(j == 0)
    def _():
        h_ref[...] = _rms_rows(x_ref[...], g_ref[...]).astype(BF16)
    h = h_ref[...]
    gate = jnp.dot(h, wg_ref[...], preferred_element_type=F32)
    up = jnp.dot(h, wu_ref[...], preferred_element_type=F32)
    tm = gate.shape[0]

    @pl.when(i % seq_tiles == 0)
    def _():
        carry_ref[j] = jnp.zeros(carry_ref.shape[1:], F32)
    carry = carry_ref[j]
    row = _iota((tm, 1), 0)
    g1 = jnp.where(row == 0, carry[7:8], pltpu.roll(gate, 1, 0))
    g2 = jnp.where(row == 0, carry[6:7], jnp.where(row == 1, carry[7:8], pltpu.roll(gate, 2, 0)))
    cw = cw_ref[...]
    conv = cb_ref[...] + cw[0:1] * g2 + cw[1:2] * g1 + cw[2:3] * gate
    act_ref[...] = (jax.nn.gelu(conv, approximate=True) * up).astype(act_ref.dtype)
    carry_ref[j] = gate[tm - 8:tm]
    tail_ref[0] = gate[tm - 8:tm]


def ffn_act(x, g, wg, wu, cw, cb, seq_len, tm=1024, tn=512):
    m, k = x.shape
    n = wg.shape[1]
    tm, tn = _tile(seq_len, tm), _tile(n, tn)
    seq_tiles = seq_len // tm
    w_spec = pl.BlockSpec((k, tn), lambda i, j: (0, j))
    act, tails = pl.pallas_call(
        functools.partial(_ffn_act_kernel, seq_tiles=seq_tiles),
        grid=(m // tm, n // tn),
        in_specs=[pl.BlockSpec((tm, k), lambda i, j: (i, 0)),
                  pl.BlockSpec((1, k), lambda i, j: (0, 0)), w_spec, w_spec,
                  pl.BlockSpec((8, tn), lambda i, j: (0, j)), pl.BlockSpec((1, tn), lambda i, j: (0, j))],
        out_specs=[pl.BlockSpec((tm, tn), lambda i, j: (i, j)),
                   pl.BlockSpec((1, 8, tn), lambda i, j: (i, 0, j))],
        out_shape=[jax.ShapeDtypeStruct((m, n), BF16), jax.ShapeDtypeStruct((m // tm, 8, n), F32)],
        scratch_shapes=[pltpu.VMEM((tm, k), BF16), pltpu.VMEM((n // tn, 8, tn), F32)],
        compiler_params=_params(("arbitrary", "arbitrary")),
        name="ffn_act",
    )(x, g.reshape(1, k), wg, wu, jnp.pad(cw, ((0, 8 - cw.shape[0]), (0, 0))), cb.reshape(1, n))
    return act, tails[seq_tiles - 1::seq_tiles]


def _ple_kernel(x_ref, xt_ref, g_ref, wg_ref, pe_ref, wp_ref, o_ref, h_ref):
    @pl.when(pl.program_id(1) == 0)
    def _():
        h_ref[...] = _rms_rows(x_ref[...], g_ref[...]).astype(BF16)
    gate = jax.nn.sigmoid(jnp.dot(h_ref[...], wg_ref[...], preferred_element_type=F32))
    emb = jnp.dot(pe_ref[...], wp_ref[...], preferred_element_type=F32)
    o_ref[...] = xt_ref[...] + gate * emb


def ple_mm(x, g, wg, pe, wp, tm=512, tn=512):
    m, k = x.shape
    n = wg.shape[1]
    kp = pe.shape[1]
    tm, tn = _tile(m, tm), _tile(n, tn)
    return pl.pallas_call(
        _ple_kernel,
        grid=(m // tm, n // tn),
        in_specs=[pl.BlockSpec((tm, k), lambda i, j: (i, 0)),
                  pl.BlockSpec((tm, tn), lambda i, j: (i, j)),
                  pl.BlockSpec((1, k), lambda i, j: (0, 0)),
                  pl.BlockSpec((k, tn), lambda i, j: (0, j)),
                  pl.BlockSpec((tm, kp), lambda i, j: (i, 0)),
                  pl.BlockSpec((kp, tn), lambda i, j: (0, j))],
        out_specs=pl.BlockSpec((tm, tn), lambda i, j: (i, j)),
        out_shape=jax.ShapeDtypeStruct((m, n), F32),
        scratch_shapes=[pltpu.VMEM((tm, k), BF16)],
        compiler_params=_params(("parallel", "arbitrary")),
        name="ple_mm",
    )(x, x, g.reshape(1, k), wg, pe, wp)


def _rmsnorm_kernel(x_ref, g_ref, o_ref):
    o_ref[...] = _rms_rows(x_ref[...], g_ref[...])


def rmsnorm_rows(x, g, tm=512):
    m, k = x.shape
    tm = _tile(m, tm)
    return pl.pallas_call(
        _rmsnorm_kernel,
        grid=(m // tm,),
        in_specs=[pl.BlockSpec((tm, k), lambda i: (i, 0)), pl.BlockSpec((1, k), lambda i: (0, 0))],
        out_specs=pl.BlockSpec((tm, k), lambda i: (i, 0)),
        out_shape=jax.ShapeDtypeStruct((m, k), F32),
        compiler_params=_params(("parallel",)),
        name="final_rmsnorm",
    )(x, g.reshape(1, k))


_NT = (((1,), (1,)), ((), ()))
ATT_TQ = 256
ATT_TK = 256
NSA_TQ = 128
ROW_BLOCK = 128


def _iota(shape, dim):
    return lax.broadcasted_iota(jnp.int32, shape, dim)


def _lanes(x, width):
    return x if width == LANE else jnp.concatenate([x] * (width // LANE), axis=1)


def _split_bf16(x):
    hi = x.astype(BF16)
    lo = (x - hi.astype(F32)).astype(BF16)
    return hi, lo


def _diff_prompt_kernel(lam_ref, q_ref, k_ref, v_ref, sub_ref, o_ref, m_ref, l_ref, acc_ref, *, tq, tk, lam_init):
    i = pl.program_id(2)
    rows = 2 * tq
    n_kv = (i * tq + tq + tk - 1) // tk
    for c in range(2):
        m_ref[c] = jnp.full((rows, LANE), NEG_INF, F32)
        l_ref[c] = jnp.zeros((rows, LANE), F32)
        acc_ref[c] = jnp.zeros((rows, 2 * D_HEAD), F32)

        def step(j, diagonal, c=c):
            k0 = pl.multiple_of(j * tk, tk)
            ks = k_ref[0, pl.ds(k0, tk), c * D_HEAD:(c + 1) * D_HEAD]
            vs = v_ref[0, pl.ds(k0, tk), :]
            kpos = k0 + _iota((1, tk), 1)
            for b0 in range(0, rows, ROW_BLOCK):
                r, off = divmod(b0, tq)
                rs = slice(b0, b0 + ROW_BLOCK)
                qc = q_ref[0, off:off + ROW_BLOCK, (r * 2 + c) * D_HEAD:(r * 2 + c + 1) * D_HEAD]
                s = lax.dot_general(qc, ks, _NT, preferred_element_type=F32)
                if diagonal:
                    ok = kpos <= i * tq + off + _iota((ROW_BLOCK, 1), 0)
                    s = jnp.where(ok, s, NEG_INF)
                m_old = m_ref[c, rs]
                m_new = jnp.maximum(m_old, jnp.max(s, axis=-1, keepdims=True))
                alpha = jnp.exp(m_old - m_new)
                p = jnp.exp(s - _lanes(m_new, tk))
                if diagonal:
                    p = jnp.where(ok, p, 0.0)
                l_ref[c, rs] = alpha * l_ref[c, rs] + jnp.sum(p, axis=-1, keepdims=True)
                acc_ref[c, rs] = (_lanes(alpha, 2 * D_HEAD) * acc_ref[c, rs]
                                  + jnp.dot(p.astype(BF16), vs, preferred_element_type=F32))
                m_ref[c, rs] = m_new

        def body(j, carry, step=step):
            step(j, False)
            return carry

        lax.fori_loop(0, n_kv - 1, body, 0)
        step(n_kv - 1, True)
    lp = lam_ref[...]
    lam = (jnp.exp(jnp.sum(lp[0:1] * lp[1:2], axis=-1, keepdims=True))
           - jnp.exp(jnp.sum(lp[2:3] * lp[3:4], axis=-1, keepdims=True)) + lam_init)
    o = (acc_ref[0] / _lanes(l_ref[0], 2 * D_HEAD) - lam * (acc_ref[1] / _lanes(l_ref[1], 2 * D_HEAD)))
    o = _rms_rows(o, sub_ref[...]) * (1.0 - lam_init)
    for r in range(2):
        o_ref[0, :, r * 2 * D_HEAD:(r + 1) * 2 * D_HEAD] = o[r * tq:(r + 1) * tq].astype(o_ref.dtype)


def diff_prompt(q, k, v, diff_lambda, diff_subln, lam_init, tq=ATT_TQ, tk=ATT_TK):
    b, t, _ = q.shape
    tq, tk = _tile(t, tq), _tile(t, tk)
    assert tk % tq == 0
    rows = 2 * tq
    return pl.pallas_call(
        functools.partial(_diff_prompt_kernel, tq=tq, tk=tk, lam_init=lam_init),
        grid=(b, DA_KV, t // tq),
        in_specs=[pl.BlockSpec((4, D_HEAD), lambda bi, g, i: (0, 0)),
                  pl.BlockSpec((1, tq, 4 * D_HEAD), lambda bi, g, i: (bi, i, g)),
                  pl.BlockSpec((1, t, 2 * D_HEAD), lambda bi, g, i: (bi, 0, g)),
                  pl.BlockSpec((1, t, 2 * D_HEAD), lambda bi, g, i: (bi, 0, g)),
                  pl.BlockSpec((1, 2 * D_HEAD), lambda bi, g, i: (0, 0))],
        out_specs=pl.BlockSpec((1, tq, 4 * D_HEAD), lambda bi, g, i: (bi, i, g)),
        out_shape=jax.ShapeDtypeStruct((b, t, DA_QW), BF16),
        scratch_shapes=[pltpu.VMEM((2, rows, LANE), F32), pltpu.VMEM((2, rows, LANE), F32),
                        pltpu.VMEM((2, rows, 2 * D_HEAD), F32)],
        compiler_params=_params(("parallel", "parallel", "arbitrary")),
        name="diff_prompt",
    )(diff_lambda.astype(F32), q, k, v, diff_subln.reshape(1, -1).astype(F32))


def _sb_prompt_kernel(q_ref, k_ref, v_ref, u_ref, o_ref, c_ref, acc_ref, *, tq, tk):
    i = pl.program_id(2)
    rows = 2 * tq
    n_kv = (i * tq + tq + tk - 1) // tk
    c_ref[...] = jnp.zeros((rows, LANE), F32)
    acc_ref[...] = jnp.zeros((rows, D_HEAD), F32)

    def step(j, diagonal):
        k0 = pl.multiple_of(j * tk, tk)
        ks = k_ref[0, pl.ds(k0, tk), :]
        vs = v_ref[0, pl.ds(k0, tk), :]
        u = u_ref[...]
        qs = jnp.concatenate([q_ref[0, :, r * D_HEAD:(r + 1) * D_HEAD] for r in range(2)], axis=0)
        z = lax.dot_general(qs, ks, _NT, preferred_element_type=F32)
        sp = jnp.maximum(z, 0.0) + jnp.log(1.0 + jnp.exp(-jnp.abs(z)))
        lk = -sp
        if diagonal:
            strict = (k0 + _iota((1, tk), 1)) < i * tq + _iota((rows, 1), 0) % tq
            lk = jnp.where(strict, lk, 0.0)
        hi, lo = _split_bf16(lk)
        later = (jnp.dot(hi, u, preferred_element_type=F32) + jnp.dot(lo, u, preferred_element_type=F32)
                 + _lanes(c_ref[...], tk))
        a = jnp.exp(z - sp + later)
        if diagonal:
            a = jnp.where(strict, a, 0.0)
        acc_ref[...] += jnp.dot(a.astype(BF16), vs, preferred_element_type=F32)
        c_ref[...] += jnp.sum(lk, axis=-1, keepdims=True)

    def body(jj, carry):
        step(n_kv - 2 - jj, False)
        return carry

    step(n_kv - 1, True)
    lax.fori_loop(0, n_kv - 1, body, 0)
    for r in range(2):
        o_ref[0, :, r * D_HEAD:(r + 1) * D_HEAD] = acc_ref[r * tq:(r + 1) * tq, :].astype(o_ref.dtype)


def _suffix_matrix(tk):
    return (_iota((tk, tk), 0) > _iota((tk, tk), 1)).astype(BF16)


def sb_prompt(q, k, v, tq=ATT_TQ, tk=ATT_TK):
    b, t, _ = q.shape
    tq, tk = _tile(t, tq), _tile(t, tk)
    assert tk % tq == 0
    rows = 2 * tq
    return pl.pallas_call(
        functools.partial(_sb_prompt_kernel, tq=tq, tk=tk),
        grid=(b, SB_KV, t // tq),
        in_specs=[pl.BlockSpec((1, tq, 2 * D_HEAD), lambda bi, g, i: (bi, i, g)),
                  pl.BlockSpec((1, t, D_HEAD), lambda bi, g, i: (bi, 0, g)),
                  pl.BlockSpec((1, t, D_HEAD), lambda bi, g, i: (bi, 0, g)),
                  pl.BlockSpec((tk, tk), lambda bi, g, i: (0, 0))],
        out_specs=pl.BlockSpec((1, tq, 2 * D_HEAD), lambda bi, g, i: (bi, i, g)),
        out_shape=jax.ShapeDtypeStruct((b, t, SB_QW), BF16),
        scratch_shapes=[pltpu.VMEM((rows, LANE), F32), pltpu.VMEM((rows, D_HEAD), F32)],
        compiler_params=_params(("parallel", "parallel", "arbitrary")),
        name="sb_prompt",
    )(q, k, v, _suffix_matrix(tk))


def _nsa_prompt_kernel(q_ref, qr_ref, kc_ref, vc_ref, ks_ref, vs_ref, kw_ref, vw_ref, ng_ref, cov_ref, exp_ref,
                       o_ref, sel_ref, m_ref, l_ref, acc_ref, *, tq, tk, t, n_cmp, n_sel):
    i = pl.program_id(2)
    rn = NSA_HEADS // NSA_KV
    rows = rn * tq
    q0 = i * tq
    qpos_t = q0 + _iota((tq, 1), 0)
    qpos = q0 + _iota((rows, 1), 0) % tq
    lane = _iota((1, LANE), 1)
    stack = lambda ref: jnp.concatenate([ref[0, :, r * D_HEAD:(r + 1) * D_HEAD] for r in range(rn)], axis=0)

    s = lax.dot_general(stack(q_ref), kc_ref[0, 0], _NT, preferred_element_type=F32)
    c_ok = ((lane * CMP_STRIDE + (CMP_LEN - 1)) <= qpos) & (lane < n_cmp)
    s = jnp.where(c_ok, s, NEG_INF)
    e = jnp.where(c_ok, jnp.exp(s - jnp.max(s, axis=-1, keepdims=True)), 0.0)
    den = jnp.sum(e, axis=-1, keepdims=True)
    p = e / jnp.where(den > 0.0, den, 1.0)
    o_cmp = jnp.dot(p.astype(BF16), vc_ref[0, 0], preferred_element_type=F32)
    psum = p[0:tq]
    for r in range(1, rn):
        psum = psum + p[r * tq:(r + 1) * tq]
    hi, lo = _split_bf16(psum)
    cov = cov_ref[...]
    imp = jnp.dot(hi, cov, preferred_element_type=F32) + jnp.dot(lo, cov, preferred_element_type=F32)

    cur = qpos_t // SEL_BLOCK
    forced = (lane == 0) | (lane == cur) | (lane == cur - 1)
    causal = lane * SEL_BLOCK <= qpos_t
    score = jnp.where(forced, FORCED, jnp.where(causal, imp, NEG_INF))
    score = jnp.where(lane < n_sel, score, -3e38)
    cnt = jnp.zeros((tq, LANE), F32)
    for sp in range(n_sel):
        col = score[:, sp:sp + 1]
        tie = jnp.where(lane > sp, 1.0, 0.0)
        cnt = cnt + jnp.where(col > score, 1.0, jnp.where(col == score, tie, 0.0))
    selm = jnp.where((cnt < float(min(SEL_TOPK, n_sel))) & (lane < n_sel), 1.0, 0.0).astype(BF16)
    for jj in range(t // tk):
        sel_ref[jj] = jnp.dot(selm, exp_ref[:, jj * tk:(jj + 1) * tk], preferred_element_type=F32)

    m_ref[...] = jnp.full((rows, LANE), NEG_INF, F32)
    l_ref[...] = jnp.zeros((rows, LANE), F32)
    acc_ref[...] = jnp.zeros((rows, D_HEAD), F32)
    n_kv = (q0 + tq + tk - 1) // tk

    def body(j, carry):
        k0 = pl.multiple_of(j * tk, tk)
        ks = ks_ref[0, pl.ds(k0, tk), :]
        vs = vs_ref[0, pl.ds(k0, tk), :]
        ok = (sel_ref[j] > 0.5) & ((k0 + _iota((1, tk), 1)) <= qpos_t)
        for r in range(rn):
            rs = slice(r * tq, (r + 1) * tq)
            sj = lax.dot_general(qr_ref[0, :, r * D_HEAD:(r + 1) * D_HEAD], ks, _NT, preferred_element_type=F32)
            sj = jnp.where(ok, sj, NEG_INF)
            m_old = m_ref[rs]
            m_new = jnp.maximum(m_old, jnp.max(sj, axis=-1, keepdims=True))
            alpha = jnp.exp(m_old - m_new)
            pj = jnp.where(ok, jnp.exp(sj - _lanes(m_new, tk)), 0.0)
            l_ref[rs] = alpha * l_ref[rs] + jnp.sum(pj, axis=-1, keepdims=True)
            acc_ref[rs] = alpha * acc_ref[rs] + jnp.dot(pj.astype(BF16), vs, preferred_element_type=F32)
            m_ref[rs] = m_new
        return carry

    lax.fori_loop(0, n_kv, body, 0)

    wlen = WINDOW + tq
    w0 = pl.multiple_of(jnp.maximum(q0 - WINDOW, 0), tq)
    kwin = kw_ref[0, pl.ds(w0, wlen), :]
    vwin = vw_ref[0, pl.ds(w0, wlen), :]
    dist = qpos_t - (w0 + _iota((1, wlen), 1))
    w_ok = (dist >= 0) & (dist < WINDOW)
    gates = jax.nn.sigmoid(ng_ref[0])
    for r in range(rn):
        sw = lax.dot_general(qr_ref[0, :, r * D_HEAD:(r + 1) * D_HEAD], kwin, _NT, preferred_element_type=F32)
        sw = jnp.where(w_ok, sw, NEG_INF)
        ew = jnp.where(w_ok, jnp.exp(sw - jnp.max(sw, axis=-1, keepdims=True)), 0.0)
        o_win = jnp.dot(ew.astype(BF16), vwin, preferred_element_type=F32) / jnp.sum(ew, axis=-1, keepdims=True)
        rs = slice(r * tq, (r + 1) * tq)
        o_sel = acc_ref[rs, :] / l_ref[rs, :]
        out = (gates[:, 3 * r:3 * r + 1] * o_cmp[rs] + gates[:, 3 * r + 1:3 * r + 2] * o_sel
               + gates[:, 3 * r + 2:3 * r + 3] * o_win)
        o_ref[0, :, r * D_HEAD:(r + 1) * D_HEAD] = out.astype(o_ref.dtype)


def nsa_prompt(q, qr, kc, vc, ks, vs, kw, vw, ng, tq=NSA_TQ, tk=ATT_TK):
    b, t, _ = q.shape
    tq, tk = _tile(t, tq), _tile(t, tk)
    assert tk % tq == 0
    n_cmp = t // CMP_STRIDE - 1
    n_sel = t // SEL_BLOCK
    assert t % SEL_BLOCK == 0 and n_cmp <= LANE and n_sel <= LANE and t >= WINDOW + tq and WINDOW % tq == 0
    rn = NSA_HEADS // NSA_KV
    rows = rn * tq
    cmp_i = _iota((LANE, LANE), 0)
    sel_i = _iota((LANE, LANE), 1)
    cover = ((cmp_i * CMP_STRIDE <= sel_i * SEL_BLOCK + SEL_BLOCK - 1)
             & (cmp_i * CMP_STRIDE + CMP_LEN - 1 >= sel_i * SEL_BLOCK)
             & (cmp_i < n_cmp) & (sel_i < n_sel)).astype(BF16)
    expand = (_iota((LANE, t), 0) == _iota((LANE, t), 1) // SEL_BLOCK).astype(BF16)
    qspec = pl.BlockSpec((1, tq, rn * D_HEAD), lambda bi, g, i: (bi, i, g))
    cspec = pl.BlockSpec((1, 1, LANE, D_HEAD), lambda bi, g, i: (bi, g, 0, 0))
    kspec = pl.BlockSpec((1, t, D_HEAD), lambda bi, g, i: (bi, 0, g))
    return pl.pallas_call(
        functools.partial(_nsa_prompt_kernel, tq=tq, tk=tk, t=t, n_cmp=n_cmp, n_sel=n_sel),
        grid=(b, NSA_KV, t // tq),
        in_specs=[qspec, qspec, cspec, cspec, kspec, kspec, kspec, kspec,
                  pl.BlockSpec((1, tq, LANE), lambda bi, g, i: (bi, i, g)),
                  pl.BlockSpec((LANE, LANE), lambda bi, g, i: (0, 0)),
                  pl.BlockSpec((LANE, t), lambda bi, g, i: (0, 0))],
        out_specs=qspec,
        out_shape=jax.ShapeDtypeStruct((b, t, NSA_QW), BF16),
        scratch_shapes=[pltpu.VMEM((t // tk, tq, tk), F32), pltpu.VMEM((rows, LANE), F32), pltpu.VMEM((rows, LANE), F32),
                        pltpu.VMEM((rows, D_HEAD), F32)],
        compiler_params=_params(("parallel", "parallel", "arbitrary")),
        name="nsa_prompt",
    )(q, qr, kc, vc, ks, vs, kw, vw, ng, cover, expand)


QKV_W = DA_QW + 2 * DA_KW + NSA_QW + 6 * NSA_KW + SB_QW + 2 * SB_KW
POST_BF16 = (DA_QW, DA_KW, DA_KW, NSA_QW, NSA_QW, NSA_KW, NSA_KW, NSA_KW, NSA_KW, SB_QW, SB_KW, SB_KW)
POST_F32 = (2 * DA_KW, 2 * NSA_KW, 2 * NSA_KW, 2 * NSA_KW, 2 * SB_KW)


def _qkv_post_kernel(p_ref, c_ref, s1_ref, s2_ref, daq_ref, dak_ref, dav_ref, nq_ref, nqr_ref, ks_ref, vs_ref, kw_ref,
                     vw_ref, sq_ref, sk_ref, sv_ref, dst_ref, cst_ref, sst_ref, wst_ref, bst_ref):
    scale = D_HEAD ** -0.5
    cos, s1, s2 = c_ref[...], s1_ref[...], s2_ref[...]
    head = lambda col: p_ref[:, col:col + D_HEAD]

    def rot(x):
        return x * cos + pltpu.roll(x, ROPE_DIM // 2, 1) * s1 + pltpu.roll(x, D_HEAD - ROPE_DIM // 2, 1) * s2

    def put(ref, j, x):
        ref[:, j * D_HEAD:(j + 1) * D_HEAD] = x.astype(ref.dtype)

    col = 0
    for j in range(DA_QW // D_HEAD):
        put(daq_ref, j, rot(head(col + j * D_HEAD)) * scale)
    col += DA_QW
    for g in range(DA_KV):
        for c in range(2):
            k = rot(head(col + (2 * g + c) * D_HEAD))
            put(dak_ref, 2 * g + c, k)
            put(dst_ref, 4 * g + c, k)
            v = head(col + DA_KW + (2 * g + c) * D_HEAD)
            put(dav_ref, 2 * g + c, v)
            put(dst_ref, 4 * g + 2 + c, v)
    col += 2 * DA_KW
    for j in range(NSA_QW // D_HEAD):
        x = head(col + j * D_HEAD)
        put(nq_ref, j, x * scale)
        put(nqr_ref, j, rot(x) * scale)
    col += NSA_QW
    for g in range(NSA_KV):
        kc, vc, ks, vs, kw, vw = [head(col + (NSA_KV * i + g) * D_HEAD) for i in range(6)]
        ks, kw = rot(ks), rot(kw)
        put(cst_ref, 2 * g, kc)
        put(cst_ref, 2 * g + 1, vc)
        put(ks_ref, g, ks)
        put(vs_ref, g, vs)
        put(sst_ref, 2 * g, ks)
        put(sst_ref, 2 * g + 1, vs)
        put(kw_ref, g, kw)
        put(vw_ref, g, vw)
        put(wst_ref, 2 * g, kw)
        put(wst_ref, 2 * g + 1, vw)
    col += 6 * NSA_KW
    for j in range(SB_QW // D_HEAD):
        put(sq_ref, j, head(col + j * D_HEAD) * scale)
    col += SB_QW
    for g in range(SB_KV):
        k = head(col + g * D_HEAD)
        v = head(col + SB_KW + g * D_HEAD)
        put(sk_ref, g, k)
        put(sv_ref, g, v)
        put(bst_ref, 2 * g, k)
        put(bst_ref, 2 * g + 1, v)


def qkv_post(proj, pos, tm=256):
    m = proj.shape[0]
    r = pos.shape[0]
    tm = min(tm, r) if r % min(tm, r) == 0 else r
    if m % tm:
        tm = m
    assert r % tm == 0 and m % tm == 0, (m, r, tm)
    half = ROPE_DIM // 2
    inv = ROPE_THETA ** (-(jnp.arange(half, dtype=F32) * 2.0 / ROPE_DIM))
    ang = pos.astype(F32)[:, None] * inv[None, :]
    cos, sin = jnp.cos(ang), jnp.sin(ang)
    rest = D_HEAD - ROPE_DIM
    c_tab = jnp.concatenate([cos, cos, jnp.ones((r, rest), F32)], axis=1)
    s1_tab = jnp.concatenate([jnp.zeros((r, half), F32), sin, jnp.zeros((r, rest), F32)], axis=1)
    s2_tab = jnp.concatenate([-sin, jnp.zeros((r, half + rest), F32)], axis=1)
    period = r // tm
    t_spec = pl.BlockSpec((tm, D_HEAD), lambda i: (i % period, 0))
    outs = [(w, BF16) for w in POST_BF16] + [(w, F32) for w in POST_F32]
    return pl.pallas_call(
        _qkv_post_kernel,
        grid=(m // tm,),
        in_specs=[pl.BlockSpec((tm, QKV_W), lambda i: (i, 0)), t_spec, t_spec, t_spec],
        out_specs=[pl.BlockSpec((tm, w), lambda i: (i, 0)) for w, _ in outs],
        out_shape=[jax.ShapeDtypeStruct((m, w), dt) for w, dt in outs],
        compiler_params=_params(("parallel",)),
        name="qkv_post",
    )(proj, c_tab, s1_tab, s2_tab)


def _cmp_prompt_kernel(x_ref, w1_ref, pe_ref, w2_ref, kc_ref, vc_ref):
    x = x_ref[0]
    n = x.shape[0]
    row_w = NSA_KV * 2 * D_HEAD
    for kv, out_ref in ((0, kc_ref), (1, vc_ref)):
        pw = jnp.dot(pe_ref[kv], w1_ref[kv], preferred_element_type=F32)
        pe_w1 = pw[0:1, :D_HEAD] + pw[1:2, D_HEAD:]
        for g in range(NSA_KV):
            c4 = 2 * g + kv
            xc = jnp.concatenate([x[:, r * row_w + c4 * D_HEAD:r * row_w + (c4 + 1) * D_HEAD]
                                  for r in range(CMP_STRIDE)], axis=1).astype(BF16)
            lt = jnp.dot(xc, w1_ref[kv], preferred_element_type=F32)
            h = jax.nn.gelu(lt[:, :D_HEAD] + pltpu.roll(lt[:, D_HEAD:], n - 1, 0) + pe_w1, approximate=True)
            out_ref[0, g] = jnp.dot(h.astype(BF16), w2_ref[kv], preferred_element_type=F32).astype(out_ref.dtype)


def cmp_prompt(chunks, w1lt, pe2, w2):
    b, n, w = chunks.shape
    assert n == LANE
    full = lambda shape: pl.BlockSpec(shape, lambda bi: (0,) * len(shape))
    o_spec = pl.BlockSpec((1, NSA_KV, n, D_HEAD), lambda bi: (bi, 0, 0, 0))
    return pl.pallas_call(
        _cmp_prompt_kernel,
        grid=(b,),
        in_specs=[pl.BlockSpec((1, n, w), lambda bi: (bi, 0, 0)), full(w1lt.shape), full(pe2.shape), full(w2.shape)],
        out_specs=[o_spec, o_spec],
        out_shape=[jax.ShapeDtypeStruct((b, NSA_KV, n, D_HEAD), BF16)] * 2,
        compiler_params=_params(("parallel",)),
        name="cmp_prompt",
    )(chunks, w1lt, pe2, w2)


DEC_ROWS = 16
MAX_PAGES_PER_STEP = 8


def _pages_per_step(n_pages):
    return max(d for d in range(1, MAX_PAGES_PER_STEP + 1) if n_pages % d == 0)


def _page_specs(n_per, pool, n_pages=None):
    blk = (1,) + pool.shape[1:]
    zeros = (0,) * (pool.ndim - 1)

    def spec(p):
        if n_pages is None:
            return pl.BlockSpec(blk, lambda b, c, pt: (pt[b, c * n_per + p],) + zeros)
        return pl.BlockSpec(blk, lambda b, c, pt: (pt[b, n_pages - (c + 1) * n_per + p],) + zeros)
    return [spec(p) for p in range(n_per)]


def _page_rows(pages, g, kv, n_groups=None):
    if n_groups is None:
        tiles = [pg[0, :, g, kv, :] for pg in pages]
    else:
        tiles = [pg[0, pl.ds(2 * g + kv, PAGE_SIZE, stride=2 * n_groups), :] for pg in pages]
    return jnp.concatenate(tiles, axis=0).astype(BF16)


def _softmax_steps(m_ref, l_ref, acc_ref, items):
    masked = items[0][3] is not None
    assert all((ok is not None) == masked for _, _, _, ok in items)
    rows = items[0][1].shape[0]
    cat = lambda xs: jnp.concatenate(xs, axis=0) if len(xs) > 1 else xs[0]
    s = cat([s for _, s, _, _ in items])
    if masked:
        ok = cat([ok for _, _, _, ok in items])
        s = jnp.where(ok, s, NEG_INF)
    m_old = cat([m_ref[g] for g, _, _, _ in items])
    l_old = cat([l_ref[g] for g, _, _, _ in items])
    m_new = jnp.maximum(m_old, jnp.max(s, axis=-1, keepdims=True))
    alpha = jnp.exp(m_old - m_new)
    p = jnp.exp(s - m_new)
    if masked:
        p = jnp.where(ok, p, 0.0)
    l_new = alpha * l_old + jnp.sum(p, axis=-1, keepdims=True)
    p = p.astype(BF16)
    for i, (g, _, v, _) in enumerate(items):
        sl = slice(i * rows, (i + 1) * rows)
        acc_ref[g] = alpha[sl] * acc_ref[g] + jnp.dot(p[sl], v, preferred_element_type=F32)
        m_ref[g] = m_new[sl]
        l_ref[g] = l_new[sl]


def _softmax_step(m_ref, l_ref, acc_ref, g, s, v, ok):
    _softmax_steps(m_ref, l_ref, acc_ref, [(g, s, v, ok)])


def _new_causal(t_new):
    return _iota((DEC_ROWS, DEC_ROWS), 1) <= _iota((DEC_ROWS, DEC_ROWS), 0) % t_new


def _diff_dec_kernel(pt_ref, lam_ref, q_ref, kn_ref, vn_ref, sub_ref, *rest, n_per, lam_init, t_new):
    pages, o_ref, (m_ref, l_ref, acc_ref) = rest[:n_per], rest[n_per], rest[n_per + 1:]
    c = pl.program_id(1)

    @pl.when(c == 0)
    def _():
        m_ref[...] = jnp.full(m_ref.shape, NEG_INF, F32)
        l_ref[...] = jnp.zeros(l_ref.shape, F32)
        acc_ref[...] = jnp.zeros(acc_ref.shape, F32)

    _softmax_steps(m_ref, l_ref, acc_ref, [
        (g, lax.dot_general(q_ref[0, g], _page_rows(pages, g, 0), _NT, preferred_element_type=F32),
         _page_rows(pages, g, 1), None) for g in range(DA_KV)])

    @pl.when(c == pl.num_programs(1) - 1)
    def _():
        lp = lam_ref[...]
        lam = (jnp.exp(jnp.sum(lp[0:1] * lp[1:2], axis=-1, keepdims=True))
               - jnp.exp(jnp.sum(lp[2:3] * lp[3:4], axis=-1, keepdims=True)) + lam_init)
        for g in range(DA_KV):
            s = lax.dot_general(q_ref[0, g], kn_ref[0, g], _NT, preferred_element_type=F32)
            _softmax_step(m_ref, l_ref, acc_ref, g, s, vn_ref[0, g], _new_causal(t_new))
            a = acc_ref[g] / l_ref[g]
            half = DEC_ROWS // 2
            o = a[0:half] - lam * a[half:DEC_ROWS]
            o_ref[0, g] = _rms_rows(o, sub_ref[...]) * (1.0 - lam_init)


def diff_dec(page_table, pool, q, kn, vn, diff_lambda, diff_subln, lam_init, t_new):
    b, n_pages = page_table.shape
    n_per = _pages_per_step(n_pages)
    blk = lambda w: pl.BlockSpec((1, DA_KV, DEC_ROWS, w), lambda bi, c, pt: (bi, 0, 0, 0))
    grid_spec = pltpu.PrefetchScalarGridSpec(
        num_scalar_prefetch=1, grid=(b, n_pages // n_per),
        in_specs=[pl.BlockSpec((4, D_HEAD), lambda bi, c, pt: (0, 0)), blk(2 * D_HEAD), blk(2 * D_HEAD), blk(2 * D_HEAD),
                  pl.BlockSpec((1, 2 * D_HEAD), lambda bi, c, pt: (0, 0))] + _page_specs(n_per, pool),
        out_specs=pl.BlockSpec((1, DA_KV, DEC_ROWS // 2, 2 * D_HEAD), lambda bi, c, pt: (bi, 0, 0, 0)),
        scratch_shapes=[pltpu.VMEM((DA_KV, DEC_ROWS, 1), F32), pltpu.VMEM((DA_KV, DEC_ROWS, 1), F32),
                        pltpu.VMEM((DA_KV, DEC_ROWS, 2 * D_HEAD), F32)])
    return pl.pallas_call(
        functools.partial(_diff_dec_kernel, n_per=n_per, lam_init=lam_init, t_new=t_new),
        grid_spec=grid_spec,
        out_shape=jax.ShapeDtypeStruct((b, DA_KV, DEC_ROWS // 2, 2 * D_HEAD), F32),
        compiler_params=_params(("parallel", "arbitrary")),
        name="diff_dec",
    )(page_table, diff_lambda.astype(F32), q, kn, vn, diff_subln.reshape(1, -1).astype(F32), *([pool] * n_per))


def _sb_dec_kernel(pt_ref, q_ref, kn_ref, vn_ref, *rest, n_per, t_new):
    pages, o_ref, (c_ref, acc_ref) = rest[:n_per], rest[n_per], rest[n_per + 1:]
    c = pl.program_id(1)

    def tiles(kv_of, strict):
        kvs = [kv_of(g) for g in range(SB_KV)]
        n = kvs[0][0].shape[0]
        lane = _iota((1, n), 1)
        z = jnp.concatenate([lax.dot_general(q_ref[0, g], k, _NT, preferred_element_type=F32)
                             for g, (k, _) in enumerate(kvs)], axis=0)
        sp = jnp.maximum(z, 0.0) + jnp.log(1.0 + jnp.exp(-jnp.abs(z)))
        lk = -sp if strict is None else jnp.where(strict(lane), -sp, 0.0)
        y = lk
        d = 1
        while d < n:
            y = y + jnp.where(lane < n - d, pltpu.roll(y, n - d, 1), 0.0)
            d *= 2
        a = jnp.exp(z - sp + (y - lk) + c_ref[...])
        if strict is not None:
            a = jnp.where(strict(lane), a, 0.0)
        a = a.astype(BF16)
        c_ref[...] += y[:, 0:1]
        for g, (_, v) in enumerate(kvs):
            acc_ref[g] += jnp.dot(a[g * DEC_ROWS:(g + 1) * DEC_ROWS], v, preferred_element_type=F32)

    @pl.when(c == 0)
    def _():
        c_ref[...] = jnp.zeros(c_ref.shape, F32)
        acc_ref[...] = jnp.zeros(acc_ref.shape, F32)
        t_row = _iota((SB_KV * DEC_ROWS, 1), 0) % t_new
        tiles(lambda g: (kn_ref[0, g], vn_ref[0, g]), lambda lane: lane < t_row)

    tiles(lambda g: (_page_rows(pages, g, 0, SB_KV), _page_rows(pages, g, 1, SB_KV)), None)

    @pl.when(c == pl.num_programs(1) - 1)
    def _():
        o_ref[0] = acc_ref[...]


def sb_dec(page_table, pool, q, kn, vn, t_new):
    b, n_pages = page_table.shape
    n_per = _pages_per_step(n_pages)
    blk = lambda r: pl.BlockSpec((1, SB_KV, r, D_HEAD), lambda bi, c, pt: (bi, 0, 0, 0))
    grid_spec = pltpu.PrefetchScalarGridSpec(
        num_scalar_prefetch=1, grid=(b, n_pages // n_per),
        in_specs=[blk(DEC_ROWS), blk(PAGE_SIZE), blk(PAGE_SIZE)]
        + _page_specs(n_per, pool, n_pages=n_pages),
        out_specs=blk(DEC_ROWS),
        scratch_shapes=[pltpu.VMEM((SB_KV * DEC_ROWS, 1), F32), pltpu.VMEM((SB_KV, DEC_ROWS, D_HEAD), F32)])
    return pl.pallas_call(
        functools.partial(_sb_dec_kernel, n_per=n_per, t_new=t_new),
        grid_spec=grid_spec,
        out_shape=jax.ShapeDtypeStruct((b, SB_KV, DEC_ROWS, D_HEAD), F32),
        compiler_params=_params(("parallel", "arbitrary")),
        name="sb_dec",
    )(page_table, q, kn, vn, *([pool] * n_per))


def _nsa_cmp_dec_kernel(pt_ref, q_ref, xn_ref, w1_ref, pe_ref, w2_ref, cov_ref, exp_ref, *rest,
                        n_per, t_new, p_len, n_cmp, n_sel):
    pages = rest[:n_per]
    ocmp_ref, sel_ref = rest[n_per], rest[n_per + 1]
    lead_ref, trail_ref = rest[n_per + 2:]
    c = pl.program_id(1)
    row_w = NSA_KV * 2 * D_HEAD
    page_ch = PAGE_SIZE // CMP_STRIDE
    n_ch = page_ch * n_per
    cb = cov_ref.shape[0]
    sl = cov_ref.shape[1]

    @pl.when(c == 0)
    def _():
        lead_ref[...] = jnp.zeros(lead_ref.shape, F32)
        trail_ref[...] = jnp.zeros(trail_ref.shape, F32)

    def project(chunk_rows, base):
        for c4 in range(2 * NSA_KV):
            xc = jnp.concatenate([chunk_rows(c4 // 2, c4 % 2, r) for r in range(CMP_STRIDE)], axis=1).astype(BF16)
            lt = jnp.dot(xc, w1_ref[c4 % 2], preferred_element_type=F32)
            lead_ref[c4, pl.ds(base, xc.shape[0]), :] = lt[:, :D_HEAD]
            trail_ref[c4, pl.ds(base, xc.shape[0]), :] = lt[:, D_HEAD:]

    project(lambda g, kv, r: jnp.concatenate(
        [pg[0, pl.ds(2 * NSA_KV * r + 2 * g + kv, page_ch, stride=2 * NSA_KV * CMP_STRIDE), :] for pg in pages], axis=0),
        pl.multiple_of(c * n_ch, n_ch))

    @pl.when(c == pl.num_programs(1) - 1)
    def _():
        project(lambda g, kv, r: xn_ref[0, :, r * row_w + (2 * g + kv) * D_HEAD:r * row_w + (2 * g + kv + 1) * D_HEAD],
                p_len // CMP_STRIDE)
        qpos = p_len + _iota((DEC_ROWS, 1), 0) % t_new
        n_idx = _iota((1, cb), 1)
        c_ok = ((n_idx * CMP_STRIDE + (CMP_LEN - 1)) <= qpos) & (n_idx < n_cmp)
        rr = (_iota((DEC_ROWS, DEC_ROWS), 0) % t_new == _iota((DEC_ROWS, DEC_ROWS), 1) % t_new)
        rr = jnp.where(rr, 1.0, 0.0).astype(BF16)
        lane = _iota((1, sl), 1)
        cur = qpos // SEL_BLOCK
        forced = (lane == 0) | (lane == cur) | (lane == cur - 1)
        causal = lane * SEL_BLOCK <= qpos
        for g in range(NSA_KV):
            blocks = []
            for kv in range(2):
                c4 = g * 2 + kv
                pw = jnp.dot(pe_ref[kv], w1_ref[kv], preferred_element_type=F32)
                pe_w1 = pw[0:1, :D_HEAD] + pw[1:2, D_HEAD:]
                h = jax.nn.gelu(lead_ref[c4, 0:cb, :] + trail_ref[c4, 1:cb + 1, :] + pe_w1, approximate=True)
                blocks.append(jnp.dot(h.astype(BF16), w2_ref[kv], preferred_element_type=F32).astype(BF16))
            s = lax.dot_general(q_ref[0, g], blocks[0], _NT, preferred_element_type=F32)
            s = jnp.where(c_ok, s, NEG_INF)
            e = jnp.where(c_ok, jnp.exp(s - jnp.max(s, axis=-1, keepdims=True)), 0.0)
            den = jnp.sum(e, axis=-1, keepdims=True)
            p = e / jnp.where(den > 0.0, den, 1.0)
            ocmp_ref[0, g] = jnp.dot(p.astype(BF16), blocks[1], preferred_element_type=F32)
            hi, lo = _split_bf16(p)
            psum = jnp.dot(rr, hi, preferred_element_type=F32) + jnp.dot(rr, lo, preferred_element_type=F32)
            hi, lo = _split_bf16(psum)
            cov = cov_ref[...]
            imp = jnp.dot(hi, cov, preferred_element_type=F32) + jnp.dot(lo, cov, preferred_element_type=F32)
            score = jnp.where(forced, FORCED, jnp.where(causal, imp, NEG_INF))
            score = jnp.where(lane < n_sel, score, -3e38)
            cnt = jnp.zeros((DEC_ROWS, sl), F32)
            for sp in range(n_sel):
                col = score[:, sp:sp + 1]
                tie = jnp.where(lane > sp, 1.0, 0.0)
                cnt = cnt + jnp.where(col > score, 1.0, jnp.where(col == score, tie, 0.0))
            selm = jnp.where((cnt < float(min(SEL_TOPK, n_sel))) & (lane < n_sel), 1.0, 0.0).astype(BF16)
            sel_ref[0, g] = jnp.dot(selm, exp_ref[...], preferred_element_type=F32)


def nsa_cmp_dec(page_table, pool, q, xnew, w1lt, pe2, w2, t_new):
    b, n_pages = page_table.shape
    n_per = _pages_per_step(n_pages)
    p_len = n_pages * PAGE_SIZE
    total = p_len + SEL_BLOCK
    n_cmp = total // CMP_STRIDE - 1
    n_sel = total // SEL_BLOCK
    cb = ((p_len // CMP_STRIDE + 8 + LANE - 1) // LANE) * LANE
    sl = ((n_sel + LANE - 1) // LANE) * LANE
    assert t_new <= CMP_STRIDE and DEC_ROWS % t_new == 0
    cmp_i = _iota((cb, sl), 0)
    sel_i = _iota((cb, sl), 1)
    cover = ((cmp_i * CMP_STRIDE <= sel_i * SEL_BLOCK + SEL_BLOCK - 1)
             & (cmp_i * CMP_STRIDE + CMP_LEN - 1 >= sel_i * SEL_BLOCK)
             & (cmp_i < n_cmp) & (sel_i < n_sel)).astype(BF16)
    klen = p_len + LANE
    expand = (_iota((sl, klen), 0) == _iota((sl, klen), 1) // SEL_BLOCK).astype(BF16)
    full = lambda shape: pl.BlockSpec(shape, lambda bi, c, pt: (0,) * len(shape))
    qspec = pl.BlockSpec((1, NSA_KV, DEC_ROWS, D_HEAD), lambda bi, c, pt: (bi, 0, 0, 0))
    grid_spec = pltpu.PrefetchScalarGridSpec(
        num_scalar_prefetch=1, grid=(b, n_pages // n_per),
        in_specs=[qspec, pl.BlockSpec((1,) + xnew.shape[1:], lambda bi, c, pt: (bi, 0, 0)),
                  full(w1lt.shape), full(pe2.shape), full(w2.shape), full(cover.shape), full(expand.shape)]
        + _page_specs(n_per, pool),
        out_specs=[qspec, pl.BlockSpec((1, NSA_KV, DEC_ROWS, klen), lambda bi, c, pt: (bi, 0, 0, 0))],
        scratch_shapes=[pltpu.VMEM((2 * NSA_KV, cb + 8, D_HEAD), F32), pltpu.VMEM((2 * NSA_KV, cb + 8, D_HEAD), F32)])
    return pl.pallas_call(
        functools.partial(_nsa_cmp_dec_kernel, n_per=n_per, t_new=t_new, p_len=p_len, n_cmp=n_cmp, n_sel=n_sel),
        grid_spec=grid_spec,
        out_shape=[jax.ShapeDtypeStruct((b, NSA_KV, DEC_ROWS, D_HEAD), F32),
                   jax.ShapeDtypeStruct((b, NSA_KV, DEC_ROWS, klen), F32)],
        compiler_params=_params(("parallel", "arbitrary")),
        name="nsa_cmp_dec",
    )(page_table, q, xnew, w1lt, pe2, w2, cover, expand, *([pool] * n_per))


def _nsa_sel_dec_kernel(pt_ref, q_ref, selp_ref, seln_ref, ksn_ref, vsn_ref, ocmp_ref, ng_ref, win_ref, kwn_ref,
                        vwn_ref, *rest, n_per, t_new):
    pages, o_ref, (m_ref, l_ref, acc_ref) = rest[:n_per], rest[n_per], rest[n_per + 1:]
    c = pl.program_id(1)

    @pl.when(c == 0)
    def _():
        m_ref[...] = jnp.full(m_ref.shape, NEG_INF, F32)
        l_ref[...] = jnp.zeros(l_ref.shape, F32)
        acc_ref[...] = jnp.zeros(acc_ref.shape, F32)

    _softmax_steps(m_ref, l_ref, acc_ref, [
        (g, lax.dot_general(q_ref[0, g], _page_rows(pages, g, 0, NSA_KV), _NT, preferred_element_type=F32),
         _page_rows(pages, g, 1, NSA_KV), selp_ref[0, g] > 0.5) for g in range(NSA_KV)])

    @pl.when(c == pl.num_programs(1) - 1)
    def _():
        causal = _new_causal(t_new)
        wb = win_ref.shape[1] // (2 * NSA_KV)
        t_row = _iota((DEC_ROWS, 1), 0) % t_new
        w_ok = _iota((1, wb), 1) > t_row
        for g in range(NSA_KV):
            q = q_ref[0, g]
            s = lax.dot_general(q, ksn_ref[0, g], _NT, preferred_element_type=F32)
            _softmax_step(m_ref, l_ref, acc_ref, g, s, vsn_ref[0, g], (seln_ref[0, g, :, 0:DEC_ROWS] > 0.5) & causal)
            o_sel = acc_ref[g] / l_ref[g]
            kw = win_ref[0, pl.ds(2 * g, wb, stride=2 * NSA_KV), :].astype(BF16)
            vw = win_ref[0, pl.ds(2 * g + 1, wb, stride=2 * NSA_KV), :].astype(BF16)
            sw = jnp.where(w_ok, lax.dot_general(q, kw, _NT, preferred_element_type=F32), NEG_INF)
            sn = jnp.where(causal, lax.dot_general(q, kwn_ref[0, g], _NT, preferred_element_type=F32), NEG_INF)
            mx = jnp.maximum(jnp.max(sw, axis=-1, keepdims=True), jnp.max(sn, axis=-1, keepdims=True))
            ew = jnp.where(w_ok, jnp.exp(sw - mx), 0.0)
            en = jnp.where(causal, jnp.exp(sn - mx), 0.0)
            den = jnp.sum(ew, axis=-1, keepdims=True) + jnp.sum(en, axis=-1, keepdims=True)
            o_win = (jnp.dot(ew.astype(BF16), vw, preferred_element_type=F32)
                     + jnp.dot(en.astype(BF16), vwn_ref[0, g], preferred_element_type=F32)) / den
            gates = jax.nn.sigmoid(ng_ref[0, g])
            o_ref[0, g] = gates[:, 0:1] * ocmp_ref[0, g] + gates[:, 1:2] * o_sel + gates[:, 2:3] * o_win


def nsa_sel_dec(page_table, pool, q, selexp, ksn, vsn, o_cmp, ng, win, kwn, vwn, t_new):
    b, n_pages = page_table.shape
    n_per = _pages_per_step(n_pages)
    assert win.shape[1] == WINDOW * 2 * NSA_KV
    blk = pl.BlockSpec((1, NSA_KV, DEC_ROWS, D_HEAD), lambda bi, c, pt: (bi, 0, 0, 0))
    grid_spec = pltpu.PrefetchScalarGridSpec(
        num_scalar_prefetch=1, grid=(b, n_pages // n_per),
        in_specs=[blk, pl.BlockSpec((1, NSA_KV, DEC_ROWS, n_per * PAGE_SIZE), lambda bi, c, pt: (bi, 0, 0, c)),
                  pl.BlockSpec((1, NSA_KV, DEC_ROWS, LANE), lambda bi, c, pt: (bi, 0, 0, n_pages)),
                  blk, blk, blk, blk,
                  pl.BlockSpec((1,) + win.shape[1:], lambda bi, c, pt: (bi, 0, 0)), blk, blk]
        + _page_specs(n_per, pool),
        out_specs=blk,
        scratch_shapes=[pltpu.VMEM((NSA_KV, DEC_ROWS, 1), F32), pltpu.VMEM((NSA_KV, DEC_ROWS, 1), F32),
                        pltpu.VMEM((NSA_KV, DEC_ROWS, D_HEAD), F32)])
    return pl.pallas_call(
        functools.partial(_nsa_sel_dec_kernel, n_per=n_per, t_new=t_new),
        grid_spec=grid_spec,
        out_shape=jax.ShapeDtypeStruct((b, NSA_KV, DEC_ROWS, D_HEAD), F32),
        compiler_params=_params(("parallel", "arbitrary")),
        name="nsa_sel_dec",
    )(page_table, q, selexp, selexp, ksn, vsn, o_cmp, ng, win, kwn, vwn, *([pool] * n_per))


def _prep_weights(w_in, w_br_a, w_br_b, w_br_c, w_out, w_ff_gate, w_ff_up, w_ff_down, w_ple, w_ple_gate):
    w_main = jnp.concatenate([w_in[:, :NG_AT], w_in[:, NG_AT + NG_W:]], axis=1).astype(BF16)
    w_ng = jnp.pad(w_in[:, NG_AT:NG_AT + NG_W], ((0, 0), (0, LANE - NG_W))).astype(BF16)
    c = lambda a: a.astype(BF16)
    fpad = D_FFP - D_FF
    w_ff_gate = jnp.pad(w_ff_gate, ((0, 0), (0, fpad)))
    w_ff_up = jnp.pad(w_ff_up, ((0, 0), (0, fpad)))
    w_ff_down = jnp.pad(w_ff_down, ((0, fpad), (0, 0)))
    return (w_main, w_ng, c(w_br_a), c(w_br_b), c(w_br_c), c(w_out), c(w_ff_gate), c(w_ff_up),
            c(w_ff_down), c(w_ple), c(w_ple_gate))


def _layer(x, pe, past, li, ln1, wts, diff_lambda, diff_subln, cmp_w1, cmp_w2, cmp_pe,
           ln2, ff_conv_w, ff_conv_b, ln3):
    (w_main, w_ng, w_br_a, w_br_b, w_br_c, w_out, w_ff_gate, w_ff_up, w_ff_down, w_ple, w_ple_gate) = wts
    b, t, _ = x.shape
    m = b * t
    p_len = 0 if past is None else past[0].shape[1] * PAGE_SIZE
    pos = p_len + jnp.arange(t, dtype=jnp.int32)
    x2 = x.reshape(m, D_MODEL)
    if past is not None:
        page_table, pool_diff, pool_cmp, pool_sel, pool_sb, win_buf, conv_buf = past
        n_pool = pool_diff.shape[0]
        assert DEC_ROWS == t * NSA_HEADS // NSA_KV == 2 * t * DA_HEADS // DA_KV and DEC_ROWS >= t * SB_HEADS // SB_KV

    def dec_rows(a, rows=DEC_ROWS):
        g, d = a.shape[2], a.shape[-1]
        a = jnp.moveaxis(a, 1, -2).reshape(b, g, -1, d)
        return jnp.pad(a, ((0, 0), (0, 0), (0, rows - a.shape[2]), (0, 0))).astype(BF16)

    def dec_out(o, rn):
        g, d = o.shape[1], o.shape[-1]
        o = o[:, :, :rn * t].reshape(b, g, rn, t, d)
        return jnp.transpose(o, (0, 3, 1, 2, 4)).reshape(b, t, g * rn * d)

    proj = norm_mm(x2, ln1, w_main)
    ng = norm_mm(x2, ln1, w_ng)[:, :NG_W].reshape(b, t, NG_W)
    pos_rows = pos if t % 8 == 0 else jnp.tile(pos, b)
    (da_q, da_k, da_v, nq, nq_rot, ks, vs, kw, vw, sq, sk, sv,
     da_new, cmp_new, sel_new, win_new, sb_new) = qkv_post(proj, pos_rows)
    seq = lambda a: a.reshape(b, t, -1)
    heads = lambda a, g: a.reshape(b, t, g, -1, D_HEAD)
    da_new = da_new.reshape(b, t, DA_KV, 2, 2 * D_HEAD)
    cmp_new, sel_new, win_new = [a.reshape(b, t, NSA_KV, 2, D_HEAD) for a in (cmp_new, sel_new, win_new)]
    sb_new = sb_new.reshape(b, t, SB_KV, 2, D_HEAD)
    lam_init = 0.8 - 0.6 * math.exp(-0.3 * li)
    rn = NSA_HEADS // NSA_KV
    chunk_w = CMP_STRIDE * NSA_KV * 2 * D_HEAD
    half = CMP_STRIDE * D_HEAD
    w1lt = jnp.concatenate([cmp_w1[:, :half], cmp_w1[:, half:]], axis=2).astype(BF16)
    pe2 = jnp.pad(cmp_pe.reshape(2, 2, half), ((0, 0), (0, DEC_ROWS - 2), (0, 0))).astype(BF16)
    if past is None:
        o_da = diff_prompt(seq(da_q), seq(da_k), seq(da_v), diff_lambda, diff_subln, lam_init)
        chunks = cmp_new.reshape(b, t // CMP_STRIDE, chunk_w)
        chunks = jnp.pad(chunks, ((0, 0), (0, LANE - t // CMP_STRIDE), (0, 0)))
        kc_blk, vc_blk = cmp_prompt(chunks, w1lt, pe2, cmp_w2.astype(BF16))
        ng_pad = jnp.pad(ng.reshape(b, t, NSA_KV, 3 * rn), ((0, 0), (0, 0), (0, 0), (0, LANE - 3 * rn)))
        o_nsa = nsa_prompt(seq(nq), seq(nq_rot), kc_blk, vc_blk, seq(ks), seq(vs), seq(kw), seq(vw),
                           ng_pad.reshape(b, t, NSA_KV * LANE))
        o_sb = sb_prompt(seq(sq), seq(sk), seq(sv))
        win_all = win_new
    else:
        rows_view = lambda a: a.reshape(a.shape[0], -1, D_HEAD)
        pool_cmp, pool_sel, pool_sb, win_rows = [rows_view(a) for a in (pool_cmp, pool_sel, pool_sb, win_buf)]
        q_bd =(jnp.swapaxes(da_q.reshape(b, t, DA_KV, DA_HEADS // DA_KV, 2, D_HEAD), 3, 4)[..., None, :]
                * jnp.eye(2, dtype=BF16)[:, None, :, None])
        o = diff_dec(page_table, pool_diff,
                     dec_rows(q_bd.reshape(b, t, DA_KV, 2, DA_HEADS // DA_KV, 2 * D_HEAD)),
                     dec_rows(da_k.reshape(b, t, DA_KV, 2 * D_HEAD)), dec_rows(da_v.reshape(b, t, DA_KV, 2 * D_HEAD)),
                     diff_lambda, diff_subln, lam_init, t)
        o_da = dec_out(o, DA_HEADS // DA_KV)
        xnew = jnp.pad(cmp_new.reshape(b, 1, -1), ((0, 0), (0, 7), (0, chunk_w - t * NSA_KV * 2 * D_HEAD)))
        o_cmp, selexp = nsa_cmp_dec(page_table, pool_cmp,
                                    dec_rows(heads(nq, NSA_KV)), xnew, w1lt, pe2, cmp_w2.astype(BF16), t)
        ng_rows = jnp.moveaxis(ng.reshape(b, t, NSA_KV, rn, 3), 1, 3).reshape(b, NSA_KV, rn * t, 3)
        ng_rows = jnp.pad(ng_rows, ((0, 0), (0, 0), (0, 0), (0, LANE - 3)))
        o = nsa_sel_dec(page_table, pool_sel, dec_rows(heads(nq_rot, NSA_KV)), selexp,
                        dec_rows(heads(ks, NSA_KV)), dec_rows(heads(vs, NSA_KV)), o_cmp, ng_rows,
                        win_rows,
                        dec_rows(heads(kw, NSA_KV)), dec_rows(heads(vw, NSA_KV)), t)
        o_nsa = dec_out(o, rn)
        o = sb_dec(page_table, pool_sb, dec_rows(heads(sq, SB_KV)),
                   dec_rows(heads(sk, SB_KV), PAGE_SIZE), dec_rows(heads(sv, SB_KV), PAGE_SIZE), t)
        o_sb = dec_out(o, SB_HEADS // SB_KV)
        win_all = jnp.concatenate([win_buf, win_new], axis=1)
    win_state = win_all[:, win_all.shape[1] - min(WINDOW, win_all.shape[1]):]

    merged = merge_mm(o_da.reshape(m, DA_QW).astype(BF16), o_nsa.reshape(m, NSA_QW).astype(BF16),
                      o_sb.reshape(m, SB_QW).astype(BF16), proj, w_br_a, w_br_b, w_br_c,
                      g_col=proj.shape[1] - 3 * D_MODEL, tm=1024)
    x2 = resid_mm(x2, merged, w_out, tm=1024)

    fpad = D_FFP - D_FF
    conv_b = jnp.pad(ff_conv_b, (0, fpad))
    cw = jnp.pad(ff_conv_w, ((0, 0), (0, fpad)))
    if past is None:
        act, tail = ffn_act(x2, ln2, w_ff_gate, w_ff_up, cw, conv_b, t)
        conv_state = tail[:, 8 - (CONV_W - 1):, :D_FF]
    else:
        gate_in, up = ffn_up(x2, ln2, w_ff_gate, w_ff_up)
        gp = jnp.concatenate([jnp.pad(conv_buf, ((0, 0), (0, 0), (0, fpad))), gate_in.reshape(b, t, D_FFP)], axis=1)
        conv = conv_b
        for i in range(CONV_W):
            conv = conv + cw[i] * gp[:, i:i + t]
        act = (jax.nn.gelu(conv, approximate=True) * up.reshape(b, t, D_FFP)).astype(BF16).reshape(m, D_FFP)
        conv_state = gp[:, t:, :D_FF]
    x2 = resid_mm(x2, act, w_ff_down)

    x2 = ple_mm(x2, ln3, w_ple_gate, pe.reshape(m, -1).astype(BF16), w_ple, tm=1024)
    return x2.reshape(b, t, D_MODEL), (da_new, cmp_new, sel_new, sb_new, win_state, conv_state)


def kernel(x_prompt, x_sample, cache_diff, cache_cmp, cache_sel, cache_sb, state_win, state_conv, page_table, p_prompt, p_sample, ln1, w_in, diff_lambda, diff_subln, cmp_w1, cmp_w2, cmp_pe, w_br_a, w_br_b, w_br_c, w_out, ln2, w_ff_gate, w_ff_up, w_ff_down, ff_conv_w, ff_conv_b, ln3, w_ple, w_ple_gate, ln_f):
    n_pool = cache_diff.shape[1]
    pools = [c.reshape((DEPTH * n_pool,) + c.shape[2:]) for c in (cache_diff, cache_cmp, cache_sel, cache_sb)]
    xp, xs = x_prompt, x_sample
    st_p, st_s = [], []
    for i in range(DEPTH):
        wts = _prep_weights(w_in[i], w_br_a[i], w_br_b[i], w_br_c[i], w_out[i], w_ff_gate[i], w_ff_up[i],
                            w_ff_down[i], w_ple[i], w_ple_gate[i])
        rest = (diff_lambda[i], diff_subln[i], cmp_w1[i], cmp_w2[i], cmp_pe[i], ln2[i], ff_conv_w[i],
                ff_conv_b[i], ln3[i])
        xp, sp = _layer(xp, p_prompt[i], None, i, ln1[i], wts, *rest)
        past = (page_table + i * n_pool, *pools, state_win[i], state_conv[i])
        xs, ss = _layer(xs, p_sample[i], past, i, ln1[i], wts, *rest)
        st_p.append(sp)
        st_s.append(ss)
    diff_p, cmp_p, sel_p, sb_p, win_p, conv_p = [jnp.stack(a) for a in zip(*st_p)]
    diff_s, cmp_s, sel_s, sb_s, win_s, conv_s = [jnp.stack(a) for a in zip(*st_s)]
    y_prompt = rmsnorm_rows(xp.reshape(-1, D_MODEL), ln_f).reshape(xp.shape)
    y_sample = rmsnorm_rows(xs.reshape(-1, D_MODEL), ln_f).reshape(xs.shape)
    return (y_prompt, y_sample, diff_p, diff_s, cmp_p, cmp_s, sel_p, sel_s, sb_p, sb_s, win_p, win_s, conv_p, conv_s)
```
